```python
import jax, jax.numpy as jnp
from jax import lax
import numpy as np


D_MODEL = 1024
BATCH = 8
SEQ = 4096
DEPTH = 4
DEC_BATCH = 8
DEC_SEQ = 8192
PAST_LEN = 128

N_MIXERS = 3
N_SUB = 3
D_FF = 2816
NORM_EPS = 1e-6

MLA_HEADS = 8
MLA_Q_LORA = 384
MLA_KV_LORA = 256
MLA_NOPE = 128
MLA_ROPE = 64
MLA_V = 128
ROPE_THETA = 10000.0
Q_BLOCK = 128
MLA_SCALE = (MLA_NOPE + MLA_ROPE) ** -0.5

GDN_HEADS = 8
GDN_DK = 128
GDN_DV = 128
GDN_QK = GDN_HEADS * GDN_DK
GDN_VW = GDN_HEADS * GDN_DV
GDN_CONV = 5
GDN_CHUNK = 64
GDN_IN = 2 * GDN_QK + 2 * GDN_VW + 4 * GDN_HEADS

FNET_GROUPS = 8

N_LAYERS_A = (DEPTH + 2) // 3
N_LAYERS_B = (DEPTH + 1) // 3
N_LAYERS_C = DEPTH // 3

kernel_name = 'hybrid_mla_gdn_fnet_macaron_encoder'


def _rmsnorm(x, g):
    xf = x.astype(jnp.float32)
    y = xf * lax.rsqrt(jnp.mean(xf * xf, axis=-1, keepdims=True) + NORM_EPS)
    return (y * g.astype(jnp.float32)).astype(x.dtype)


def _l2norm(x):
    return x * lax.rsqrt(jnp.sum(x * x, axis=-1, keepdims=True) + NORM_EPS)


def _modulate(x, mod_j, g_pre):
    h = _rmsnorm(x, g_pre)
    return h * (1.0 + mod_j[:, 1][:, None, :]) + mod_j[:, 0][:, None, :]


def _residual(x, out, mod_j, g_post, weight):
    return x + weight * mod_j[:, 2][:, None, :] * _rmsnorm(out, g_post)


def _swiglu(h, w_in, w_out):
    gu = h @ w_in
    g, u = gu[..., :D_FF], gu[..., D_FF:]
    return (jax.nn.silu(g) * u) @ w_out


def _rotate(x, cos, sin):
    half = x.shape[-1] // 2
    x1, x2 = x[..., :half], x[..., half:]
    return jnp.concatenate([x1 * cos - x2 * sin, x1 * sin + x2 * cos], axis=-1)


def _mla(h, cos, sin, w_down, q_norm, kv_norm, w_uq, w_ukv, w_out):
    B, S, _ = h.shape
    H = MLA_HEADS
    down = h @ w_down
    cq = _rmsnorm(down[..., :MLA_Q_LORA], q_norm)
    ckv = _rmsnorm(down[..., MLA_Q_LORA:MLA_Q_LORA + MLA_KV_LORA], kv_norm)
    k_rope = _rotate(down[..., MLA_Q_LORA + MLA_KV_LORA:], cos, sin)
    q = (cq @ w_uq).reshape(B, S, H, MLA_NOPE + MLA_ROPE)
    q_nope = q[..., :MLA_NOPE]
    q_rope = _rotate(q[..., MLA_NOPE:], cos[:, None, :], sin[:, None, :])
    kv = (ckv @ w_ukv).reshape(B, S, H, MLA_NOPE + MLA_V)
    k_nope, v = kv[..., :MLA_NOPE], kv[..., MLA_NOPE:]
    nb = S // Q_BLOCK

    def to_blocks(t):
        return jnp.moveaxis(t.reshape(B, nb, Q_BLOCK, *t.shape[2:]), 1, 0)

    def attend(args):
        qn_b, qr_b = args
        s = (jnp.einsum('bqhd,bkhd->bhqk', qn_b, k_nope)
             + jnp.einsum('bqhr,bkr->bhqk', qr_b, k_rope))
        p = jax.nn.softmax(s.astype(jnp.float32) * MLA_SCALE, axis=-1).astype(v.dtype)
        return jnp.einsum('bhqk,bkhd->bqhd', p, v)

    o = lax.map(attend, (to_blocks(q_nope), to_blocks(q_rope)))
    o = jnp.moveaxis(o, 0, 1).reshape(B, S, H * MLA_V)
    return o @ w_out


def _centred_conv(x, w):
    K = w.shape[0]
    pad = K // 2
    S = x.shape[1]
    xp = jnp.pad(x, ((0, 0), (pad, pad), (0, 0)))
    return sum(xp[:, t:t + S] * w[t] for t in range(K))


def _delta_chunked(q, k, v, g, beta):
    B, H, S, DK = q.shape
    DV = v.shape[-1]
    C = GDN_CHUNK
    N = S // C
    q = q * (DK ** -0.5)

    def rs(t):
        return t.reshape(B, H, N, C, *t.shape[3:])

    q, k, v, g, beta = rs(q), rs(k), rs(v), rs(g), rs(beta)
    g = jnp.cumsum(g, axis=-1)
    tril = jnp.tril(jnp.ones((C, C), dtype=bool))
    strict = jnp.tril(jnp.ones((C, C), dtype=bool), -1)
    diff = g[..., :, None] - g[..., None, :]
    decay_mat = jnp.where(tril, jnp.exp(jnp.where(tril, diff, 0.0)), 0.0)
    kb = k * beta[..., None]
    m = jnp.where(strict, jnp.einsum('bhnid,bhnjd->bhnij', kb, k) * decay_mat, 0.0)
    a = m + jnp.eye(C, dtype=q.dtype)
    rhs = jnp.concatenate([v * beta[..., None], kb * jnp.exp(g)[..., None]], axis=-1)
    sol = lax.linalg.triangular_solve(a, rhs, left_side=True, lower=True, unit_diagonal=True)
    u, w = sol[..., :DV], sol[..., DV:]
    qk = jnp.einsum('bhnid,bhnjd->bhnij', q, k) * decay_mat
    g_last = g[..., -1]
    k_dec = k * jnp.exp(g_last[..., None] - g)[..., None]
    q_dec = q * jnp.exp(g)[..., None]

    def step(state, xs):
        qd, qkc, uc, wc, kd, gl = xs
        v_new = uc - jnp.einsum('bhcd,bhde->bhce', wc, state)
        o = jnp.einsum('bhcd,bhde->bhce', qd, state) + jnp.einsum('bhij,bhje->bhie', qkc, v_new)
        state = state * jnp.exp(gl)[..., None, None] + jnp.einsum('bhcd,bhce->bhde', kd, v_new)
        return state, o

    xs = tuple(jnp.moveaxis(t, 2, 0) for t in (q_dec, qk, u, w, k_dec, g_last))
    s0 = jnp.zeros((B, H, DK, DV), dtype=q.dtype)
    _, o = lax.scan(step, s0, xs)
    return jnp.moveaxis(o, 0, 2).reshape(B, H, S, DV)


def _gdn(h, w_in, conv_w, a_log, dt_bias, o_norm, w_out):
    B, S, _ = h.shape
    H = GDN_HEADS
    proj = h @ w_in
    n_conv = 2 * GDN_QK + GDN_VW
    qkv = jax.nn.silu(_centred_conv(proj[..., :n_conv], conv_w))
    q = qkv[..., :GDN_QK].reshape(B, S, H, GDN_DK)
    k = qkv[..., GDN_QK:2 * GDN_QK].reshape(B, S, H, GDN_DK)
    v = qkv[..., 2 * GDN_QK:].reshape(B, S, H, GDN_DV)
    off = n_conv
    gate = proj[..., off:off + GDN_VW].reshape(B, S, H, GDN_DV)
    off += GDN_VW
    a = proj[..., off:off + 2 * H].reshape(B, S, 2, H).astype(jnp.float32)
    b = proj[..., off + 2 * H:].reshape(B, S, 2, H).astype(jnp.float32)
    g = -jnp.exp(a_log.astype(jnp.float32)) * jax.nn.softplus(a + dt_bias.astype(jnp.float32))
    beta = jax.nn.sigmoid(b)
    g = jnp.transpose(g, (2, 0, 3, 1))
    beta = jnp.transpose(beta, (2, 0, 3, 1))

    def to_bhs(t):
        return jnp.transpose(t, (0, 2, 1, 3)).astype(jnp.float32)

    q, k, v = _l2norm(to_bhs(q)), _l2norm(to_bhs(k)), to_bhs(v)

    def flip(t):
        return jnp.flip(t, axis=2)

    o_fw = _delta_chunked(q, k, v, g[0], beta[0])
    o_bw = flip(_delta_chunked(flip(q), flip(k), flip(v), flip(g[1]), flip(beta[1])))
    o = jnp.transpose(o_fw + o_bw, (0, 2, 1, 3)).astype(h.dtype)
    o = _rmsnorm(o, o_norm) * jax.nn.silu(gate)
    return o.reshape(B, S, GDN_VW) @ w_out


def _fnet(h, w_out, b_out):
    B, S, D = h.shape
    hg = h.astype(jnp.float32).reshape(B, S, FNET_GROUPS, D // FNET_GROUPS)
    f = jnp.fft.fft2(hg, axes=(1, 3), norm='ortho').real
    return f.reshape(B, S, D).astype(h.dtype) @ w_out + b_out


def _trunk(x, c, prm):
    B, S, D = x.shape
    half = MLA_ROPE // 2
    pos = jnp.arange(S, dtype=jnp.float32)
    inv_freq = ROPE_THETA ** (-jnp.arange(half, dtype=jnp.float32) / half)
    ang = pos[:, None] * inv_freq[None, :]
    cos, sin = jnp.cos(ang).astype(x.dtype), jnp.sin(ang).astype(x.dtype)
    sc = jax.nn.silu(c)
    for l in range(DEPTH):
        i = l // N_MIXERS
        kind = l % N_MIXERS
        mod = (sc @ prm['w_ada'][l] + prm['b_ada'][l]).reshape(B, N_SUB, 3, D)
        h = _modulate(x, mod[:, 0], prm['norm_pre'][l, 0])
        out = _swiglu(h, prm['ffn_w_in'][l, 0], prm['ffn_w_out'][l, 0])
        x = _residual(x, out, mod[:, 0], prm['norm_post'][l, 0], 0.5)
        h = _modulate(x, mod[:, 1], prm['norm_pre'][l, 1])
        if kind == 0:
            out = _mla(h, cos, sin, prm['mla_w_down'][i], prm['mla_q_norm'][i], prm['mla_kv_norm'][i],
                       prm['mla_w_uq'][i], prm['mla_w_ukv'][i], prm['mla_w_out'][i])
        elif kind == 1:
            out = _gdn(h, prm['gdn_w_in'][i], prm['gdn_conv'][i], prm['gdn_a_log'][i],
                       prm['gdn_dt_bias'][i], prm['gdn_o_norm'][i], prm['gdn_w_out'][i])
        else:
            out = _fnet(h, prm['fnet_w_out'][i], prm['fnet_b_out'][i])
        x = _residual(x, out, mod[:, 1], prm['norm_post'][l, 1], 1.0)
        h = _modulate(x, mod[:, 2], prm['norm_pre'][l, 2])
        out = _swiglu(h, prm['ffn_w_in'][l, 1], prm['ffn_w_out'][l, 1])
        x = _residual(x, out, mod[:, 2], prm['norm_post'][l, 2], 0.5)
    return x


def _normal(key, shape, scale):
    return jax.random.normal(key, shape, jnp.float32) * scale


def setup_inputs(seed: int = 0) -> dict:
    key = jax.random.key(seed)
    ks = jax.random.split(key, 26)
    D = D_MODEL
    H = GDN_HEADS
    dt = jnp.exp(jax.random.uniform(ks[18], (N_LAYERS_B, 2, H), jnp.float32,
                                    minval=np.log(1e-3), maxval=np.log(1e-1)))
    return {
        'x_prompt': _normal(ks[0], (BATCH, SEQ, D), 1.0),
        'x_sample': _normal(ks[1], (DEC_BATCH, DEC_SEQ, D), 1.0),
        'c_prompt': _normal(ks[2], (BATCH, D), 1.0),
        'c_sample': _normal(ks[3], (DEC_BATCH, D), 1.0),
        'w_ada': _normal(ks[4], (DEPTH, D, N_SUB * 3 * D), 0.5 * D ** -0.5),
        'b_ada': _normal(ks[5], (DEPTH, N_SUB * 3 * D), 0.02),
        'norm_pre': 1.0 + _normal(ks[6], (DEPTH, N_SUB, D), 0.05),
        'norm_post': 1.0 + _normal(ks[7], (DEPTH, N_SUB, D), 0.05),
        'ffn_w_in': _normal(ks[8], (DEPTH, 2, D, 2 * D_FF), D ** -0.5),
        'ffn_w_out': _normal(ks[9], (DEPTH, 2, D_FF, D), D_FF ** -0.5),
        'mla_w_down': _normal(ks[10], (N_LAYERS_A, D, MLA_Q_LORA + MLA_KV_LORA + MLA_ROPE), D ** -0.5),
        'mla_q_norm': 1.0 + _normal(ks[11], (N_LAYERS_A, MLA_Q_LORA), 0.05),
        'mla_kv_norm': 1.0 + _normal(ks[12], (N_LAYERS_A, MLA_KV_LORA), 0.05),
        'mla_w_uq': _normal(ks[13], (N_LAYERS_A, MLA_Q_LORA, MLA_HEADS * (MLA_NOPE + MLA_ROPE)), MLA_Q_LORA ** -0.5),
        'mla_w_ukv': _normal(ks[14], (N_LAYERS_A, MLA_KV_LORA, MLA_HEADS * (MLA_NOPE + MLA_V)), MLA_KV_LORA ** -0.5),
        'mla_w_out': _normal(ks[15], (N_LAYERS_A, MLA_HEADS * MLA_V, D), (MLA_HEADS * MLA_V) ** -0.5),
        'gdn_w_in': _normal(ks[16], (N_LAYERS_B, D, GDN_IN), D ** -0.5),
        'gdn_conv': _normal(ks[17], (N_LAYERS_B, GDN_CONV, 2 * GDN_QK + GDN_VW), GDN_CONV ** -0.5),
        'gdn_a_log': jnp.log(jax.random.uniform(ks[19], (N_LAYERS_B, 2, H), jnp.float32, minval=1.0, maxval=16.0)),
        'gdn_dt_bias': dt + jnp.log(-jnp.expm1(-dt)),
        'gdn_o_norm': 1.0 + _normal(ks[20], (N_LAYERS_B, GDN_DV), 0.05),
        'gdn_w_out': _normal(ks[21], (N_LAYERS_B, GDN_VW, D), GDN_VW ** -0.5),
        'fnet_w_out': _normal(ks[22], (N_LAYERS_C, D, D), D ** -0.5),
        'fnet_b_out': _normal(ks[23], (N_LAYERS_C, D), 0.02),
    }


def reference(x_prompt, x_sample, c_prompt, c_sample, w_ada, b_ada, norm_pre, norm_post,
              ffn_w_in, ffn_w_out, mla_w_down, mla_q_norm, mla_kv_norm, mla_w_uq, mla_w_ukv,
              mla_w_out, gdn_w_in, gdn_conv, gdn_a_log, gdn_dt_bias, gdn_o_norm, gdn_w_out,
              fnet_w_out, fnet_b_out):
    prm = {
        'w_ada': w_ada, 'b_ada': b_ada, 'norm_pre': norm_pre, 'norm_post': norm_post,
        'ffn_w_in': ffn_w_in, 'ffn_w_out': ffn_w_out,
        'mla_w_down': mla_w_down, 'mla_q_norm': mla_q_norm, 'mla_kv_norm': mla_kv_norm,
        'mla_w_uq': mla_w_uq, 'mla_w_ukv': mla_w_ukv, 'mla_w_out': mla_w_out,
        'gdn_w_in': gdn_w_in, 'gdn_conv': gdn_conv, 'gdn_a_log': gdn_a_log,
        'gdn_dt_bias': gdn_dt_bias, 'gdn_o_norm': gdn_o_norm, 'gdn_w_out': gdn_w_out,
        'fnet_w_out': fnet_w_out, 'fnet_b_out': fnet_b_out,
    }
    y_prompt = _trunk(x_prompt, c_prompt, prm)
    y_sample = _trunk(x_sample, c_sample, prm)
    return (y_prompt, y_sample)
```

```python
import functools
import math

import jax
import jax.numpy as jnp
from jax import lax
from jax.experimental import pallas as pl
from jax.experimental.pallas import tpu as pltpu

F32 = jnp.float32
BF16 = jnp.bfloat16

N_SUB = 3
D_FF = 2816
NORM_EPS = 1e-6

MLA_HEADS = 8
MLA_Q_LORA = 384
MLA_KV_LORA = 256
MLA_NOPE = 128
MLA_ROPE = 64
MLA_V = 128
ROPE_THETA = 10000.0
MLA_SCALE = (MLA_NOPE + MLA_ROPE) ** -0.5
MLA_QK_PAD = 256

GDN_HEADS = 8
GDN_DK = 128
GDN_DV = 128
GDN_QK = GDN_HEADS * GDN_DK
GDN_VW = GDN_HEADS * GDN_DV
GDN_CONV = 5
GDN_CHUNK = 64
GDN_HALO = 8

FNET_GROUPS = 8

LANES = 128
VMEM_LIMIT = 56 * 1024 * 1024


def _cparams(sem):
    return pltpu.CompilerParams(dimension_semantics=sem, vmem_limit_bytes=VMEM_LIMIT)


def _const_spec(shape):
    nd = len(shape)
    return pl.BlockSpec(shape, lambda *_: (0,) * nd, pipeline_mode=pl.Buffered(1))


def _rms(x, g):
    ms = jnp.mean(x * x, axis=-1, keepdims=True)
    return x * lax.rsqrt(ms + NORM_EPS) * g


def _modulated(x, mod_ref, gpre_ref, sub):
    shift = mod_ref[0, 3 * sub:3 * sub + 1, :]
    scale = mod_ref[0, 3 * sub + 1:3 * sub + 2, :]
    return _rms(x, gpre_ref[...]) * (1.0 + scale) + shift


def _residual(x, out, mod_ref, gpost_ref, sub, weight):
    gate = mod_ref[0, 3 * sub + 2:3 * sub + 3, :]
    return x + (weight * gate) * _rms(out, gpost_ref[...])


def _dot(a, b):
    return jnp.dot(a, b, preferred_element_type=F32)


def _dot_nt(a, b):
    return lax.dot_general(a, b, (((1,), (1,)), ((), ())), preferred_element_type=F32)


def _dot_tn(a, b):
    return lax.dot_general(a, b, (((0,), (0,)), ((), ())), preferred_element_type=F32)


def _ada_kernel(c_ref, w_ref, b_ref, o_ref):
    c = c_ref[...]
    sc = (c * jax.nn.sigmoid(c)).astype(BF16)
    o_ref[0] = _dot(sc, w_ref[0].astype(BF16)) + b_ref[0]


def _ada_mod(c_all, w_ada, b_ada):
    nb, d = c_all.shape
    depth, _, n_out = w_ada.shape
    tn = n_out // 8
    return pl.pallas_call(
        _ada_kernel,
        grid=(depth, n_out // tn),
        in_specs=[
            pl.BlockSpec((nb, d), lambda l, j: (0, 0)),
            pl.BlockSpec((1, d, tn), lambda l, j: (l, 0, j)),
            pl.BlockSpec((1, 1, tn), lambda l, j: (l, 0, j)),
        ],
        out_specs=pl.BlockSpec((1, nb, tn), lambda l, j: (l, 0, j)),
        out_shape=jax.ShapeDtypeStruct((depth, nb, n_out), F32),
        compiler_params=_cparams(("arbitrary", "arbitrary")),
        name="ada_mod",
    )(c_all, w_ada, b_ada.reshape(depth, 1, n_out))


def _ffn_kernel(x_ref, mod_ref, gpre_ref, gpost_ref, win_ref, wout_ref, o_ref, act_ref,
                *, sub, ck):
    x = x_ref[...]
    hb = _modulated(x, mod_ref, gpre_ref, sub).astype(BF16)
    for c in range(D_FF // ck):
        g = _dot(hb, win_ref[:, c * ck:(c + 1) * ck])
        u = _dot(hb, win_ref[:, D_FF + c * ck:D_FF + (c + 1) * ck])
        act_ref[:, c * ck:(c + 1) * ck] = (g * jax.nn.sigmoid(g) * u).astype(BF16)
    out = _dot(act_ref[...], wout_ref[...])
    o_ref[...] = _residual(x, out, mod_ref, gpost_ref, sub, 0.5)


def _ffn(x, mod, gpre, gpost, w_in, w_out, *, sub, seq, tm):
    t, d = x.shape
    return pl.pallas_call(
        functools.partial(_ffn_kernel, sub=sub, ck=256),
        grid=(t // tm,),
        in_specs=[
            pl.BlockSpec((tm, d), lambda i: (i, 0)),
            pl.BlockSpec((1, 3 * N_SUB, d), lambda i: ((i * tm) // seq, 0, 0)),
            _const_spec((1, d)),
            _const_spec((1, d)),
            _const_spec(w_in.shape),
            _const_spec(w_out.shape),
        ],
        out_specs=pl.BlockSpec((tm, d), lambda i: (i, 0)),
        out_shape=jax.ShapeDtypeStruct((t, d), F32),
        scratch_shapes=[pltpu.VMEM((tm, D_FF), BF16)],
        compiler_params=_cparams(("parallel",)),
        name="ffn",
    )(x, mod, gpre, gpost, w_in, w_out)


def _rope_pair(t):
    return t + pltpu.roll(t, 64, axis=1)


def _mla_proj_kernel(x_ref, mod_ref, gpre_ref, trig_ref, wd_ref, qn_ref, kvn_ref, wq_ref, wkv_ref,
                     q_ref, k_ref, v_ref):
    x = x_ref[...]
    hb = _modulated(x, mod_ref, gpre_ref, 1).astype(BF16)
    down = _dot(hb, wd_ref[...])
    cq = _rms(down[:, :MLA_Q_LORA], qn_ref[...]).astype(BF16)
    ckv = _rms(down[:, MLA_Q_LORA:MLA_Q_LORA + MLA_KV_LORA], kvn_ref[...]).astype(BF16)
    trig = trig_ref[...]
    lane = lax.broadcasted_iota(jnp.int32, trig.shape, 1)
    k_rope = _rope_pair(down[:, MLA_Q_LORA + MLA_KV_LORA:] * trig)
    k_rope = jnp.where(lane < MLA_ROPE, k_rope, 0.0).astype(BF16)
    q = _dot(cq, wq_ref[...])
    kv = _dot(ckv, wkv_ref[...])
    qs = MLA_SCALE * math.log2(math.e)
    for h in range(MLA_HEADS):
        c0 = h * MLA_QK_PAD
        q_ref[:, c0:c0 + LANES] = (q[:, c0:c0 + LANES] * qs).astype(BF16)
        q_rope = _rope_pair(q[:, c0 + LANES:c0 + 2 * LANES] * trig)
        q_ref[:, c0 + LANES:c0 + 2 * LANES] = (q_rope * qs).astype(BF16)
        k_ref[:, c0:c0 + LANES] = kv[:, h * MLA_NOPE:(h + 1) * MLA_NOPE].astype(BF16)
        k_ref[:, c0 + LANES:c0 + 2 * LANES] = k_rope
    v_ref[...] = kv[:, MLA_HEADS * MLA_NOPE:].astype(BF16)


def _mla_proj(x, mod, gpre, trig, wd, qn, kvn, wq, wkv, *, seq, tm):
    t, d = x.shape
    nq = MLA_HEADS * MLA_QK_PAD
    nv = MLA_HEADS * MLA_V
    return pl.pallas_call(
        _mla_proj_kernel,
        grid=(t // tm,),
        in_specs=[
            pl.BlockSpec((tm, d), lambda i: (i, 0)),
            pl.BlockSpec((1, 3 * N_SUB, d), lambda i: ((i * tm) // seq, 0, 0)),
            _const_spec((1, d)),
            pl.BlockSpec((tm, LANES), lambda i: (i % (seq // tm), 0)),
            _const_spec(wd.shape),
            _const_spec(qn.shape),
            _const_spec(kvn.shape),
            _const_spec(wq.shape),
            _const_spec(wkv.shape),
        ],
        out_specs=[
            pl.BlockSpec((tm, nq), lambda i: (i, 0)),
            pl.BlockSpec((tm, nq), lambda i: (i, 0)),
            pl.BlockSpec((tm, nv), lambda i: (i, 0)),
        ],
        out_shape=[
            jax.ShapeDtypeStruct((t, nq), BF16),
            jax.ShapeDtypeStruct((t, nq), BF16),
            jax.ShapeDtypeStruct((t, nv), BF16),
        ],
        compiler_params=_cparams(("parallel",)),
        name="mla_proj",
    )(x, mod, gpre, trig, wd, qn, kvn, wq, wkv)


def _flash_kernel(q_ref, k_ref, v_ref, o_ref, *, tk):
    q = q_ref[0]
    tq = q.shape[0]
    nk = k_ref.shape[1] // tk

    def body(j, carry):
        m, l, acc = carry
        r0 = pl.multiple_of(j * tk, tk)
        k = k_ref[0, pl.ds(r0, tk), :]
        v = v_ref[0, pl.ds(r0, tk), :]
        s = _dot_nt(q, k)
        m_new = jnp.maximum(m, jnp.max(s, axis=-1, keepdims=True))
        alpha = jnp.exp2(m - m_new)
        p = jnp.exp2(s - m_new)
        l = alpha * l + jnp.sum(p, axis=-1, keepdims=True)
        acc = alpha * acc + _dot(p.astype(BF16), v)
        return m_new, l, acc

    m0 = jnp.full((tq, 1), -1e30, F32)
    l0 = jnp.zeros((tq, 1), F32)
    a0 = jnp.zeros((tq, MLA_V), F32)
    _, l, acc = lax.fori_loop(0, nk, body, (m0, l0, a0))
    o_ref[0] = (acc / l).astype(o_ref.dtype)


def _flash(q, k, v, *, tq, tk):
    b, s, _ = q.shape
    return pl.pallas_call(
        functools.partial(_flash_kernel, tk=tk),
        grid=(b, MLA_HEADS, s // tq),
        in_specs=[
            pl.BlockSpec((1, tq, MLA_QK_PAD), lambda bi, h, i: (bi, i, h)),
            pl.BlockSpec((1, s, MLA_QK_PAD), lambda bi, h, i: (bi, 0, h)),
            pl.BlockSpec((1, s, MLA_V), lambda bi, h, i: (bi, 0, h)),
        ],
        out_specs=pl.BlockSpec((1, tq, MLA_V), lambda bi, h, i: (bi, i, h)),
        out_shape=jax.ShapeDtypeStruct((b, s, MLA_HEADS * MLA_V), BF16),
        compiler_params=_cparams(("parallel", "parallel", "arbitrary")),
        name="mla_flash",
    )(q, k, v)


def _out_proj_kernel(a_ref, x_ref, mod_ref, gpost_ref, w_ref, o_ref):
    out = _dot(a_ref[...], w_ref[...])
    o_ref[...] = _residual(x_ref[...], out, mod_ref, gpost_ref, 1, 1.0)


def _out_proj(a, x, mod, gpost, w, *, seq, tm):
    t, d = x.shape
    return pl.pallas_call(
        _out_proj_kernel,
        grid=(t // tm,),
        in_specs=[
            pl.BlockSpec((tm, a.shape[1]), lambda i: (i, 0)),
            pl.BlockSpec((tm, d), lambda i: (i, 0)),
            pl.BlockSpec((1, 3 * N_SUB, d), lambda i: ((i * tm) // seq, 0, 0)),
            _const_spec((1, d)),
            _const_spec(w.shape),
        ],
        out_specs=pl.BlockSpec((tm, d), lambda i: (i, 0)),
        out_shape=jax.ShapeDtypeStruct((t, d), F32),
        compiler_params=_cparams(("parallel",)),
        name="out_proj",
    )(a, x, mod, gpost, w)


def _gdn_proj_kernel(xp_ref, x_ref, xn_ref, mod_ref, gpre_ref, wqkv_ref, wg_ref, wab_ref, conv_ref,
                     aneg_ref, dtb_ref, q_ref, k_ref, v_ref, gate_ref, gb_ref, p_ref,
                     *, tiles_per_seq):
    i = pl.program_id(0)
    tm = x_ref.shape[0]
    first = (i % tiles_per_seq) == 0
    last = (i % tiles_per_seq) == tiles_per_seq - 1

    def proj_rows(xr):
        return _dot(_modulated(xr, mod_ref, gpre_ref, 1).astype(BF16), wqkv_ref[...])

    hb = _modulated(x_ref[...], mod_ref, gpre_ref, 1).astype(BF16)
    p_ref[0:GDN_HALO, :] = jnp.where(first, 0.0, proj_rows(xp_ref[...]))
    p_ref[GDN_HALO:GDN_HALO + tm, :] = _dot(hb, wqkv_ref[...])
    p_ref[GDN_HALO + tm:, :] = jnp.where(last, 0.0, proj_rows(xn_ref[...]))

    pad = GDN_CONV // 2
    acc = None
    for tap in range(GDN_CONV):
        r0 = GDN_HALO - pad + tap
        term = p_ref[r0:r0 + tm, :] * conv_ref[tap:tap + 1, :]
        acc = term if acc is None else acc + term
    qkv = acc * jax.nn.sigmoid(acc)

    def l2n(z):
        return z * lax.rsqrt(jnp.sum(z * z, axis=-1, keepdims=True) + NORM_EPS)

    for h in range(GDN_HEADS):
        c0 = h * GDN_DK
        q_ref[:, c0:c0 + GDN_DK] = l2n(qkv[:, c0:c0 + GDN_DK]) * (GDN_DK ** -0.5)
        k_ref[:, c0:c0 + GDN_DK] = l2n(qkv[:, GDN_QK + c0:GDN_QK + c0 + GDN_DK])
    v_ref[...] = qkv[:, 2 * GDN_QK:]
    gate_ref[...] = _dot(hb, wg_ref[...])
    ab = _dot(hb, wab_ref[...])
    z = ab + dtb_ref[...]
    softplus = jnp.maximum(z, 0.0) + jnp.log(1.0 + jnp.exp(-jnp.abs(z)))
    lane = lax.broadcasted_iota(jnp.int32, ab.shape, 1)
    gb_ref[...] = jnp.where(lane < 2 * GDN_HEADS, aneg_ref[...] * softplus, jax.nn.sigmoid(ab))


def _gdn_proj(x, mod, gpre, wqkv, wg, wab, conv, aneg, dtb, *, seq, tm):
    t, d = x.shape
    hb = tm // GDN_HALO
    nblk8 = t // GDN_HALO
    n_conv = wqkv.shape[1]
    return pl.pallas_call(
        functools.partial(_gdn_proj_kernel, tiles_per_seq=seq // tm),
        grid=(t // tm,),
        in_specs=[
            pl.BlockSpec((GDN_HALO, d), lambda i: (jnp.maximum(i * hb - 1, 0), 0)),
            pl.BlockSpec((tm, d), lambda i: (i, 0)),
            pl.BlockSpec((GDN_HALO, d), lambda i: (jnp.minimum((i + 1) * hb, nblk8 - 1), 0)),
            pl.BlockSpec((1, 3 * N_SUB, d), lambda i: ((i * tm) // seq, 0, 0)),
            _const_spec((1, d)),
            _const_spec(wqkv.shape),
            _const_spec(wg.shape),
            _const_spec(wab.shape),
            _const_spec(conv.shape),
            _const_spec(aneg.shape),
            _const_spec(dtb.shape),
        ],
        out_specs=[
            pl.BlockSpec((tm, GDN_QK), lambda i: (i, 0)),
            pl.BlockSpec((tm, GDN_QK), lambda i: (i, 0)),
            pl.BlockSpec((tm, GDN_VW), lambda i: (i, 0)),
            pl.BlockSpec((tm, GDN_VW), lambda i: (i, 0)),
            pl.BlockSpec((tm, LANES), lambda i: (i, 0)),
        ],
        out_shape=[
            jax.ShapeDtypeStruct((t, GDN_QK), F32),
            jax.ShapeDtypeStruct((t, GDN_QK), F32),
            jax.ShapeDtypeStruct((t, GDN_VW), F32),
            jax.ShapeDtypeStruct((t, GDN_VW), F32),
            jax.ShapeDtypeStruct((t, LANES), F32),
        ],
        scratch_shapes=[pltpu.VMEM((tm + 2 * GDN_HALO, n_conv), F32)],
        compiler_params=_cparams(("parallel",)),
        name="gdn_proj",
    )(x, x, x, mod, gpre, wqkv, wg, wab, conv, aneg, dtb)


def _gdn_cumsum_kernel(gb_ref, o_ref):
    g = gb_ref[...]
    tm = g.shape[0]
    row = lax.broadcasted_iota(jnp.int32, (tm, tm), 0)
    col = lax.broadcasted_iota(jnp.int32, (tm, tm), 1)
    sh = int(math.log2(GDN_CHUNK))
    same = lax.shift_right_logical(row, sh) == lax.shift_right_logical(col, sh)
    lower = jnp.where(same & (col <= row), 1.0, 0.0).astype(BF16)
    upper = jnp.where(same & (col >= row), 1.0, 0.0).astype(BF16)
    g1 = g.astype(BF16)
    r1 = g - g1.astype(F32)
    g2 = r1.astype(BF16)
    g3 = (r1 - g2.astype(F32)).astype(BF16)
    pre = _dot(lower, g1) + _dot(lower, g2) + _dot(lower, g3)
    suf = _dot(upper, g1) + _dot(upper, g2) + _dot(upper, g3)
    lane = lax.broadcasted_iota(jnp.int32, g.shape, 1)
    o_ref[...] = jnp.where(lane < GDN_HEADS, pre, jnp.where(lane < 2 * GDN_HEADS, suf, g))


def _gdn_cumsum(gb, *, tm):
    t = gb.shape[0]
    return pl.pallas_call(
        _gdn_cumsum_kernel,
        grid=(t // tm,),
        in_specs=[pl.BlockSpec((tm, LANES), lambda i: (i, 0))],
        out_specs=pl.BlockSpec((tm, LANES), lambda i: (i, 0)),
        out_shape=jax.ShapeDtypeStruct((t, LANES), F32),
        compiler_params=_cparams(("parallel",)),
        name="gdn_cumsum",
    )(gb)


def _gdn_delta_kernel(q_ref, k_ref, v_ref, gcol_ref, bcol_ref, grow_ref, o_ref, state_ref,
                      *, reverse):
    c = GDN_CHUNK
    nchunk = q_ref.shape[0] // c

    @pl.when(pl.program_id(1) == 0)
    def _():
        state_ref[...] = jnp.zeros_like(state_ref)

    row = lax.broadcasted_iota(jnp.int32, (c, c), 0)
    col = lax.broadcasted_iota(jnp.int32, (c, c), 1)
    if reverse:
        row, col = col, row
    tri = row >= col
    strict = row > col
    eye = jnp.where(row == col, 1.0, 0.0)
    level_masks = []
    for lv in range(int(math.log2(c))):
        rb = lax.shift_right_logical(row, lv)
        cb = lax.shift_right_logical(col, lv)
        level_masks.append((lax.shift_right_logical(rb, 1) == lax.shift_right_logical(cb, 1))
                           & ((rb & 1) == 1) & ((cb & 1) == 0))
    last_row = 0 if reverse else c - 1

    def chunk_body(ci, carry):
        cc = (nchunk - 1 - ci) if reverse else ci
        r0 = pl.multiple_of(cc * c, c)
        gcol = gcol_ref[0, pl.ds(r0, c), :]
        bcol = bcol_ref[0, pl.ds(r0, c), :]
        grow = grow_ref[0, cc]
        for h in range(GDN_HEADS):
            cs = slice(h * GDN_DK, (h + 1) * GDN_DK)
            qh = q_ref[pl.ds(r0, c), cs]
            kh = k_ref[pl.ds(r0, c), cs]
            vh = v_ref[pl.ds(r0, c), cs]
            gcb = jnp.broadcast_to(gcol[:, h:h + 1], (c, GDN_DK))
            beta = jnp.broadcast_to(bcol[:, h:h + 1], (c, GDN_DK))
            grb = jnp.broadcast_to(grow[h:h + 1, :], (c, c))
            decay = jnp.where(tri, jnp.exp(jnp.where(tri, gcb[:, :c] - grb, 0.0)), 0.0)
            kb = kh * beta
            khb = kh.astype(BF16)
            a2 = _dot_nt(jnp.concatenate([kb, qh], axis=0).astype(BF16), khb)
            m = jnp.where(strict, a2[:c] * decay, 0.0)
            qk = a2[c:] * decay
            tinv = eye - jnp.where(level_masks[0], m, 0.0)
            for lm in level_masks[1:]:
                off = jnp.where(lm, m, 0.0).astype(BF16)
                tb = tinv.astype(BF16)
                tinv = tinv - _dot(_dot(tb, off).astype(BF16), tb)
            eg = jnp.exp(gcb)
            rhs = jnp.concatenate([vh * beta, kb * eg], axis=1).astype(BF16)
            sol = _dot(tinv.astype(BF16), rhs)
            u = sol[:, :GDN_DV]
            w = sol[:, GDN_DV:]
            g_last = gcb[last_row:last_row + 1, :]
            k_dec = kh * jnp.exp(g_last - gcb)
            q_dec = qh * eg
            st = state_ref[h]
            wq = _dot(jnp.concatenate([w, q_dec], axis=0).astype(BF16), st.astype(BF16))
            v_new = u - wq[:c]
            vnb = v_new.astype(BF16)
            o_ref[pl.ds(r0, c), cs] = wq[c:] + _dot(qk.astype(BF16), vnb)
            state_ref[h] = st * jnp.exp(g_last) + _dot_tn(k_dec.astype(BF16), vnb)
        return carry

    lax.fori_loop(0, nchunk, chunk_body, 0)


def _gdn_delta(q, k, v, gcol, bcol, grow, *, batch, seq, tb, reverse):
    t = q.shape[0]
    nb = seq // tb
    cpb = tb // GDN_CHUNK

    def blk(bi, i):
        return bi * nb + ((nb - 1 - i) if reverse else i)

    def seq_blk(bi, i):
        return (nb - 1 - i) if reverse else i

    return pl.pallas_call(
        functools.partial(_gdn_delta_kernel, reverse=reverse),
        grid=(batch, nb),
        in_specs=[
            pl.BlockSpec((tb, GDN_QK), lambda bi, i: (blk(bi, i), 0)),
            pl.BlockSpec((tb, GDN_QK), lambda bi, i: (blk(bi, i), 0)),
            pl.BlockSpec((tb, GDN_VW), lambda bi, i: (blk(bi, i), 0)),
            pl.BlockSpec((1, tb, GDN_HEADS), lambda bi, i: (bi, seq_blk(bi, i), 0)),
            pl.BlockSpec((1, tb, GDN_HEADS), lambda bi, i: (bi, seq_blk(bi, i), 0)),
            pl.BlockSpec((1, cpb, GDN_HEADS, GDN_CHUNK), lambda bi, i: (bi, seq_blk(bi, i), 0, 0)),
        ],
        out_specs=pl.BlockSpec((tb, GDN_VW), lambda bi, i: (blk(bi, i), 0)),
        out_shape=jax.ShapeDtypeStruct((t, GDN_VW), F32),
        scratch_shapes=[pltpu.VMEM((GDN_HEADS, GDN_DK, GDN_DV), F32)],
        compiler_params=_cparams(("parallel", "arbitrary")),
        name="gdn_delta_bw" if reverse else "gdn_delta_fw",
    )(q, k, v, gcol, bcol, grow)


def _gdn_out_kernel(of_ref, ob_ref, gate_ref, x_ref, mod_ref, gpost_ref, onorm_ref, w_ref, o_ref,
                    a_ref):
    o = of_ref[...] + ob_ref[...]
    gate = gate_ref[...]
    og = gate * jax.nn.sigmoid(gate)
    for h in range(GDN_HEADS):
        cs = slice(h * GDN_DV, (h + 1) * GDN_DV)
        a_ref[:, cs] = (_rms(o[:, cs], onorm_ref[...]) * og[:, cs]).astype(BF16)
    out = _dot(a_ref[...], w_ref[...])
    o_ref[...] = _residual(x_ref[...], out, mod_ref, gpost_ref, 1, 1.0)


def _gdn_out(o_fw, o_bw, gate, x, mod, gpost, onorm, w, *, seq, tm):
    t, d = x.shape
    return pl.pallas_call(
        _gdn_out_kernel,
        grid=(t // tm,),
        in_specs=[
            pl.BlockSpec((tm, GDN_VW), lambda i: (i, 0)),
            pl.BlockSpec((tm, GDN_VW), lambda i: (i, 0)),
            pl.BlockSpec((tm, GDN_VW), lambda i: (i, 0)),
            pl.BlockSpec((tm, d), lambda i: (i, 0)),
            pl.BlockSpec((1, 3 * N_SUB, d), lambda i: ((i * tm) // seq, 0, 0)),
            _const_spec((1, d)),
            _const_spec(onorm.shape),
            _const_spec(w.shape),
        ],
        out_specs=pl.BlockSpec((tm, d), lambda i: (i, 0)),
        out_shape=jax.ShapeDtypeStruct((t, d), F32),
        scratch_shapes=[pltpu.VMEM((tm, GDN_VW), BF16)],
        compiler_params=_cparams(("parallel",)),
        name="gdn_out",
    )(o_fw, o_bw, gate, x, mod, gpost, onorm, w)


def _fnet_chan_kernel(x_ref, mod_ref, gpre_ref, cs_ref, o_ref):
    hb = _modulated(x_ref[...], mod_ref, gpre_ref, 1).astype(BF16)
    cg = hb.shape[1] // FNET_GROUPS
    for g in range(FNET_GROUPS):
        ab = _dot(hb[:, g * cg:(g + 1) * cg], cs_ref[...])
        o_ref[0, 0, :, g * cg:(g + 1) * cg] = ab[:, :cg].astype(BF16)
        o_ref[0, 1, :, g * cg:(g + 1) * cg] = ab[:, cg:].astype(BF16)


def _fnet_chan(x, mod, gpre, cs, *, batch, seq, tm):
    t, d = x.shape
    nt = seq // tm
    return pl.pallas_call(
        _fnet_chan_kernel,
        grid=(t // tm,),
        in_specs=[
            pl.BlockSpec((tm, d), lambda i: (i, 0)),
            pl.BlockSpec((1, 3 * N_SUB, d), lambda i: ((i * tm) // seq, 0, 0)),
            _const_spec((1, d)),
            _const_spec(cs.shape),
        ],
        out_specs=pl.BlockSpec((1, 2, tm, d), lambda i: (i // nt, 0, i % nt, 0)),
        out_shape=jax.ShapeDtypeStruct((batch, 2, seq, d), BF16),
        compiler_params=_cparams(("parallel",)),
        name="fnet_chan",
    )(x, mod, gpre, cs)


def _fnet_seq_kernel(wseq_ref, ab_ref, x_ref, mod_ref, gpost_ref, w_ref, b_ref, o_ref, acc_ref):
    kk = pl.program_id(2)

    @pl.when(kk == 0)
    def _():
        acc_ref[...] = jnp.zeros_like(acc_ref)

    acc_ref[...] += _dot(wseq_ref[...], ab_ref[0])

    @pl.when(kk == pl.num_programs(2) - 1)
    def _():
        out = _dot(acc_ref[...].astype(BF16), w_ref[...]) + b_ref[...]
        o_ref[0] = _residual(x_ref[0], out, mod_ref, gpost_ref, 1, 1.0)


def _fnet_seq(wseq, ab, x, mod, gpost, w, b, *, tm, tk):
    batch, s2, d = ab.shape
    seq = s2 // 2
    return pl.pallas_call(
        _fnet_seq_kernel,
        grid=(batch, seq // tm, s2 // tk),
        in_specs=[
            pl.BlockSpec((tm, tk), lambda bi, i, kk: (i, kk)),
            pl.BlockSpec((1, tk, d), lambda bi, i, kk: (bi, kk, 0)),
            pl.BlockSpec((1, tm, d), lambda bi, i, kk: (bi, i, 0)),
            pl.BlockSpec((1, 3 * N_SUB, d), lambda bi, i, kk: (bi, 0, 0)),
            _const_spec((1, d)),
            _const_spec(w.shape),
            _const_spec((1, d)),
        ],
        out_specs=pl.BlockSpec((1, tm, d), lambda bi, i, kk: (bi, i, 0)),
        out_shape=jax.ShapeDtypeStruct((batch, seq, d), F32),
        scratch_shapes=[pltpu.VMEM((tm, d), F32)],
        compiler_params=_cparams(("parallel", "parallel", "arbitrary")),
        name="fnet_seq",
    )(wseq, ab, x, mod, gpost, w, b)


def _rope_table(seq):
    half = MLA_ROPE // 2
    pos = jnp.arange(seq, dtype=F32)
    inv_freq = ROPE_THETA ** (-jnp.arange(half, dtype=F32) / half)
    ang = pos[:, None] * inv_freq[None, :]
    cos, sin = jnp.cos(ang), jnp.sin(ang)
    return jnp.concatenate([cos, cos, -sin, sin], axis=1)


def _swap_halves(w):
    half = w.shape[-1] // 2
    return jnp.concatenate([w[..., half:], w[..., :half]], axis=-1)


def _prep_mla(w_down, w_uq, w_ukv):
    d = w_down.shape[0]
    rope = w_down[:, MLA_Q_LORA + MLA_KV_LORA:]
    wd = jnp.concatenate([w_down, _swap_halves(rope)], axis=1).astype(BF16)
    uq = w_uq.reshape(MLA_Q_LORA, MLA_HEADS, MLA_NOPE + MLA_ROPE)
    uq_rope = uq[..., MLA_NOPE:]
    wq = jnp.concatenate([uq, _swap_halves(uq_rope)], axis=-1)
    wq = wq.reshape(MLA_Q_LORA, MLA_HEADS * MLA_QK_PAD).astype(BF16)
    ukv = w_ukv.reshape(MLA_KV_LORA, MLA_HEADS, MLA_NOPE + MLA_V)
    wkv = jnp.concatenate([ukv[..., :MLA_NOPE].reshape(MLA_KV_LORA, -1),
                           ukv[..., MLA_NOPE:].reshape(MLA_KV_LORA, -1)], axis=1).astype(BF16)
    del d
    return wd, wq, wkv


def _fnet_tables(seq, cg):
    def trig(n):
        idx = jnp.arange(n, dtype=jnp.int32)
        ang = ((idx[:, None] * idx[None, :]) % n).astype(F32) * (2.0 * math.pi / n)
        return jnp.cos(ang), jnp.sin(ang)

    cc, sc = trig(cg)
    cs = (jnp.concatenate([cc, sc], axis=1) * (cg ** -0.5)).astype(BF16)
    cseq, sseq = trig(seq)
    wseq = (jnp.concatenate([cseq, -sseq], axis=1) * (seq ** -0.5)).astype(BF16)
    return cs, wseq


def _pick(n, pref):
    return pref if n % pref == 0 else n


def _trunk(x3, mods, prm):
    batch, seq, d = x3.shape
    t = batch * seq
    x = x3.reshape(t, d)
    tm = _pick(seq, 512)
    row = lambda v: v.reshape(1, -1)
    trig = _rope_table(seq)
    for l in range(mods.shape[0]):
        i = l // 3
        kind = l % 3
        mod = mods[l]
        gpre = prm['norm_pre'][l]
        gpost = prm['norm_post'][l]
        x = _ffn(x, mod, row(gpre[0]), row(gpost[0]), prm['ffn_w_in'][l][0], prm['ffn_w_out'][l][0],
                 sub=0, seq=seq, tm=tm)
        if kind == 0:
            wd, wq, wkv = prm['mla'][i]
            q, k, v = _mla_proj(x, mod, row(gpre[1]), trig, wd, row(prm['mla_q_norm'][i]),
                                row(prm['mla_kv_norm'][i]), wq, wkv, seq=seq, tm=tm)
            o = _flash(q.reshape(batch, seq, -1), k.reshape(batch, seq, -1), v.reshape(batch, seq, -1),
                       tq=_pick(seq, 512), tk=_pick(seq, 512))
            x = _out_proj(o.reshape(t, -1), x, mod, row(gpost[1]), prm['mla_w_out'][i], seq=seq, tm=tm)
        elif kind == 1:
            g = prm['gdn'][i]
            q, k, v, gate, gb = _gdn_proj(x, mod, row(gpre[1]), g['wqkv'], g['wg'], g['wab'], g['conv'],
                                          g['aneg'], g['dtb'], seq=seq, tm=_pick(seq, 256))
            gc = _gdn_cumsum(gb, tm=tm)
            nh = GDN_HEADS
            gc3 = gc.reshape(batch, seq, LANES)
            outs = []
            for dr in range(2):
                gcol = gc3[:, :, dr * nh:(dr + 1) * nh]
                bcol = gc3[:, :, 2 * nh + dr * nh:2 * nh + (dr + 1) * nh]
                grow = jnp.swapaxes(gcol.reshape(batch, seq // GDN_CHUNK, GDN_CHUNK, nh), 2, 3)
                outs.append(_gdn_delta(q, k, v, gcol, bcol, grow, batch=batch, seq=seq,
                                       tb=_pick(seq, 256), reverse=bool(dr)))
            x = _gdn_out(outs[0], outs[1], gate, x, mod, row(gpost[1]), row(prm['gdn_o_norm'][i]),
                         g['wout'], seq=seq, tm=tm)
        else:
            cs, wseq = _fnet_tables(seq, d // FNET_GROUPS)
            ab = _fnet_chan(x, mod, row(gpre[1]), cs, batch=batch, seq=seq, tm=tm)
            x = _fnet_seq(wseq, ab.reshape(batch, 2 * seq, d), x.reshape(batch, seq, d), mod,
                          row(gpost[1]), prm['fnet_w_out'][i], row(prm['fnet_b_out'][i]),
                          tm=_pick(seq, 1024), tk=_pick(seq, 1024)).reshape(t, d)
        x = _ffn(x, mod, row(gpre[2]), row(gpost[2]), prm['ffn_w_in'][l][1], prm['ffn_w_out'][l][1],
                 sub=2, seq=seq, tm=tm)
    return x.reshape(batch, seq, d)


def _prep_gdn(w_in, conv, a_log, dt_bias, w_out):
    n_conv = 2 * GDN_QK + GDN_VW
    wab = jnp.pad(w_in[:, n_conv + GDN_VW:], ((0, 0), (0, LANES - 4 * GDN_HEADS)))
    pad16 = lambda v: jnp.pad(v.reshape(1, -1).astype(F32), ((0, 0), (0, LANES - 2 * GDN_HEADS)))
    return {
        'wqkv': w_in[:, :n_conv].astype(BF16),
        'wg': w_in[:, n_conv:n_conv + GDN_VW].astype(BF16),
        'wab': wab.astype(BF16),
        'conv': conv.astype(F32),
        'aneg': pad16(-jnp.exp(a_log.astype(F32))),
        'dtb': pad16(dt_bias),
        'wout': w_out.astype(BF16),
    }


def kernel(x_prompt, x_sample, c_prompt, c_sample, w_ada, b_ada, norm_pre, norm_post, ffn_w_in, ffn_w_out, mla_w_down, mla_q_norm, mla_kv_norm, mla_w_uq, mla_w_ukv, mla_w_out, gdn_w_in, gdn_conv, gdn_a_log, gdn_dt_bias, gdn_o_norm, gdn_w_out, fnet_w_out, fnet_b_out):
    d = x_prompt.shape[-1]
    prm = {
        'norm_pre': norm_pre, 'norm_post': norm_post,
        'ffn_w_in': ffn_w_in.astype(BF16), 'ffn_w_out': ffn_w_out.astype(BF16),
        'mla': [_prep_mla(mla_w_down[i], mla_w_uq[i], mla_w_ukv[i]) for i in range(mla_w_down.shape[0])],
        'mla_q_norm': mla_q_norm, 'mla_kv_norm': mla_kv_norm, 'mla_w_out': mla_w_out.astype(BF16),
        'gdn': [_prep_gdn(gdn_w_in[i], gdn_conv[i], gdn_a_log[i], gdn_dt_bias[i], gdn_w_out[i])
                for i in range(gdn_w_in.shape[0])],
        'gdn_o_norm': gdn_o_norm,
        'fnet_w_out': fnet_w_out.astype(BF16), 'fnet_b_out': fnet_b_out,
    }
    nbp = c_prompt.shape[0]
    c_all = jnp.concatenate([c_prompt, c_sample], axis=0)
    mods = _ada_mod(c_all, w_ada, b_ada).reshape(w_ada.shape[0], c_all.shape[0], 3 * N_SUB, d)
    y_prompt = _trunk(x_prompt, mods[:, :nbp], prm)
    y_sample = _trunk(x_sample, mods[:, nbp:], prm)
    return (y_prompt, y_sample)
```

```python
import functools
import math

import jax
import jax.numpy as jnp
from jax import lax
from jax.experimental import pallas as pl
from jax.experimental.pallas import tpu as pltpu

F32 = jnp.float32
BF16 = jnp.bfloat16

N_SUB = 3
D_FF = 2816
NORM_EPS = 1e-6

MLA_HEADS = 8
MLA_Q_LORA = 384
MLA_KV_LORA = 256
MLA_NOPE = 128
MLA_ROPE = 64
MLA_V = 128
ROPE_THETA = 10000.0
MLA_SCALE = (MLA_NOPE + MLA_ROPE) ** -0.5
MLA_QK_PAD = 256

GDN_HEADS = 8
GDN_DK = 128
GDN_DV = 128
GDN_QK = GDN_HEADS * GDN_DK
GDN_VW = GDN_HEADS * GDN_DV
GDN_CONV = 5
GDN_CHUNK = 64
GDN_HALO = 8

FNET_GROUPS = 8

LANES = 128
VMEM_LIMIT = 56 * 1024 * 1024


def _cparams(sem):
    return pltpu.CompilerParams(dimension_semantics=sem, vmem_limit_bytes=VMEM_LIMIT)


def _const_spec(shape):
    nd = len(shape)
    return pl.BlockSpec(shape, lambda *_: (0,) * nd, pipeline_mode=pl.Buffered(1))


def _rms(x, g):
    ms = jnp.mean(x * x, axis=-1, keepdims=True)
    return x * lax.rsqrt(ms + NORM_EPS) * g


def _modulated(x, mod_ref, gpre_ref, sub):
    shift = mod_ref[0, 3 * sub:3 * sub + 1, :]
    scale = mod_ref[0, 3 * sub + 1:3 * sub + 2, :]
    return _rms(x, gpre_ref[...]) * (1.0 + scale) + shift


def _residual(x, out, mod_ref, gpost_ref, sub, weight):
    gate = mod_ref[0, 3 * sub + 2:3 * sub + 3, :]
    return x + (weight * gate) * _rms(out, gpost_ref[...])


def _dot(a, b):
    return jnp.dot(a, b, preferred_element_type=F32)


def _dot_nt(a, b):
    return lax.dot_general(a, b, (((1,), (1,)), ((), ())), preferred_element_type=F32)


def _dot_tn(a, b):
    return lax.dot_general(a, b, (((0,), (0,)), ((), ())), preferred_element_type=F32)


def _ada_kernel(c_ref, w_ref, b_ref, o_ref):
    c = c_ref[...]
    sc = (c * jax.nn.sigmoid(c)).astype(BF16)
    o_ref[0] = _dot(sc, w_ref[0].astype(BF16)) + b_ref[0]


def _ada_mod(c_all, w_ada, b_ada):
    nb, d = c_all.shape
    depth, _, n_out = w_ada.shape
    tn = n_out // 8
    return pl.pallas_call(
        _ada_kernel,
        grid=(depth, n_out // tn),
        in_specs=[
            pl.BlockSpec((nb, d), lambda l, j: (0, 0)),
            pl.BlockSpec((1, d, tn), lambda l, j: (l, 0, j)),
            pl.BlockSpec((1, 1, tn), lambda l, j: (l, 0, j)),
        ],
        out_specs=pl.BlockSpec((1, nb, tn), lambda l, j: (l, 0, j)),
        out_shape=jax.ShapeDtypeStruct((depth, nb, n_out), F32),
        compiler_params=_cparams(("arbitrary", "arbitrary")),
        name="ada_mod",
    )(c_all, w_ada, b_ada.reshape(depth, 1, n_out))


def _ffn_kernel(x_ref, mod_ref, gpre_ref, gpost_ref, win_ref, wout_ref, o_ref, act_ref,
                *, sub, ck):
    x = x_ref[...]
    hb = _modulated(x, mod_ref, gpre_ref, sub).astype(BF16)
    for c in range(D_FF // ck):
        g = _dot(hb, win_ref[:, c * ck:(c + 1) * ck])
        u = _dot(hb, win_ref[:, D_FF + c * ck:D_FF + (c + 1) * ck])
        act_ref[:, c * ck:(c + 1) * ck] = (g * jax.nn.sigmoid(g) * u).astype(BF16)
    out = _dot(act_ref[...], wout_ref[...])
    o_ref[...] = _residual(x, out, mod_ref, gpost_ref, sub, 0.5)


def _ffn(x, mod, gpre, gpost, w_in, w_out, *, sub, seq, tm):
    t, d = x.shape
    return pl.pallas_call(
        functools.partial(_ffn_kernel, sub=sub, ck=256),
        grid=(t // tm,),
        in_specs=[
            pl.BlockSpec((tm, d), lambda i: (i, 0)),
            pl.BlockSpec((1, 3 * N_SUB, d), lambda i: ((i * tm) // seq, 0, 0)),
            _const_spec((1, d)),
            _const_spec((1, d)),
            _const_spec(w_in.shape),
            _const_spec(w_out.shape),
        ],
        out_specs=pl.BlockSpec((tm, d), lambda i: (i, 0)),
        out_shape=jax.ShapeDtypeStruct((t, d), F32),
        scratch_shapes=[pltpu.VMEM((tm, D_FF), BF16)],
        compiler_params=_cparams(("parallel",)),
        name="ffn",
    )(x, mod, gpre, gpost, w_in, w_out)


def _rope_pair(t):
    return t + pltpu.roll(t, 64, axis=1)


def _mla_proj_kernel(x_ref, mod_ref, gpre_ref, trig_ref, wd_ref, qn_ref, kvn_ref, wq_ref, wkv_ref,
                     q_ref, k_ref, vt_ref):
    x = x_ref[...]
    hb = _modulated(x, mod_ref, gpre_ref, 1).astype(BF16)
    down = _dot(hb, wd_ref[...])
    cq = _rms(down[:, :MLA_Q_LORA], qn_ref[...]).astype(BF16)
    ckv = _rms(down[:, MLA_Q_LORA:MLA_Q_LORA + MLA_KV_LORA], kvn_ref[...]).astype(BF16)
    trig = trig_ref[...]
    lane = lax.broadcasted_iota(jnp.int32, trig.shape, 1)
    k_rope = _rope_pair(down[:, MLA_Q_LORA + MLA_KV_LORA:] * trig)
    k_rope = jnp.where(lane < MLA_ROPE, k_rope, 0.0).astype(BF16)
    q = _dot(cq, wq_ref[...])
    kv = _dot(ckv, wkv_ref[...])
    qs = MLA_SCALE * math.log2(math.e)
    for h in range(MLA_HEADS):
        c0 = h * MLA_QK_PAD
        q_ref[:, c0:c0 + LANES] = (q[:, c0:c0 + LANES] * qs).astype(BF16)
        q_rope = _rope_pair(q[:, c0 + LANES:c0 + 2 * LANES] * trig)
        q_ref[:, c0 + LANES:c0 + 2 * LANES] = (q_rope * qs).astype(BF16)
        k_ref[:, c0:c0 + LANES] = kv[:, h * MLA_NOPE:(h + 1) * MLA_NOPE].astype(BF16)
        k_ref[:, c0 + LANES:c0 + 2 * LANES] = k_rope
        v0 = MLA_HEADS * MLA_NOPE + h * MLA_V
        vt_ref[0, h, 0] = kv[:, v0:v0 + MLA_V].T.astype(BF16)


def _mla_proj(x, mod, gpre, trig, wd, qn, kvn, wq, wkv, *, seq, tm):
    t, d = x.shape
    nq = MLA_HEADS * MLA_QK_PAD
    nt = seq // tm
    return pl.pallas_call(
        _mla_proj_kernel,
        grid=(t // tm,),
        in_specs=[
            pl.BlockSpec((tm, d), lambda i: (i, 0)),
            pl.BlockSpec((1, 3 * N_SUB, d), lambda i: ((i * tm) // seq, 0, 0)),
            _const_spec((1, d)),
            pl.BlockSpec((tm, LANES), lambda i: (i % (seq // tm), 0)),
            _const_spec(wd.shape),
            _const_spec(qn.shape),
            _const_spec(kvn.shape),
            _const_spec(wq.shape),
            _const_spec(wkv.shape),
        ],
        out_specs=[
            pl.BlockSpec((tm, nq), lambda i: (i, 0)),
            pl.BlockSpec((tm, nq), lambda i: (i, 0)),
            pl.BlockSpec((1, MLA_HEADS, 1, MLA_V, tm), lambda i: (i // nt, 0, i % nt, 0, 0)),
        ],
        out_shape=[
            jax.ShapeDtypeStruct((t, nq), BF16),
            jax.ShapeDtypeStruct((t, nq), BF16),
            jax.ShapeDtypeStruct((t // seq, MLA_HEADS, nt, MLA_V, tm), BF16),
        ],
        compiler_params=_cparams(("parallel",)),
        name="mla_proj",
    )(x, mod, gpre, trig, wd, qn, kvn, wq, wkv)


def _flash_kernel(q_ref, k_ref, vt_ref, o_ref, s_ref, p_ref, acc_ref):
    q = q_ref[0]
    tq = q.shape[0]
    nk, _, tk = vt_ref.shape[2:]
    assert nk == 1 or nk % 2 == 0

    def scores(j, slot):
        r0 = pl.multiple_of(j * tk, tk)
        s_ref[slot] = _dot_nt(k_ref[0, pl.ds(r0, tk), :], q)

    def softmax(slot, m, l):
        s = s_ref[slot]
        m_new = jnp.maximum(m, jnp.max(jnp.max(s.reshape(tk // 8, 8, tq), axis=0),
                                       axis=0, keepdims=True))
        alpha = jnp.exp2(m - m_new)
        p = jnp.exp2(s - m_new)
        l = alpha * l + jnp.sum(p.reshape(tk // 8, 8, tq), axis=0)
        p_ref[slot] = p.astype(BF16)
        return m_new, l, alpha

    def weighted(j, slot, alpha):
        acc_ref[...] = alpha * acc_ref[...] + _dot(vt_ref[0, 0, j], p_ref[slot])

    def step(j, slot, m, l, alpha):
        weighted(j - 1, 1 - slot, alpha)
        scores(j + 1, 1 - slot)
        return softmax(slot, m, l)

    acc_ref[...] = jnp.zeros_like(acc_ref)
    m = jnp.full((1, tq), -1e30, F32)
    l = jnp.zeros((8, tq), F32)
    scores(0, 0)
    if nk > 1:
        scores(1, 1)
    m, l, alpha = softmax(0, m, l)
    if nk > 1:
        def body(t, carry):
            m, l, alpha = carry
            j = 1 + 2 * t
            m, l, alpha = step(j, 1, m, l, alpha)
            return step(j + 1, 0, m, l, alpha)

        m, l, alpha = lax.fori_loop(0, (nk - 2) // 2, body, (m, l, alpha))
        weighted(nk - 2, 0, alpha)
        m, l, alpha = softmax(1, m, l)
    weighted(nk - 1, (nk - 1) % 2, alpha)
    o_t = acc_ref[...] / jnp.sum(l, axis=0, keepdims=True)
    o_ref[0] = o_t.T.astype(o_ref.dtype)


def _flash(q, k, vt, *, tq):
    b, s, _ = q.shape
    _, _, nk, dv, tk = vt.shape
    return pl.pallas_call(
        _flash_kernel,
        grid=(b, MLA_HEADS, s // tq),
        in_specs=[
            pl.BlockSpec((1, tq, MLA_QK_PAD), lambda bi, h, i: (bi, i, h)),
            pl.BlockSpec((1, s, MLA_QK_PAD), lambda bi, h, i: (bi, 0, h)),
            pl.BlockSpec((1, 1, nk, dv, tk), lambda bi, h, i: (bi, h, 0, 0, 0)),
        ],
        out_specs=pl.BlockSpec((1, tq, MLA_V), lambda bi, h, i: (bi, i, h)),
        out_shape=jax.ShapeDtypeStruct((b, s, MLA_HEADS * MLA_V), BF16),
        scratch_shapes=[
            pltpu.VMEM((2, tk, tq), F32),
            pltpu.VMEM((2, tk, tq), BF16),
            pltpu.VMEM((dv, tq), F32),
        ],
        compiler_params=_cparams(("parallel", "parallel", "arbitrary")),
        name="mla_flash",
    )(q, k, vt)


def _out_proj_kernel(a_ref, x_ref, mod_ref, gpost_ref, w_ref, o_ref):
    out = _dot(a_ref[...], w_ref[...])
    o_ref[...] = _residual(x_ref[...], out, mod_ref, gpost_ref, 1, 1.0)


def _out_proj(a, x, mod, gpost, w, *, seq, tm):
    t, d = x.shape
    return pl.pallas_call(
        _out_proj_kernel,
        grid=(t // tm,),
        in_specs=[
            pl.BlockSpec((tm, a.shape[1]), lambda i: (i, 0)),
            pl.BlockSpec((tm, d), lambda i: (i, 0)),
            pl.BlockSpec((1, 3 * N_SUB, d), lambda i: ((i * tm) // seq, 0, 0)),
            _const_spec((1, d)),
            _const_spec(w.shape),
        ],
        out_specs=pl.BlockSpec((tm, d), lambda i: (i, 0)),
        out_shape=jax.ShapeDtypeStruct((t, d), F32),
        compiler_params=_cparams(("parallel",)),
        name="out_proj",
    )(a, x, mod, gpost, w)


def _gdn_proj_kernel(xp_ref, x_ref, xn_ref, mod_ref, gpre_ref, wqkv_ref, wg_ref, wab_ref, conv_ref,
                     aneg_ref, dtb_ref, q_ref, k_ref, v_ref, gate_ref, gb_ref, p_ref,
                     *, tiles_per_seq):
    i = pl.program_id(0)
    tm = x_ref.shape[0]
    first = (i % tiles_per_seq) == 0
    last = (i % tiles_per_seq) == tiles_per_seq - 1

    def proj_rows(xr):
        return _dot(_modulated(xr, mod_ref, gpre_ref, 1).astype(BF16), wqkv_ref[...])

    hb = _modulated(x_ref[...], mod_ref, gpre_ref, 1).astype(BF16)
    p_ref[0:GDN_HALO, :] = jnp.where(first, 0.0, proj_rows(xp_ref[...]))
    p_ref[GDN_HALO:GDN_HALO + tm, :] = _dot(hb, wqkv_ref[...])
    p_ref[GDN_HALO + tm:, :] = jnp.where(last, 0.0, proj_rows(xn_ref[...]))

    pad = GDN_CONV // 2
    acc = None
    for tap in range(GDN_CONV):
        r0 = GDN_HALO - pad + tap
        term = p_ref[r0:r0 + tm, :] * conv_ref[tap:tap + 1, :]
        acc = term if acc is None else acc + term
    qkv = acc * jax.nn.sigmoid(acc)

    def l2n(z):
        return z * lax.rsqrt(jnp.sum(z * z, axis=-1, keepdims=True) + NORM_EPS)

    for h in range(GDN_HEADS):
        c0 = h * GDN_DK
        q_ref[:, c0:c0 + GDN_DK] = l2n(qkv[:, c0:c0 + GDN_DK]) * (GDN_DK ** -0.5)
        k_ref[:, c0:c0 + GDN_DK] = l2n(qkv[:, GDN_QK + c0:GDN_QK + c0 + GDN_DK])
    v_ref[...] = qkv[:, 2 * GDN_QK:]
    gate_ref[...] = _dot(hb, wg_ref[...])
    ab = _dot(hb, wab_ref[...])
    z = ab + dtb_ref[...]
    softplus = jnp.maximum(z, 0.0) + jnp.log(1.0 + jnp.exp(-jnp.abs(z)))
    lane = lax.broadcasted_iota(jnp.int32, ab.shape, 1)
    gb_ref[...] = jnp.where(lane < 2 * GDN_HEADS, aneg_ref[...] * softplus, jax.nn.sigmoid(ab))


def _gdn_proj(x, mod, gpre, wqkv, wg, wab, conv, aneg, dtb, *, seq, tm):
    t, d = x.shape
    hb = tm // GDN_HALO
    nblk8 = t // GDN_HALO
    n_conv = wqkv.shape[1]
    return pl.pallas_call(
        functools.partial(_gdn_proj_kernel, tiles_per_seq=seq // tm),
        grid=(t // tm,),
        in_specs=[
            pl.BlockSpec((GDN_HALO, d), lambda i: (jnp.maximum(i * hb - 1, 0), 0)),
            pl.BlockSpec((tm, d), lambda i: (i, 0)),
            pl.BlockSpec((GDN_HALO, d), lambda i: (jnp.minimum((i + 1) * hb, nblk8 - 1), 0)),
            pl.BlockSpec((1, 3 * N_SUB, d), lambda i: ((i * tm) // seq, 0, 0)),
            _const_spec((1, d)),
            _const_spec(wqkv.shape),
            _const_spec(wg.shape),
            _const_spec(wab.shape),
            _const_spec(conv.shape),
            _const_spec(aneg.shape),
            _const_spec(dtb.shape),
        ],
        out_specs=[
            pl.BlockSpec((tm, GDN_QK), lambda i: (i, 0)),
            pl.BlockSpec((tm, GDN_QK), lambda i: (i, 0)),
            pl.BlockSpec((tm, GDN_VW), lambda i: (i, 0)),
            pl.BlockSpec((tm, GDN_VW), lambda i: (i, 0)),
            pl.BlockSpec((tm, LANES), lambda i: (i, 0)),
        ],
        out_shape=[
            jax.ShapeDtypeStruct((t, GDN_QK), F32),
            jax.ShapeDtypeStruct((t, GDN_QK), F32),
            jax.ShapeDtypeStruct((t, GDN_VW), F32),
            jax.ShapeDtypeStruct((t, GDN_VW), F32),
            jax.ShapeDtypeStruct((t, LANES), F32),
        ],
        scratch_shapes=[pltpu.VMEM((tm + 2 * GDN_HALO, n_conv), F32)],
        compiler_params=_cparams(("parallel",)),
        name="gdn_proj",
    )(x, x, x, mod, gpre, wqkv, wg, wab, conv, aneg, dtb)


def _gdn_cumsum_kernel(gb_ref, o_ref):
    g = gb_ref[...]
    tm = g.shape[0]
    row = lax.broadcasted_iota(jnp.int32, (tm, tm), 0)
    col = lax.broadcasted_iota(jnp.int32, (tm, tm), 1)
    sh = int(math.log2(GDN_CHUNK))
    same = lax.shift_right_logical(row, sh) == lax.shift_right_logical(col, sh)
    lower = jnp.where(same & (col <= row), 1.0, 0.0).astype(BF16)
    upper = jnp.where(same & (col >= row), 1.0, 0.0).astype(BF16)
    g1 = g.astype(BF16)
    r1 = g - g1.astype(F32)
    g2 = r1.astype(BF16)
    g3 = (r1 - g2.astype(F32)).astype(BF16)
    pre = _dot(lower, g1) + _dot(lower, g2) + _dot(lower, g3)
    suf = _dot(upper, g1) + _dot(upper, g2) + _dot(upper, g3)
    lane = lax.broadcasted_iota(jnp.int32, g.shape, 1)
    o_ref[...] = jnp.where(lane < GDN_HEADS, pre, jnp.where(lane < 2 * GDN_HEADS, suf, g))


def _gdn_cumsum(gb, *, tm):
    t = gb.shape[0]
    return pl.pallas_call(
        _gdn_cumsum_kernel,
        grid=(t // tm,),
        in_specs=[pl.BlockSpec((tm, LANES), lambda i: (i, 0))],
        out_specs=pl.BlockSpec((tm, LANES), lambda i: (i, 0)),
        out_shape=jax.ShapeDtypeStruct((t, LANES), F32),
        compiler_params=_cparams(("parallel",)),
        name="gdn_cumsum",
    )(gb)


def _gdn_delta_kernel(q_ref, k_ref, v_ref, gcol_ref, bcol_ref, grow_ref, o_ref, state_ref,
                      *, reverse):
    c = GDN_CHUNK
    nchunk = q_ref.shape[0] // c

    @pl.when(pl.program_id(1) == 0)
    def _():
        state_ref[...] = jnp.zeros_like(state_ref)

    row = lax.broadcasted_iota(jnp.int32, (c, c), 0)
    col = lax.broadcasted_iota(jnp.int32, (c, c), 1)
    if reverse:
        row, col = col, row
    tri = row >= col
    strict = row > col
    eye = jnp.where(row == col, 1.0, 0.0)
    level_masks = []
    for lv in range(int(math.log2(c))):
        rb = lax.shift_right_logical(row, lv)
        cb = lax.shift_right_logical(col, lv)
        level_masks.append((lax.shift_right_logical(rb, 1) == lax.shift_right_logical(cb, 1))
                           & ((rb & 1) == 1) & ((cb & 1) == 0))
    last_row = 0 if reverse else c - 1

    heads = range(GDN_HEADS)
    order = list(reversed(range(nchunk))) if reverse else list(range(nchunk))
    probs = [(j, h) for j in order for h in heads]
    rows = lambda j: slice(j * c, (j + 1) * c)
    cols = lambda h: slice(h * GDN_DK, (h + 1) * GDN_DK)

    qs = [q_ref[rows(j), cols(h)] for j, h in probs]
    ks = [k_ref[rows(j), cols(h)] for j, h in probs]
    vs = [v_ref[rows(j), cols(h)] for j, h in probs]
    gcbs = [jnp.broadcast_to(gcol_ref[0, rows(j), h:h + 1], (c, GDN_DK)) for j, h in probs]
    betas = [jnp.broadcast_to(bcol_ref[0, rows(j), h:h + 1], (c, GDN_DK)) for j, h in probs]
    n = len(probs)
    decays = []
    for p, (j, h) in enumerate(probs):
        grb = jnp.broadcast_to(grow_ref[0, j, h:h + 1, :], (c, c))
        decays.append(jnp.where(tri, jnp.exp(jnp.where(tri, gcbs[p][:, :c] - grb, 0.0)), 0.0))
    kbs = [ks[p] * betas[p] for p in range(n)]
    a2s = [_dot_nt(jnp.concatenate([kbs[p], qs[p]], axis=0).astype(BF16), ks[p].astype(BF16))
           for p in range(n)]
    ms = [jnp.where(strict, a2s[p][:c] * decays[p], 0.0) for p in range(n)]
    qks = [(a2s[p][c:] * decays[p]).astype(BF16) for p in range(n)]
    tinvs = [eye - jnp.where(level_masks[0], ms[p], 0.0) for p in range(n)]
    for lm in level_masks[1:]:
        tbs = [t.astype(BF16) for t in tinvs]
        tcs = [_dot(tbs[p], jnp.where(lm, ms[p], 0.0).astype(BF16)).astype(BF16) for p in range(n)]
        tinvs = [tinvs[p] - _dot(tcs[p], tbs[p]) for p in range(n)]
    egs = [jnp.exp(g) for g in gcbs]
    sols = [_dot(tinvs[p].astype(BF16),
                 jnp.concatenate([vs[p] * betas[p], kbs[p] * egs[p]], axis=1).astype(BF16))
            for p in range(n)]
    g_lasts = [g[last_row:last_row + 1, :] for g in gcbs]
    k_decs = [(ks[p] * jnp.exp(g_lasts[p] - gcbs[p])).astype(BF16) for p in range(n)]
    wq_lhs = [jnp.concatenate([sols[p][:, GDN_DV:], qs[p] * egs[p]], axis=0).astype(BF16)
              for p in range(n)]

    sts = [state_ref[h] for h in heads]
    for step, j in enumerate(order):
        ps = [step * GDN_HEADS + h for h in heads]
        wqs = [_dot(wq_lhs[ps[h]], sts[h].astype(BF16)) for h in heads]
        vnbs = [(sols[ps[h]][:, :GDN_DV] - wqs[h][:c]).astype(BF16) for h in heads]
        for h in heads:
            o_ref[rows(j), cols(h)] = wqs[h][c:] + _dot(qks[ps[h]], vnbs[h])
        sts = [sts[h] * jnp.exp(g_lasts[ps[h]]) + _dot_tn(k_decs[ps[h]], vnbs[h]) for h in heads]
    for h in heads:
        state_ref[h] = sts[h]


def _gdn_delta(q, k, v, gcol, bcol, grow, *, batch, seq, tb, reverse):
    t = q.shape[0]
    nb = seq // tb
    cpb = tb // GDN_CHUNK

    def blk(bi, i):
        return bi * nb + ((nb - 1 - i) if reverse else i)

    def seq_blk(bi, i):
        return (nb - 1 - i) if reverse else i

    return pl.pallas_call(
        functools.partial(_gdn_delta_kernel, reverse=reverse),
        grid=(batch, nb),
        in_specs=[
            pl.BlockSpec((tb, GDN_QK), lambda bi, i: (blk(bi, i), 0)),
            pl.BlockSpec((tb, GDN_QK), lambda bi, i: (blk(bi, i), 0)),
            pl.BlockSpec((tb, GDN_VW), lambda bi, i: (blk(bi, i), 0)),
            pl.BlockSpec((1, tb, GDN_HEADS), lambda bi, i: (bi, seq_blk(bi, i), 0)),
            pl.BlockSpec((1, tb, GDN_HEADS), lambda bi, i: (bi, seq_blk(bi, i), 0)),
            pl.BlockSpec((1, cpb, GDN_HEADS, GDN_CHUNK), lambda bi, i: (bi, seq_blk(bi, i), 0, 0)),
        ],
        out_specs=pl.BlockSpec((tb, GDN_VW), lambda bi, i: (blk(bi, i), 0)),
        out_shape=jax.ShapeDtypeStruct((t, GDN_VW), F32),
        scratch_shapes=[pltpu.VMEM((GDN_HEADS, GDN_DK, GDN_DV), F32)],
        compiler_params=_cparams(("parallel", "arbitrary")),
        name="gdn_delta_bw" if reverse else "gdn_delta_fw",
    )(q, k, v, gcol, bcol, grow)


def _gdn_out_kernel(of_ref, ob_ref, gate_ref, x_ref, mod_ref, gpost_ref, onorm_ref, w_ref, o_ref,
                    a_ref):
    o = of_ref[...] + ob_ref[...]
    gate = gate_ref[...]
    og = gate * jax.nn.sigmoid(gate)
    for h in range(GDN_HEADS):
        cs = slice(h * GDN_DV, (h + 1) * GDN_DV)
        a_ref[:, cs] = (_rms(o[:, cs], onorm_ref[...]) * og[:, cs]).astype(BF16)
    out = _dot(a_ref[...], w_ref[...])
    o_ref[...] = _residual(x_ref[...], out, mod_ref, gpost_ref, 1, 1.0)


def _gdn_out(o_fw, o_bw, gate, x, mod, gpost, onorm, w, *, seq, tm):
    t, d = x.shape
    return pl.pallas_call(
        _gdn_out_kernel,
        grid=(t // tm,),
        in_specs=[
            pl.BlockSpec((tm, GDN_VW), lambda i: (i, 0)),
            pl.BlockSpec((tm, GDN_VW), lambda i: (i, 0)),
            pl.BlockSpec((tm, GDN_VW), lambda i: (i, 0)),
            pl.BlockSpec((tm, d), lambda i: (i, 0)),
            pl.BlockSpec((1, 3 * N_SUB, d), lambda i: ((i * tm) // seq, 0, 0)),
            _const_spec((1, d)),
            _const_spec(onorm.shape),
            _const_spec(w.shape),
        ],
        out_specs=pl.BlockSpec((tm, d), lambda i: (i, 0)),
        out_shape=jax.ShapeDtypeStruct((t, d), F32),
        scratch_shapes=[pltpu.VMEM((tm, GDN_VW), BF16)],
        compiler_params=_cparams(("parallel",)),
        name="gdn_out",
    )(o_fw, o_bw, gate, x, mod, gpost, onorm, w)


def _fnet_chan_kernel(x_ref, mod_ref, gpre_ref, cs_ref, o_ref):
    hb = _modulated(x_ref[...], mod_ref, gpre_ref, 1).astype(BF16)
    cg = hb.shape[1] // FNET_GROUPS
    for g in range(FNET_GROUPS):
        ab = _dot(hb[:, g * cg:(g + 1) * cg], cs_ref[...])
        o_ref[0, 0, :, g * cg:(g + 1) * cg] = ab[:, :cg].astype(BF16)
        o_ref[0, 1, :, g * cg:(g + 1) * cg] = ab[:, cg:].astype(BF16)


def _fnet_chan(x, mod, gpre, cs, *, batch, seq, tm):
    t, d = x.shape
    nt = seq // tm
    return pl.pallas_call(
        _fnet_chan_kernel,
        grid=(t // tm,),
        in_specs=[
            pl.BlockSpec((tm, d), lambda i: (i, 0)),
            pl.BlockSpec((1, 3 * N_SUB, d), lambda i: ((i * tm) // seq, 0, 0)),
            _const_spec((1, d)),
            _const_spec(cs.shape),
        ],
        out_specs=pl.BlockSpec((1, 2, tm, d), lambda i: (i // nt, 0, i % nt, 0)),
        out_shape=jax.ShapeDtypeStruct((batch, 2, seq, d), BF16),
        compiler_params=_cparams(("parallel",)),
        name="fnet_chan",
    )(x, mod, gpre, cs)


def _fnet_seq_kernel(wseq_ref, ab_ref, x_ref, mod_ref, gpost_ref, w_ref, b_ref, o_ref, acc_ref):
    kk = pl.program_id(2)

    @pl.when(kk == 0)
    def _():
        acc_ref[...] = jnp.zeros_like(acc_ref)

    acc_ref[...] += _dot(wseq_ref[...], ab_ref[0])

    @pl.when(kk == pl.num_programs(2) - 1)
    def _():
        out = _dot(acc_ref[...].astype(BF16), w_ref[...]) + b_ref[...]
        o_ref[0] = _residual(x_ref[0], out, mod_ref, gpost_ref, 1, 1.0)


def _fnet_seq(wseq, ab, x, mod, gpost, w, b, *, tm, tk):
    batch, s2, d = ab.shape
    seq = s2 // 2
    return pl.pallas_call(
        _fnet_seq_kernel,
        grid=(batch, seq // tm, s2 // tk),
        in_specs=[
            pl.BlockSpec((tm, tk), lambda bi, i, kk: (i, kk)),
            pl.BlockSpec((1, tk, d), lambda bi, i, kk: (bi, kk, 0)),
            pl.BlockSpec((1, tm, d), lambda bi, i, kk: (bi, i, 0)),
            pl.BlockSpec((1, 3 * N_SUB, d), lambda bi, i, kk: (bi, 0, 0)),
            _const_spec((1, d)),
            _const_spec(w.shape),
            _const_spec((1, d)),
        ],
        out_specs=pl.BlockSpec((1, tm, d), lambda bi, i, kk: (bi, i, 0)),
        out_shape=jax.ShapeDtypeStruct((batch, seq, d), F32),
        scratch_shapes=[pltpu.VMEM((tm, d), F32)],
        compiler_params=_cparams(("parallel", "parallel", "arbitrary")),
        name="fnet_seq",
    )(wseq, ab, x, mod, gpost, w, b)


def _rope_table(seq):
    half = MLA_ROPE // 2
    pos = jnp.arange(seq, dtype=F32)
    inv_freq = ROPE_THETA ** (-jnp.arange(half, dtype=F32) / half)
    ang = pos[:, None] * inv_freq[None, :]
    cos, sin = jnp.cos(ang), jnp.sin(ang)
    return jnp.concatenate([cos, cos, -sin, sin], axis=1)


def _swap_halves(w):
    half = w.shape[-1] // 2
    return jnp.concatenate([w[..., half:], w[..., :half]], axis=-1)


def _prep_mla(w_down, w_uq, w_ukv):
    d = w_down.shape[0]
    rope = w_down[:, MLA_Q_LORA + MLA_KV_LORA:]
    wd = jnp.concatenate([w_down, _swap_halves(rope)], axis=1).astype(BF16)
    uq = w_uq.reshape(MLA_Q_LORA, MLA_HEADS, MLA_NOPE + MLA_ROPE)
    uq_rope = uq[..., MLA_NOPE:]
    wq = jnp.concatenate([uq, _swap_halves(uq_rope)], axis=-1)
    wq = wq.reshape(MLA_Q_LORA, MLA_HEADS * MLA_QK_PAD).astype(BF16)
    ukv = w_ukv.reshape(MLA_KV_LORA, MLA_HEADS, MLA_NOPE + MLA_V)
    wkv = jnp.concatenate([ukv[..., :MLA_NOPE].reshape(MLA_KV_LORA, -1),
                           ukv[..., MLA_NOPE:].reshape(MLA_KV_LORA, -1)], axis=1).astype(BF16)
    del d
    return wd, wq, wkv


def _fnet_tables(seq, cg):
    def trig(n):
        idx = jnp.arange(n, dtype=jnp.int32)
        ang = ((idx[:, None] * idx[None, :]) % n).astype(F32) * (2.0 * math.pi / n)
        return jnp.cos(ang), jnp.sin(ang)

    cc, sc = trig(cg)
    cs = (jnp.concatenate([cc, sc], axis=1) * (cg ** -0.5)).astype(BF16)
    cseq, sseq = trig(seq)
    wseq = (jnp.concatenate([cseq, -sseq], axis=1) * (seq ** -0.5)).astype(BF16)
    return cs, wseq


def _pick(n, pref):
    return pref if n % pref == 0 else n


def _trunk(x3, mods, prm):
    batch, seq, d = x3.shape
    t = batch * seq
    x = x3.reshape(t, d)
    tm = _pick(seq, 512)
    row = lambda v: v.reshape(1, -1)
    trig = _rope_table(seq)
    for l in range(mods.shape[0]):
        i = l // 3
        kind = l % 3
        mod = mods[l]
        gpre = prm['norm_pre'][l]
        gpost = prm['norm_post'][l]
        x = _ffn(x, mod, row(gpre[0]), row(gpost[0]), prm['ffn_w_in'][l][0], prm['ffn_w_out'][l][0],
                 sub=0, seq=seq, tm=tm)
        if kind == 0:
            wd, wq, wkv = prm['mla'][i]
            q, k, vt = _mla_proj(x, mod, row(gpre[1]), trig, wd, row(prm['mla_q_norm'][i]),
                                 row(prm['mla_kv_norm'][i]), wq, wkv, seq=seq, tm=tm)
            o = _flash(q.reshape(batch, seq, -1), k.reshape(batch, seq, -1), vt, tq=_pick(seq, 512))
            x = _out_proj(o.reshape(t, -1), x, mod, row(gpost[1]), prm['mla_w_out'][i], seq=seq, tm=tm)
        elif kind == 1:
            g = prm['gdn'][i]
            q, k, v, gate, gb = _gdn_proj(x, mod, row(gpre[1]), g['wqkv'], g['wg'], g['wab'], g['conv'],
                                          g['aneg'], g['dtb'], seq=seq, tm=_pick(seq, 256))
            gc = _gdn_cumsum(gb, tm=tm)
            nh = GDN_HEADS
            gc3 = gc.reshape(batch, seq, LANES)
            outs = []
            for dr in range(2):
                gcol = gc3[:, :, dr * nh:(dr + 1) * nh]
                bcol = gc3[:, :, 2 * nh + dr * nh:2 * nh + (dr + 1) * nh]
                grow = jnp.swapaxes(gcol.reshape(batch, seq // GDN_CHUNK, GDN_CHUNK, nh), 2, 3)
                outs.append(_gdn_delta(q, k, v, gcol, bcol, grow, batch=batch, seq=seq,
                                       tb=_pick(seq, 256), reverse=bool(dr)))
            x = _gdn_out(outs[0], outs[1], gate, x, mod, row(gpost[1]), row(prm['gdn_o_norm'][i]),
                         g['wout'], seq=seq, tm=tm)
        else:
            cs, wseq = _fnet_tables(seq, d // FNET_GROUPS)
            ab = _fnet_chan(x, mod, row(gpre[1]), cs, batch=batch, seq=seq, tm=tm)
            x = _fnet_seq(wseq, ab.reshape(batch, 2 * seq, d), x.reshape(batch, seq, d), mod,
                          row(gpost[1]), prm['fnet_w_out'][i], row(prm['fnet_b_out'][i]),
                          tm=_pick(seq, 1024), tk=_pick(seq, 1024)).reshape(t, d)
        x = _ffn(x, mod, row(gpre[2]), row(gpost[2]), prm['ffn_w_in'][l][1], prm['ffn_w_out'][l][1],
                 sub=2, seq=seq, tm=tm)
    return x.reshape(batch, seq, d)


def _prep_gdn(w_in, conv, a_log, dt_bias, w_out):
    n_conv = 2 * GDN_QK + GDN_VW
    wab = jnp.pad(w_in[:, n_conv + GDN_VW:], ((0, 0), (0, LANES - 4 * GDN_HEADS)))
    pad16 = lambda v: jnp.pad(v.reshape(1, -1).astype(F32), ((0, 0), (0, LANES - 2 * GDN_HEADS)))
    return {
        'wqkv': w_in[:, :n_conv].astype(BF16),
        'wg': w_in[:, n_conv:n_conv + GDN_VW].astype(BF16),
        'wab': wab.astype(BF16),
        'conv': conv.astype(F32),
        'aneg': pad16(-jnp.exp(a_log.astype(F32))),
        'dtb': pad16(dt_bias),
        'wout': w_out.astype(BF16),
    }


def kernel(x_prompt, x_sample, c_prompt, c_sample, w_ada, b_ada, norm_pre, norm_post, ffn_w_in, ffn_w_out, mla_w_down, mla_q_norm, mla_kv_norm, mla_w_uq, mla_w_ukv, mla_w_out, gdn_w_in, gdn_conv, gdn_a_log, gdn_dt_bias, gdn_o_norm, gdn_w_out, fnet_w_out, fnet_b_out):
    d = x_prompt.shape[-1]
    prm = {
        'norm_pre': norm_pre, 'norm_post': norm_post,
        'ffn_w_in': ffn_w_in.astype(BF16), 'ffn_w_out': ffn_w_out.astype(BF16),
        'mla': [_prep_mla(mla_w_down[i], mla_w_uq[i], mla_w_ukv[i]) for i in range(mla_w_down.shape[0])],
        'mla_q_norm': mla_q_norm, 'mla_kv_norm': mla_kv_norm, 'mla_w_out': mla_w_out.astype(BF16),
        'gdn': [_prep_gdn(gdn_w_in[i], gdn_conv[i], gdn_a_log[i], gdn_dt_bias[i], gdn_w_out[i])
                for i in range(gdn_w_in.shape[0])],
        'gdn_o_norm': gdn_o_norm,
        'fnet_w_out': fnet_w_out.astype(BF16), 'fnet_b_out': fnet_b_out,
    }
    nbp = c_prompt.shape[0]
    c_all = jnp.concatenate([c_prompt, c_sample], axis=0)
    mods = _ada_mod(c_all, w_ada, b_ada).reshape(w_ada.shape[0], c_all.shape[0], 3 * N_SUB, d)
    y_prompt = _trunk(x_prompt, mods[:, :nbp], prm)
    y_sample = _trunk(x_sample, mods[:, nbp:], prm)
    return (y_prompt, y_sample)
```

```python
import functools
import math

import jax
import jax.numpy as jnp
from jax import lax
from jax.experimental import pallas as pl
from jax.experimental.pallas import tpu as pltpu

F32 = jnp.float32
BF16 = jnp.bfloat16

N_SUB = 3
D_FF = 2816
NORM_EPS = 1e-6

MLA_HEADS = 8
MLA_Q_LORA = 384
MLA_KV_LORA = 256
MLA_NOPE = 128
MLA_ROPE = 64
MLA_V = 128
ROPE_THETA = 10000.0
MLA_SCALE = (MLA_NOPE + MLA_ROPE) ** -0.5
MLA_QK_PAD = 256

GDN_HEADS = 8
GDN_DK = 128
GDN_DV = 128
GDN_QK = GDN_HEADS * GDN_DK
GDN_VW = GDN_HEADS * GDN_DV
GDN_CONV = 5
GDN_CHUNK = 64
GDN_HALO = 8

FNET_GROUPS = 8

LANES = 128
VMEM_LIMIT = 56 * 1024 * 1024


def _cparams(sem):
    return pltpu.CompilerParams(dimension_semantics=sem, vmem_limit_bytes=VMEM_LIMIT)


def _const_spec(shape):
    nd = len(shape)
    return pl.BlockSpec(shape, lambda *_: (0,) * nd, pipeline_mode=pl.Buffered(1))


def _rms(x, g):
    ms = jnp.mean(x * x, axis=-1, keepdims=True)
    return x * lax.rsqrt(ms + NORM_EPS) * g


def _modulated(x, mod_ref, gpre_ref, sub):
    shift = mod_ref[0, 3 * sub:3 * sub + 1, :]
    scale = mod_ref[0, 3 * sub + 1:3 * sub + 2, :]
    return _rms(x, gpre_ref[...]) * (1.0 + scale) + shift


def _residual(x, out, mod_ref, gpost_ref, sub, weight):
    gate = mod_ref[0, 3 * sub + 2:3 * sub + 3, :]
    return x + (weight * gate) * _rms(out, gpost_ref[...])


def _dot(a, b):
    return jnp.dot(a, b, preferred_element_type=F32)


def _dot_nt(a, b):
    return lax.dot_general(a, b, (((1,), (1,)), ((), ())), preferred_element_type=F32)


def _dot_tn(a, b):
    return lax.dot_general(a, b, (((0,), (0,)), ((), ())), preferred_element_type=F32)


def _ada_kernel(c_ref, w_ref, b_ref, o_ref):
    c = c_ref[...]
    sc = (c * jax.nn.sigmoid(c)).astype(BF16)
    o_ref[0] = _dot(sc, w_ref[0].astype(BF16)) + b_ref[0]


def _ada_mod(c_all, w_ada, b_ada):
    nb, d = c_all.shape
    depth, _, n_out = w_ada.shape
    tn = n_out // 8
    return pl.pallas_call(
        _ada_kernel,
        grid=(depth, n_out // tn),
        in_specs=[
            pl.BlockSpec((nb, d), lambda l, j: (0, 0)),
            pl.BlockSpec((1, d, tn), lambda l, j: (l, 0, j)),
            pl.BlockSpec((1, 1, tn), lambda l, j: (l, 0, j)),
        ],
        out_specs=pl.BlockSpec((1, nb, tn), lambda l, j: (l, 0, j)),
        out_shape=jax.ShapeDtypeStruct((depth, nb, n_out), F32),
        compiler_params=_cparams(("arbitrary", "arbitrary")),
        name="ada_mod",
    )(c_all, w_ada, b_ada.reshape(depth, 1, n_out))


def _ffn_kernel(x_ref, mod_ref, gpre_ref, gpost_ref, win_ref, wout_ref, o_ref, act_ref,
                *, sub, ck):
    x = x_ref[...]
    hb = _modulated(x, mod_ref, gpre_ref, sub).astype(BF16)
    for c in range(D_FF // ck):
        g = _dot(hb, win_ref[:, c * ck:(c + 1) * ck])
        u = _dot(hb, win_ref[:, D_FF + c * ck:D_FF + (c + 1) * ck])
        act_ref[:, c * ck:(c + 1) * ck] = (g * jax.nn.sigmoid(g) * u).astype(BF16)
    out = _dot(act_ref[...], wout_ref[...])
    o_ref[...] = _residual(x, out, mod_ref, gpost_ref, sub, 0.5)


def _ffn(x, mod, gpre, gpost, w_in, w_out, *, sub, seq, tm):
    t, d = x.shape
    return pl.pallas_call(
        functools.partial(_ffn_kernel, sub=sub, ck=256),
        grid=(t // tm,),
        in_specs=[
            pl.BlockSpec((tm, d), lambda i: (i, 0)),
            pl.BlockSpec((1, 3 * N_SUB, d), lambda i: ((i * tm) // seq, 0, 0)),
            _const_spec((1, d)),
            _const_spec((1, d)),
            _const_spec(w_in.shape),
            _const_spec(w_out.shape),
        ],
        out_specs=pl.BlockSpec((tm, d), lambda i: (i, 0)),
        out_shape=jax.ShapeDtypeStruct((t, d), F32),
        scratch_shapes=[pltpu.VMEM((tm, D_FF), BF16)],
        compiler_params=_cparams(("parallel",)),
        name="ffn",
    )(x, mod, gpre, gpost, w_in, w_out)


def _rope_pair(t):
    return t + pltpu.roll(t, 64, axis=1)


def _mla_proj_kernel(x_ref, mod_ref, gpre_ref, trig_ref, wd_ref, qn_ref, kvn_ref, wq_ref, wkv_ref,
                     qt_ref, k_ref, vt_ref):
    x = x_ref[...]
    hb = _modulated(x, mod_ref, gpre_ref, 1).astype(BF16)
    down = _dot(hb, wd_ref[...])
    cq = _rms(down[:, :MLA_Q_LORA], qn_ref[...]).astype(BF16)
    ckv = _rms(down[:, MLA_Q_LORA:MLA_Q_LORA + MLA_KV_LORA], kvn_ref[...]).astype(BF16)
    trig = trig_ref[...]
    lane = lax.broadcasted_iota(jnp.int32, trig.shape, 1)
    k_rope = _rope_pair(down[:, MLA_Q_LORA + MLA_KV_LORA:] * trig)
    k_rope = jnp.where(lane < MLA_ROPE, k_rope, 0.0).astype(BF16)
    q = _dot(cq, wq_ref[...])
    kv = _dot(ckv, wkv_ref[...])
    qs = MLA_SCALE * math.log2(math.e)
    for h in range(MLA_HEADS):
        c0 = h * MLA_QK_PAD
        q_rope = _rope_pair(q[:, c0 + LANES:c0 + 2 * LANES] * trig)
        qt_ref[0, h, :LANES, :] = (q[:, c0:c0 + LANES] * qs).T.astype(BF16)
        qt_ref[0, h, LANES:, :] = (q_rope * qs).T.astype(BF16)
        k_ref[:, c0:c0 + LANES] = kv[:, h * MLA_NOPE:(h + 1) * MLA_NOPE].astype(BF16)
        k_ref[:, c0 + LANES:c0 + 2 * LANES] = k_rope
        v0 = MLA_HEADS * MLA_NOPE + h * MLA_V
        vt_ref[0, h, 0] = kv[:, v0:v0 + MLA_V].T.astype(BF16)


def _mla_proj(x, mod, gpre, trig, wd, qn, kvn, wq, wkv, *, seq, tm):
    t, d = x.shape
    nq = MLA_HEADS * MLA_QK_PAD
    nt = seq // tm
    return pl.pallas_call(
        _mla_proj_kernel,
        grid=(t // tm,),
        in_specs=[
            pl.BlockSpec((tm, d), lambda i: (i, 0)),
            pl.BlockSpec((1, 3 * N_SUB, d), lambda i: ((i * tm) // seq, 0, 0)),
            _const_spec((1, d)),
            pl.BlockSpec((tm, LANES), lambda i: (i % (seq // tm), 0)),
            _const_spec(wd.shape),
            _const_spec(qn.shape),
            _const_spec(kvn.shape),
            _const_spec(wq.shape),
            _const_spec(wkv.shape),
        ],
        out_specs=[
            pl.BlockSpec((1, MLA_HEADS, MLA_QK_PAD, tm), lambda i: (i // nt, 0, 0, i % nt)),
            pl.BlockSpec((tm, nq), lambda i: (i, 0)),
            pl.BlockSpec((1, MLA_HEADS, 1, MLA_V, tm), lambda i: (i // nt, 0, i % nt, 0, 0)),
        ],
        out_shape=[
            jax.ShapeDtypeStruct((t // seq, MLA_HEADS, MLA_QK_PAD, seq), BF16),
            jax.ShapeDtypeStruct((t, nq), BF16),
            jax.ShapeDtypeStruct((t // seq, MLA_HEADS, nt, MLA_V, tm), BF16),
        ],
        compiler_params=_cparams(("parallel",)),
        name="mla_proj",
    )(x, mod, gpre, trig, wd, qn, kvn, wq, wkv)


def _flash_kernel(qt_ref, k_ref, vt_ref, o_ref, s_ref, p_ref, acc_ref):
    qt = qt_ref[0, 0]
    tq = qt.shape[1]
    nk, _, tk = vt_ref.shape[2:]
    assert nk == 1 or nk % 2 == 0

    def scores(j, slot):
        r0 = pl.multiple_of(j * tk, tk)
        s = _dot(k_ref[0, pl.ds(r0, tk), :], qt)
        s_ref[slot] = s
        return jnp.max(s.reshape(tk // 8, 8, tq), axis=0)

    def softmax(slot, cmax, m, l):
        m_new = jnp.maximum(m, jnp.max(cmax, axis=0, keepdims=True))
        alpha = jnp.exp2(m - m_new)
        p = jnp.exp2(s_ref[slot] - m_new)
        l = alpha * l + jnp.sum(p.reshape(tk // 8, 8, tq), axis=0)
        p_ref[slot] = p.astype(BF16)
        return m_new, l, alpha

    def weighted(j, slot, alpha):
        acc_ref[...] = alpha * acc_ref[...] + _dot(vt_ref[0, 0, j], p_ref[slot])

    def step(j, slot, cmax, m, l, alpha):
        weighted(j - 1, 1 - slot, alpha)
        cmax_next = scores(j + 1, 1 - slot)
        return (cmax_next,) + softmax(slot, cmax, m, l)

    acc_ref[...] = jnp.zeros_like(acc_ref)
    m = jnp.full((1, tq), -1e30, F32)
    l = jnp.zeros((8, tq), F32)
    cmax = scores(0, 0)
    if nk > 1:
        cmax_next = scores(1, 1)
    m, l, alpha = softmax(0, cmax, m, l)
    if nk > 1:
        unroll = 4 if nk >= 8 else 2
        trips = (nk - 2) // unroll

        def body(t, carry):
            j = 1 + unroll * t
            for u in range(unroll):
                carry = step(j + u, (1 + u) % 2, *carry)
            return carry

        carry = lax.fori_loop(0, trips, body, (cmax_next, m, l, alpha))
        for j in range(1 + unroll * trips, nk - 1):
            carry = step(j, j % 2, *carry)
        cmax, m, l, alpha = carry
        weighted(nk - 2, 0, alpha)
        m, l, alpha = softmax(1, cmax, m, l)
    weighted(nk - 1, (nk - 1) % 2, alpha)
    o_t = acc_ref[...] / jnp.sum(l, axis=0, keepdims=True)
    o_ref[0] = o_t.T.astype(o_ref.dtype)


def _flash(qt, k, vt, *, tq):
    b, s, _ = k.shape
    _, _, nk, dv, tk = vt.shape
    return pl.pallas_call(
        _flash_kernel,
        grid=(b, MLA_HEADS, s // tq),
        in_specs=[
            pl.BlockSpec((1, 1, MLA_QK_PAD, tq), lambda bi, h, i: (bi, h, 0, i)),
            pl.BlockSpec((1, s, MLA_QK_PAD), lambda bi, h, i: (bi, 0, h)),
            pl.BlockSpec((1, 1, nk, dv, tk), lambda bi, h, i: (bi, h, 0, 0, 0)),
        ],
        out_specs=pl.BlockSpec((1, tq, MLA_V), lambda bi, h, i: (bi, i, h)),
        out_shape=jax.ShapeDtypeStruct((b, s, MLA_HEADS * MLA_V), BF16),
        scratch_shapes=[
            pltpu.VMEM((2, tk, tq), F32),
            pltpu.VMEM((2, tk, tq), BF16),
            pltpu.VMEM((dv, tq), F32),
        ],
        compiler_params=_cparams(("parallel", "parallel", "arbitrary")),
        name="mla_flash",
    )(qt, k, vt)


def _out_proj_kernel(a_ref, x_ref, mod_ref, gpost_ref, w_ref, o_ref):
    out = _dot(a_ref[...], w_ref[...])
    o_ref[...] = _residual(x_ref[...], out, mod_ref, gpost_ref, 1, 1.0)


def _out_proj(a, x, mod, gpost, w, *, seq, tm):
    t, d = x.shape
    return pl.pallas_call(
        _out_proj_kernel,
        grid=(t // tm,),
        in_specs=[
            pl.BlockSpec((tm, a.shape[1]), lambda i: (i, 0)),
            pl.BlockSpec((tm, d), lambda i: (i, 0)),
            pl.BlockSpec((1, 3 * N_SUB, d), lambda i: ((i * tm) // seq, 0, 0)),
            _const_spec((1, d)),
            _const_spec(w.shape),
        ],
        out_specs=pl.BlockSpec((tm, d), lambda i: (i, 0)),
        out_shape=jax.ShapeDtypeStruct((t, d), F32),
        compiler_params=_cparams(("parallel",)),
        name="out_proj",
    )(a, x, mod, gpost, w)


def _gdn_proj_kernel(xp_ref, x_ref, xn_ref, mod_ref, gpre_ref, wqkv_ref, wg_ref, wab_ref, conv_ref,
                     aneg_ref, dtb_ref, q_ref, k_ref, v_ref, gate_ref, gb_ref, p_ref,
                     *, tiles_per_seq):
    i = pl.program_id(0)
    tm = x_ref.shape[0]
    first = (i % tiles_per_seq) == 0
    last = (i % tiles_per_seq) == tiles_per_seq - 1

    def proj_rows(xr):
        return _dot(_modulated(xr, mod_ref, gpre_ref, 1).astype(BF16), wqkv_ref[...])

    hb = _modulated(x_ref[...], mod_ref, gpre_ref, 1).astype(BF16)
    p_prev = jnp.where(first, 0.0, proj_rows(xp_ref[...]))
    p_main = _dot(hb, wqkv_ref[...])
    p_next = jnp.where(last, 0.0, proj_rows(xn_ref[...]))
    n_slab = p_ref.shape[0]
    for c in range(n_slab):
        cs = slice(c * LANES, (c + 1) * LANES)
        p_ref[c, 0:GDN_HALO, :] = p_prev[:, cs]
        p_ref[c, GDN_HALO:GDN_HALO + tm, :] = p_main[:, cs]
        p_ref[c, GDN_HALO + tm:, :] = p_next[:, cs]

    def l2n(z):
        return z * lax.rsqrt(jnp.sum(z * z, axis=-1, keepdims=True) + NORM_EPS)

    pad = GDN_CONV // 2
    for c in range(n_slab):
        cs = slice(c * LANES, (c + 1) * LANES)
        acc = None
        for tap in range(GDN_CONV):
            r0 = GDN_HALO - pad + tap
            term = p_ref[c, r0:r0 + tm, :] * conv_ref[tap:tap + 1, cs]
            acc = term if acc is None else acc + term
        z = acc * jax.nn.sigmoid(acc)
        if c < GDN_HEADS:
            q_ref[:, cs] = l2n(z) * (GDN_DK ** -0.5)
        elif c < 2 * GDN_HEADS:
            k_ref[:, c * LANES - GDN_QK:(c + 1) * LANES - GDN_QK] = l2n(z)
        else:
            v_ref[:, c * LANES - 2 * GDN_QK:(c + 1) * LANES - 2 * GDN_QK] = z
    gate_ref[...] = _dot(hb, wg_ref[...])
    ab = _dot(hb, wab_ref[...])
    z = ab + dtb_ref[...]
    softplus = jnp.maximum(z, 0.0) + jnp.log(1.0 + jnp.exp(-jnp.abs(z)))
    lane = lax.broadcasted_iota(jnp.int32, ab.shape, 1)
    gb_ref[...] = jnp.where(lane < 2 * GDN_HEADS, aneg_ref[...] * softplus, jax.nn.sigmoid(ab))


def _gdn_proj(x, mod, gpre, wqkv, wg, wab, conv, aneg, dtb, *, seq, tm):
    t, d = x.shape
    hb = tm // GDN_HALO
    nblk8 = t // GDN_HALO
    n_conv = wqkv.shape[1]
    return pl.pallas_call(
        functools.partial(_gdn_proj_kernel, tiles_per_seq=seq // tm),
        grid=(t // tm,),
        in_specs=[
            pl.BlockSpec((GDN_HALO, d), lambda i: (jnp.maximum(i * hb - 1, 0), 0)),
            pl.BlockSpec((tm, d), lambda i: (i, 0)),
            pl.BlockSpec((GDN_HALO, d), lambda i: (jnp.minimum((i + 1) * hb, nblk8 - 1), 0)),
            pl.BlockSpec((1, 3 * N_SUB, d), lambda i: ((i * tm) // seq, 0, 0)),
            _const_spec((1, d)),
            _const_spec(wqkv.shape),
            _const_spec(wg.shape),
            _const_spec(wab.shape),
            _const_spec(conv.shape),
            _const_spec(aneg.shape),
            _const_spec(dtb.shape),
        ],
        out_specs=[
            pl.BlockSpec((tm, GDN_QK), lambda i: (i, 0)),
            pl.BlockSpec((tm, GDN_QK), lambda i: (i, 0)),
            pl.BlockSpec((tm, GDN_VW), lambda i: (i, 0)),
            pl.BlockSpec((tm, GDN_VW), lambda i: (i, 0)),
            pl.BlockSpec((tm, LANES), lambda i: (i, 0)),
        ],
        out_shape=[
            jax.ShapeDtypeStruct((t, GDN_QK), F32),
            jax.ShapeDtypeStruct((t, GDN_QK), F32),
            jax.ShapeDtypeStruct((t, GDN_VW), F32),
            jax.ShapeDtypeStruct((t, GDN_VW), F32),
            jax.ShapeDtypeStruct((t, LANES), F32),
        ],
        scratch_shapes=[pltpu.VMEM((n_conv // LANES, tm + 2 * GDN_HALO, LANES), F32)],
        compiler_params=_cparams(("parallel",)),
        name="gdn_proj",
    )(x, x, x, mod, gpre, wqkv, wg, wab, conv, aneg, dtb)


def _gdn_cumsum_kernel(gb_ref, o_ref):
    g = gb_ref[...]
    tm = g.shape[0]
    row = lax.broadcasted_iota(jnp.int32, (tm, tm), 0)
    col = lax.broadcasted_iota(jnp.int32, (tm, tm), 1)
    sh = int(math.log2(GDN_CHUNK))
    same = lax.shift_right_logical(row, sh) == lax.shift_right_logical(col, sh)
    lower = jnp.where(same & (col <= row), 1.0, 0.0).astype(BF16)
    upper = jnp.where(same & (col >= row), 1.0, 0.0).astype(BF16)
    g1 = g.astype(BF16)
    r1 = g - g1.astype(F32)
    g2 = r1.astype(BF16)
    g3 = (r1 - g2.astype(F32)).astype(BF16)
    pre = _dot(lower, g1) + _dot(lower, g2) + _dot(lower, g3)
    suf = _dot(upper, g1) + _dot(upper, g2) + _dot(upper, g3)
    lane = lax.broadcasted_iota(jnp.int32, g.shape, 1)
    o_ref[...] = jnp.where(lane < GDN_HEADS, pre, jnp.where(lane < 2 * GDN_HEADS, suf, g))


def _gdn_cumsum(gb, *, tm):
    t = gb.shape[0]
    return pl.pallas_call(
        _gdn_cumsum_kernel,
        grid=(t // tm,),
        in_specs=[pl.BlockSpec((tm, LANES), lambda i: (i, 0))],
        out_specs=pl.BlockSpec((tm, LANES), lambda i: (i, 0)),
        out_shape=jax.ShapeDtypeStruct((t, LANES), F32),
        compiler_params=_cparams(("parallel",)),
        name="gdn_cumsum",
    )(gb)


def _gdn_delta_kernel(q_ref, k_ref, v_ref, gcol_ref, bcol_ref, grow_ref, o_ref, state_ref,
                      *, reverse):
    c = GDN_CHUNK
    nchunk = q_ref.shape[0] // c

    @pl.when(pl.program_id(1) == 0)
    def _():
        state_ref[...] = jnp.zeros_like(state_ref)

    row = lax.broadcasted_iota(jnp.int32, (c, c), 0)
    col = lax.broadcasted_iota(jnp.int32, (c, c), 1)
    if reverse:
        row, col = col, row
    tri = row >= col
    strict = row > col
    eye = jnp.where(row == col, 1.0, 0.0)
    level_masks = []
    for lv in range(int(math.log2(c))):
        rb = lax.shift_right_logical(row, lv)
        cb = lax.shift_right_logical(col, lv)
        level_masks.append((lax.shift_right_logical(rb, 1) == lax.shift_right_logical(cb, 1))
                           & ((rb & 1) == 1) & ((cb & 1) == 0))
    last_row = 0 if reverse else c - 1

    heads = range(GDN_HEADS)
    order = list(reversed(range(nchunk))) if reverse else list(range(nchunk))
    probs = [(j, h) for j in order for h in heads]
    rows = lambda j: slice(j * c, (j + 1) * c)
    cols = lambda h: slice(h * GDN_DK, (h + 1) * GDN_DK)

    qs = [q_ref[rows(j), cols(h)] for j, h in probs]
    ks = [k_ref[rows(j), cols(h)] for j, h in probs]
    vs = [v_ref[rows(j), cols(h)] for j, h in probs]
    gcbs = [jnp.broadcast_to(gcol_ref[0, rows(j), h:h + 1], (c, GDN_DK)) for j, h in probs]
    betas = [jnp.broadcast_to(bcol_ref[0, rows(j), h:h + 1], (c, GDN_DK)) for j, h in probs]
    n = len(probs)
    decays = []
    for p, (j, h) in enumerate(probs):
        grb = jnp.broadcast_to(grow_ref[0, j, h:h + 1, :], (c, c))
        decays.append(jnp.where(tri, jnp.exp(jnp.where(tri, gcbs[p][:, :c] - grb, 0.0)), 0.0))
    kbs = [ks[p] * betas[p] for p in range(n)]
    a2s = [_dot_nt(jnp.concatenate([kbs[p], qs[p]], axis=0).astype(BF16), ks[p].astype(BF16))
           for p in range(n)]
    ms = [jnp.where(strict, a2s[p][:c] * decays[p], 0.0) for p in range(n)]
    qks = [(a2s[p][c:] * decays[p]).astype(BF16) for p in range(n)]
    tinvs = [eye - jnp.where(level_masks[0], ms[p], 0.0) for p in range(n)]
    for lm in level_masks[1:]:
        tbs = [t.astype(BF16) for t in tinvs]
        tcs = [_dot(tbs[p], jnp.where(lm, ms[p], 0.0).astype(BF16)).astype(BF16) for p in range(n)]
        tinvs = [tinvs[p] - _dot(tcs[p], tbs[p]) for p in range(n)]
    egs = [jnp.exp(g) for g in gcbs]
    sols = [_dot(tinvs[p].astype(BF16),
                 jnp.concatenate([vs[p] * betas[p], kbs[p] * egs[p]], axis=1).astype(BF16))
            for p in range(n)]
    g_lasts = [g[last_row:last_row + 1, :] for g in gcbs]
    k_decs = [(ks[p] * jnp.exp(g_lasts[p] - gcbs[p])).astype(BF16) for p in range(n)]
    wq_lhs = [jnp.concatenate([sols[p][:, GDN_DV:], qs[p] * egs[p]], axis=0).astype(BF16)
              for p in range(n)]

    sts = [state_ref[h] for h in heads]
    for step, j in enumerate(order):
        ps = [step * GDN_HEADS + h for h in heads]
        wqs = [_dot(wq_lhs[ps[h]], sts[h].astype(BF16)) for h in heads]
        vnbs = [(sols[ps[h]][:, :GDN_DV] - wqs[h][:c]).astype(BF16) for h in heads]
        for h in heads:
            o_ref[rows(j), cols(h)] = wqs[h][c:] + _dot(qks[ps[h]], vnbs[h])
        sts = [sts[h] * jnp.exp(g_lasts[ps[h]]) + _dot_tn(k_decs[ps[h]], vnbs[h]) for h in heads]
    for h in heads:
        state_ref[h] = sts[h]


def _gdn_delta(q, k, v, gcol, bcol, grow, *, batch, seq, tb, reverse):
    t = q.shape[0]
    nb = seq // tb
    cpb = tb // GDN_CHUNK

    def blk(bi, i):
        return bi * nb + ((nb - 1 - i) if reverse else i)

    def seq_blk(bi, i):
        return (nb - 1 - i) if reverse else i

    return pl.pallas_call(
        functools.partial(_gdn_delta_kernel, reverse=reverse),
        grid=(batch, nb),
        in_specs=[
            pl.BlockSpec((tb, GDN_QK), lambda bi, i: (blk(bi, i), 0)),
            pl.BlockSpec((tb, GDN_QK), lambda bi, i: (blk(bi, i), 0)),
            pl.BlockSpec((tb, GDN_VW), lambda bi, i: (blk(bi, i), 0)),
            pl.BlockSpec((1, tb, GDN_HEADS), lambda bi, i: (bi, seq_blk(bi, i), 0)),
            pl.BlockSpec((1, tb, GDN_HEADS), lambda bi, i: (bi, seq_blk(bi, i), 0)),
            pl.BlockSpec((1, cpb, GDN_HEADS, GDN_CHUNK), lambda bi, i: (bi, seq_blk(bi, i), 0, 0)),
        ],
        out_specs=pl.BlockSpec((tb, GDN_VW), lambda bi, i: (blk(bi, i), 0)),
        out_shape=jax.ShapeDtypeStruct((t, GDN_VW), F32),
        scratch_shapes=[pltpu.VMEM((GDN_HEADS, GDN_DK, GDN_DV), F32)],
        compiler_params=_cparams(("parallel", "arbitrary")),
        name="gdn_delta_bw" if reverse else "gdn_delta_fw",
    )(q, k, v, gcol, bcol, grow)


def _gdn_out_kernel(of_ref, ob_ref, gate_ref, x_ref, mod_ref, gpost_ref, onorm_ref, w_ref, o_ref,
                    a_ref):
    o = of_ref[...] + ob_ref[...]
    gate = gate_ref[...]
    og = gate * jax.nn.sigmoid(gate)
    for h in range(GDN_HEADS):
        cs = slice(h * GDN_DV, (h + 1) * GDN_DV)
        a_ref[:, cs] = (_rms(o[:, cs], onorm_ref[...]) * og[:, cs]).astype(BF16)
    out = _dot(a_ref[...], w_ref[...])
    o_ref[...] = _residual(x_ref[...], out, mod_ref, gpost_ref, 1, 1.0)


def _gdn_out(o_fw, o_bw, gate, x, mod, gpost, onorm, w, *, seq, tm):
    t, d = x.shape
    return pl.pallas_call(
        _gdn_out_kernel,
        grid=(t // tm,),
        in_specs=[
            pl.BlockSpec((tm, GDN_VW), lambda i: (i, 0)),
            pl.BlockSpec((tm, GDN_VW), lambda i: (i, 0)),
            pl.BlockSpec((tm, GDN_VW), lambda i: (i, 0)),
            pl.BlockSpec((tm, d), lambda i: (i, 0)),
            pl.BlockSpec((1, 3 * N_SUB, d), lambda i: ((i * tm) // seq, 0, 0)),
            _const_spec((1, d)),
            _const_spec(onorm.shape),
            _const_spec(w.shape),
        ],
        out_specs=pl.BlockSpec((tm, d), lambda i: (i, 0)),
        out_shape=jax.ShapeDtypeStruct((t, d), F32),
        scratch_shapes=[pltpu.VMEM((tm, GDN_VW), BF16)],
        compiler_params=_cparams(("parallel",)),
        name="gdn_out",
    )(o_fw, o_bw, gate, x, mod, gpost, onorm, w)


def _fnet_chan_kernel(x_ref, mod_ref, gpre_ref, cs_ref, o_ref):
    hb = _modulated(x_ref[...], mod_ref, gpre_ref, 1).astype(BF16)
    cg = hb.shape[1] // FNET_GROUPS
    for g in range(FNET_GROUPS):
        ab = _dot(hb[:, g * cg:(g + 1) * cg], cs_ref[...])
        o_ref[0, 0, :, g * cg:(g + 1) * cg] = ab[:, :cg].astype(BF16)
        o_ref[0, 1, :, g * cg:(g + 1) * cg] = ab[:, cg:].astype(BF16)


def _fnet_chan(x, mod, gpre, cs, *, batch, seq, tm):
    t, d = x.shape
    nt = seq // tm
    return pl.pallas_call(
        _fnet_chan_kernel,
        grid=(t // tm,),
        in_specs=[
            pl.BlockSpec((tm, d), lambda i: (i, 0)),
            pl.BlockSpec((1, 3 * N_SUB, d), lambda i: ((i * tm) // seq, 0, 0)),
            _const_spec((1, d)),
            _const_spec(cs.shape),
        ],
        out_specs=pl.BlockSpec((1, 2, tm, d), lambda i: (i // nt, 0, i % nt, 0)),
        out_shape=jax.ShapeDtypeStruct((batch, 2, seq, d), BF16),
        compiler_params=_cparams(("parallel",)),
        name="fnet_chan",
    )(x, mod, gpre, cs)


def _fnet_s1_kernel(m1_ref, ab_ref, y_ref):
    s1 = ab_ref.shape[2]
    ab = jnp.concatenate([ab_ref[0, 0], ab_ref[0, 1]], axis=0)
    y = _dot(m1_ref[...], ab)
    y_ref[0, 0] = y[:s1]
    y_ref[0, 1] = y[s1:]


def _fnet_s1(m1, ab, *, tn):
    batch, _, s1, n = ab.shape
    return pl.pallas_call(
        _fnet_s1_kernel,
        grid=(batch, n // tn),
        in_specs=[
            _const_spec(m1.shape),
            pl.BlockSpec((1, 2, s1, tn), lambda bi, j: (bi, 0, 0, j)),
        ],
        out_specs=pl.BlockSpec((1, 2, s1, tn), lambda bi, j: (bi, 0, 0, j)),
        out_shape=jax.ShapeDtypeStruct((batch, 2, s1, n), F32),
        compiler_params=_cparams(("parallel", "parallel")),
        name="fnet_s1",
    )(m1, ab)


def _fnet_s2_kernel(m2_ref, tw_ref, y_ref, x_ref, mod_ref, gpost_ref, w_ref, b_ref, o_ref):
    d = x_ref.shape[2]
    reps = d // LANES
    cos = jnp.concatenate([tw_ref[0, 0]] * reps, axis=1)
    sin = jnp.concatenate([tw_ref[0, 1]] * reps, axis=1)
    yr = y_ref[0, 0]
    yi = y_ref[0, 1]
    z = jnp.concatenate([yr * cos + yi * sin, yi * cos - yr * sin], axis=0).astype(BF16)
    f = _dot(m2_ref[...], z)
    out = _dot(f.astype(BF16), w_ref[...]) + b_ref[...]
    o_ref[0] = _residual(x_ref[0], out, mod_ref, gpost_ref, 1, 1.0)


def _fnet_s2(m2, tw, y, x, mod, gpost, w, b, *, s1, s2):
    batch = y.shape[0]
    d = w.shape[0]
    return pl.pallas_call(
        _fnet_s2_kernel,
        grid=(batch, s1),
        in_specs=[
            _const_spec(m2.shape),
            pl.BlockSpec((1, 2, s2, LANES), lambda bi, t1: (t1, 0, 0, 0)),
            pl.BlockSpec((1, 2, s2, d), lambda bi, t1: (bi, 0, t1, 0)),
            pl.BlockSpec((1, s2, d), lambda bi, t1: (bi, 0, t1)),
            pl.BlockSpec((1, 3 * N_SUB, d), lambda bi, t1: (bi, 0, 0)),
            _const_spec((1, d)),
            _const_spec(w.shape),
            _const_spec((1, d)),
        ],
        out_specs=pl.BlockSpec((1, s2, d), lambda bi, t1: (bi, 0, t1)),
        out_shape=jax.ShapeDtypeStruct((batch, s2, s1 * d), F32),
        compiler_params=_cparams(("parallel", "parallel")),
        name="fnet_s2",
    )(m2, tw, y, x, mod, gpost, w, b)


def _rope_table(seq):
    half = MLA_ROPE // 2
    pos = jnp.arange(seq, dtype=F32)
    inv_freq = ROPE_THETA ** (-jnp.arange(half, dtype=F32) / half)
    ang = pos[:, None] * inv_freq[None, :]
    cos, sin = jnp.cos(ang), jnp.sin(ang)
    return jnp.concatenate([cos, cos, -sin, sin], axis=1)


def _swap_halves(w):
    half = w.shape[-1] // 2
    return jnp.concatenate([w[..., half:], w[..., :half]], axis=-1)


def _prep_mla(w_down, w_uq, w_ukv):
    d = w_down.shape[0]
    rope = w_down[:, MLA_Q_LORA + MLA_KV_LORA:]
    wd = jnp.concatenate([w_down, _swap_halves(rope)], axis=1).astype(BF16)
    uq = w_uq.reshape(MLA_Q_LORA, MLA_HEADS, MLA_NOPE + MLA_ROPE)
    uq_rope = uq[..., MLA_NOPE:]
    wq = jnp.concatenate([uq, _swap_halves(uq_rope)], axis=-1)
    wq = wq.reshape(MLA_Q_LORA, MLA_HEADS * MLA_QK_PAD).astype(BF16)
    ukv = w_ukv.reshape(MLA_KV_LORA, MLA_HEADS, MLA_NOPE + MLA_V)
    wkv = jnp.concatenate([ukv[..., :MLA_NOPE].reshape(MLA_KV_LORA, -1),
                           ukv[..., MLA_NOPE:].reshape(MLA_KV_LORA, -1)], axis=1).astype(BF16)
    del d
    return wd, wq, wkv


def _fnet_tables(seq, cg):
    def trig(n):
        idx = jnp.arange(n, dtype=jnp.int32)
        ang = ((idx[:, None] * idx[None, :]) % n).astype(F32) * (2.0 * math.pi / n)
        return jnp.cos(ang), jnp.sin(ang)

    cc, sc = trig(cg)
    cs = (jnp.concatenate([cc, sc], axis=1) * (cg ** -0.5)).astype(BF16)
    s2 = min(seq, LANES)
    s1 = seq // s2
    c1, sn1 = trig(s1)
    m1 = (jnp.block([[c1, -sn1], [-sn1, -c1]]) * (s1 ** -0.5)).astype(BF16)
    c2, sn2 = trig(s2)
    m2 = (jnp.concatenate([c2, sn2], axis=1) * (s2 ** -0.5)).astype(BF16)
    t1 = jnp.arange(s1, dtype=jnp.int32)[:, None]
    k2 = jnp.arange(s2, dtype=jnp.int32)[None, :]
    ang = (t1 * k2).astype(F32) * (2.0 * math.pi / seq)
    tw = jnp.stack([jnp.cos(ang), jnp.sin(ang)], axis=1)
    tw = jnp.broadcast_to(tw[..., None], (s1, 2, s2, LANES))
    return cs, m1, m2, tw


def _pick(n, pref):
    return pref if n % pref == 0 else n


def _trunk(x3, mods, prm):
    batch, seq, d = x3.shape
    t = batch * seq
    x = x3.reshape(t, d)
    tm = _pick(seq, 512)
    row = lambda v: v.reshape(1, -1)
    trig = _rope_table(seq)
    for l in range(mods.shape[0]):
        i = l // 3
        kind = l % 3
        mod = mods[l]
        gpre = prm['norm_pre'][l]
        gpost = prm['norm_post'][l]
        x = _ffn(x, mod, row(gpre[0]), row(gpost[0]), prm['ffn_w_in'][l][0], prm['ffn_w_out'][l][0],
                 sub=0, seq=seq, tm=tm)
        if kind == 0:
            wd, wq, wkv = prm['mla'][i]
            qt, k, vt = _mla_proj(x, mod, row(gpre[1]), trig, wd, row(prm['mla_q_norm'][i]),
                                  row(prm['mla_kv_norm'][i]), wq, wkv, seq=seq, tm=tm)
            o = _flash(qt, k.reshape(batch, seq, -1), vt, tq=_pick(seq, 512))
            x = _out_proj(o.reshape(t, -1), x, mod, row(gpost[1]), prm['mla_w_out'][i], seq=seq, tm=tm)
        elif kind == 1:
            g = prm['gdn'][i]
            q, k, v, gate, gb = _gdn_proj(x, mod, row(gpre[1]), g['wqkv'], g['wg'], g['wab'], g['conv'],
                                          g['aneg'], g['dtb'], seq=seq, tm=_pick(seq, 256))
            gc = _gdn_cumsum(gb, tm=tm)
            nh = GDN_HEADS
            gc3 = gc.reshape(batch, seq, LANES)
            outs = []
            for dr in range(2):
                gcol = gc3[:, :, dr * nh:(dr + 1) * nh]
                bcol = gc3[:, :, 2 * nh + dr * nh:2 * nh + (dr + 1) * nh]
                grow = jnp.swapaxes(gcol.reshape(batch, seq // GDN_CHUNK, GDN_CHUNK, nh), 2, 3)
                outs.append(_gdn_delta(q, k, v, gcol, bcol, grow, batch=batch, seq=seq,
                                       tb=_pick(seq, 256), reverse=bool(dr)))
            x = _gdn_out(outs[0], outs[1], gate, x, mod, row(gpost[1]), row(prm['gdn_o_norm'][i]),
                         g['wout'], seq=seq, tm=tm)
        else:
            cs, m1, m2, tw = _fnet_tables(seq, d // FNET_GROUPS)
            s1, _, s2, _ = tw.shape
            ab = _fnet_chan(x, mod, row(gpre[1]), cs, batch=batch, seq=seq, tm=tm)
            y = _fnet_s1(m1, ab.reshape(batch, 2, s1, s2 * d), tn=2 * d)
            x = _fnet_s2(m2, tw, y.reshape(batch, 2, seq, d), x.reshape(batch, s2, s1 * d), mod,
                         row(gpost[1]), prm['fnet_w_out'][i], row(prm['fnet_b_out'][i]),
                         s1=s1, s2=s2).reshape(t, d)
        x = _ffn(x, mod, row(gpre[2]), row(gpost[2]), prm['ffn_w_in'][l][1], prm['ffn_w_out'][l][1],
                 sub=2, seq=seq, tm=tm)
    return x.reshape(batch, seq, d)


def _prep_gdn(w_in, conv, a_log, dt_bias, w_out):
    n_conv = 2 * GDN_QK + GDN_VW
    wab = jnp.pad(w_in[:, n_conv + GDN_VW:], ((0, 0), (0, LANES - 4 * GDN_HEADS)))
    pad16 = lambda v: jnp.pad(v.reshape(1, -1).astype(F32), ((0, 0), (0, LANES - 2 * GDN_HEADS)))
    return {
        'wqkv': w_in[:, :n_conv].astype(BF16),
        'wg': w_in[:, n_conv:n_conv + GDN_VW].astype(BF16),
        'wab': wab.astype(BF16),
        'conv': conv.astype(F32),
        'aneg': pad16(-jnp.exp(a_log.astype(F32))),
        'dtb': pad16(dt_bias),
        'wout': w_out.astype(BF16),
    }


def kernel(x_prompt, x_sample, c_prompt, c_sample, w_ada, b_ada, norm_pre, norm_post, ffn_w_in, ffn_w_out, mla_w_down, mla_q_norm, mla_kv_norm, mla_w_uq, mla_w_ukv, mla_w_out, gdn_w_in, gdn_conv, gdn_a_log, gdn_dt_bias, gdn_o_norm, gdn_w_out, fnet_w_out, fnet_b_out):
    d = x_prompt.shape[-1]
    prm = {
        'norm_pre': norm_pre, 'norm_post': norm_post,
        'ffn_w_in': ffn_w_in.astype(BF16), 'ffn_w_out': ffn_w_out.astype(BF16),
        'mla': [_prep_mla(mla_w_down[i], mla_w_uq[i], mla_w_ukv[i]) for i in range(mla_w_down.shape[0])],
        'mla_q_norm': mla_q_norm, 'mla_kv_norm': mla_kv_norm, 'mla_w_out': mla_w_out.astype(BF16),
        'gdn': [_prep_gdn(gdn_w_in[i], gdn_conv[i], gdn_a_log[i], gdn_dt_bias[i], gdn_w_out[i])
                for i in range(gdn_w_in.shape[0])],
        'gdn_o_norm': gdn_o_norm,
        'fnet_w_out': fnet_w_out.astype(BF16), 'fnet_b_out': fnet_b_out,
    }
    nbp = c_prompt.shape[0]
    c_all = jnp.concatenate([c_prompt, c_sample], axis=0)
    mods = _ada_mod(c_all, w_ada, b_ada).reshape(w_ada.shape[0], c_all.shape[0], 3 * N_SUB, d)
    y_prompt = _trunk(x_prompt, mods[:, :nbp], prm)
    y_sample = _trunk(x_sample, mods[:, nbp:], prm)
    return (y_prompt, y_sample)
```

```python
import functools
import math

import jax
import jax.numpy as jnp
from jax import lax
from jax.experimental import pallas as pl
from jax.experimental.pallas import tpu as pltpu

F32 = jnp.float32
BF16 = jnp.bfloat16

N_SUB = 3
D_FF = 2816
NORM_EPS = 1e-6

MLA_HEADS = 8
MLA_Q_LORA = 384
MLA_KV_LORA = 256
MLA_NOPE = 128
MLA_ROPE = 64
MLA_V = 128
ROPE_THETA = 10000.0
MLA_SCALE = (MLA_NOPE + MLA_ROPE) ** -0.5
MLA_QK_PAD = 256
MLA_V_PAD = MLA_V + 16

GDN_HEADS = 8
GDN_DK = 128
GDN_DV = 128
GDN_QK = GDN_HEADS * GDN_DK
GDN_VW = GDN_HEADS * GDN_DV
GDN_CONV = 5
GDN_CHUNK = 64
GDN_HALO = 8

FNET_GROUPS = 8

LANES = 128
VMEM_LIMIT = 56 * 1024 * 1024


def _cparams(sem):
    return pltpu.CompilerParams(dimension_semantics=sem, vmem_limit_bytes=VMEM_LIMIT)


def _const_spec(shape):
    nd = len(shape)
    return pl.BlockSpec(shape, lambda *_: (0,) * nd, pipeline_mode=pl.Buffered(1))


def _rms(x, g):
    ms = jnp.mean(x * x, axis=-1, keepdims=True)
    return x * lax.rsqrt(ms + NORM_EPS) * g


def _modulated(x, mod_ref, gpre_ref, sub):
    shift = mod_ref[0, 3 * sub:3 * sub + 1, :]
    scale = mod_ref[0, 3 * sub + 1:3 * sub + 2, :]
    return _rms(x, gpre_ref[...]) * (1.0 + scale) + shift


def _residual(x, out, mod_ref, gpost_ref, sub, weight):
    gate = mod_ref[0, 3 * sub + 2:3 * sub + 3, :]
    return x + (weight * gate) * _rms(out, gpost_ref[...])


def _dot(a, b):
    return jnp.dot(a, b, preferred_element_type=F32)


def _dot_nt(a, b):
    return lax.dot_general(a, b, (((1,), (1,)), ((), ())), preferred_element_type=F32)


def _dot_tn(a, b):
    return lax.dot_general(a, b, (((0,), (0,)), ((), ())), preferred_element_type=F32)


def _ada_kernel(c_ref, w_ref, b_ref, o_ref):
    c = c_ref[...]
    sc = (c * jax.nn.sigmoid(c)).astype(BF16)
    o_ref[0] = _dot(sc, w_ref[0].astype(BF16)) + b_ref[0]


def _ada_mod(c_all, w_ada, b_ada):
    nb, d = c_all.shape
    depth, _, n_out = w_ada.shape
    tn = n_out // 8
    return pl.pallas_call(
        _ada_kernel,
        grid=(depth, n_out // tn),
        in_specs=[
            pl.BlockSpec((nb, d), lambda l, j: (0, 0)),
            pl.BlockSpec((1, d, tn), lambda l, j: (l, 0, j)),
            pl.BlockSpec((1, 1, tn), lambda l, j: (l, 0, j)),
        ],
        out_specs=pl.BlockSpec((1, nb, tn), lambda l, j: (l, 0, j)),
        out_shape=jax.ShapeDtypeStruct((depth, nb, n_out), F32),
        compiler_params=_cparams(("arbitrary", "arbitrary")),
        name="ada_mod",
    )(c_all, w_ada, b_ada.reshape(depth, 1, n_out))


def _ffn_kernel(x_ref, mod_ref, gpre_ref, gpost_ref, win_ref, wout_ref, o_ref, act_ref,
                *, sub, ck):
    x = x_ref[...]
    hb = _modulated(x, mod_ref, gpre_ref, sub).astype(BF16)
    for c in range(D_FF // ck):
        g = _dot(hb, win_ref[:, c * ck:(c + 1) * ck])
        u = _dot(hb, win_ref[:, D_FF + c * ck:D_FF + (c + 1) * ck])
        act_ref[:, c * ck:(c + 1) * ck] = (g * jax.nn.sigmoid(g) * u).astype(BF16)
    out = _dot(act_ref[...], wout_ref[...])
    o_ref[...] = _residual(x, out, mod_ref, gpost_ref, sub, 0.5)


def _ffn(x, mod, gpre, gpost, w_in, w_out, *, sub, seq, tm):
    t, d = x.shape
    return pl.pallas_call(
        functools.partial(_ffn_kernel, sub=sub, ck=256),
        grid=(t // tm,),
        in_specs=[
            pl.BlockSpec((tm, d), lambda i: (i, 0)),
            pl.BlockSpec((1, 3 * N_SUB, d), lambda i: ((i * tm) // seq, 0, 0)),
            _const_spec((1, d)),
            _const_spec((1, d)),
            _const_spec(w_in.shape),
            _const_spec(w_out.shape),
        ],
        out_specs=pl.BlockSpec((tm, d), lambda i: (i, 0)),
        out_shape=jax.ShapeDtypeStruct((t, d), F32),
        scratch_shapes=[pltpu.VMEM((tm, D_FF), BF16)],
        compiler_params=_cparams(("parallel",)),
        name="ffn",
    )(x, mod, gpre, gpost, w_in, w_out)


def _rope_pair(t):
    return t + pltpu.roll(t, 64, axis=1)


def _mla_proj_kernel(x_ref, mod_ref, gpre_ref, trig_ref, wd_ref, qn_ref, kvn_ref, wq_ref, wkv_ref,
                     qt_ref, k_ref, vt_ref):
    x = x_ref[...]
    hb = _modulated(x, mod_ref, gpre_ref, 1).astype(BF16)
    down = _dot(hb, wd_ref[...])
    cq = _rms(down[:, :MLA_Q_LORA], qn_ref[...]).astype(BF16)
    ckv = _rms(down[:, MLA_Q_LORA:MLA_Q_LORA + MLA_KV_LORA], kvn_ref[...]).astype(BF16)
    trig = trig_ref[...]
    lane = lax.broadcasted_iota(jnp.int32, trig.shape, 1)
    k_rope = _rope_pair(down[:, MLA_Q_LORA + MLA_KV_LORA:] * trig)
    k_rope = jnp.where(lane < MLA_ROPE, k_rope, 0.0).astype(BF16)
    q = _dot(cq, wq_ref[...])
    kv = _dot(ckv, wkv_ref[...])
    qs = MLA_SCALE * math.log2(math.e)
    for h in range(MLA_HEADS):
        c0 = h * MLA_QK_PAD
        q_rope = _rope_pair(q[:, c0 + LANES:c0 + 2 * LANES] * trig)
        qt_ref[0, h, :LANES, :] = (q[:, c0:c0 + LANES] * qs).T.astype(BF16)
        qt_ref[0, h, LANES:, :] = (q_rope * qs).T.astype(BF16)
        k_ref[:, c0:c0 + LANES] = kv[:, h * MLA_NOPE:(h + 1) * MLA_NOPE].astype(BF16)
        k_ref[:, c0 + LANES:c0 + 2 * LANES] = k_rope
        v0 = MLA_HEADS * MLA_NOPE + h * MLA_V
        vt_ref[0, h, 0, :MLA_V, :] = kv[:, v0:v0 + MLA_V].T.astype(BF16)
        pad_row = lax.broadcasted_iota(jnp.int32, (MLA_V_PAD - MLA_V, x.shape[0]), 0)
        vt_ref[0, h, 0, MLA_V:, :] = jnp.where(pad_row == 0, 1.0, 0.0).astype(BF16)


def _mla_proj(x, mod, gpre, trig, wd, qn, kvn, wq, wkv, *, seq, tm):
    t, d = x.shape
    nq = MLA_HEADS * MLA_QK_PAD
    nt = seq // tm
    return pl.pallas_call(
        _mla_proj_kernel,
        grid=(t // tm,),
        in_specs=[
            pl.BlockSpec((tm, d), lambda i: (i, 0)),
            pl.BlockSpec((1, 3 * N_SUB, d), lambda i: ((i * tm) // seq, 0, 0)),
            _const_spec((1, d)),
            pl.BlockSpec((tm, LANES), lambda i: (i % (seq // tm), 0)),
            _const_spec(wd.shape),
            _const_spec(qn.shape),
            _const_spec(kvn.shape),
            _const_spec(wq.shape),
            _const_spec(wkv.shape),
        ],
        out_specs=[
            pl.BlockSpec((1, MLA_HEADS, MLA_QK_PAD, tm), lambda i: (i // nt, 0, 0, i % nt)),
            pl.BlockSpec((tm, nq), lambda i: (i, 0)),
            pl.BlockSpec((1, MLA_HEADS, 1, MLA_V_PAD, tm), lambda i: (i // nt, 0, i % nt, 0, 0)),
        ],
        out_shape=[
            jax.ShapeDtypeStruct((t // seq, MLA_HEADS, MLA_QK_PAD, seq), BF16),
            jax.ShapeDtypeStruct((t, nq), BF16),
            jax.ShapeDtypeStruct((t // seq, MLA_HEADS, nt, MLA_V_PAD, tm), BF16),
        ],
        compiler_params=_cparams(("parallel",)),
        name="mla_proj",
    )(x, mod, gpre, trig, wd, qn, kvn, wq, wkv)


def _flash_kernel(qt_ref, k_ref, vt_ref, o_ref, s_ref, p_ref, acc_ref):
    qt = qt_ref[0, 0]
    tq = qt.shape[1]
    nk, _, tk = vt_ref.shape[2:]
    assert nk == 1 or nk % 2 == 0

    def scores(j, slot):
        r0 = pl.multiple_of(j * tk, tk)
        s = _dot(k_ref[0, pl.ds(r0, tk), :], qt)
        s_ref[slot] = s
        return jnp.max(s.reshape(tk // 8, 8, tq), axis=0)

    def softmax(slot, cmax, m):
        m_new = jnp.maximum(m, jnp.max(cmax, axis=0, keepdims=True))
        alpha = jnp.exp2(m - m_new)
        p_ref[slot] = jnp.exp2((s_ref[slot] - m_new).astype(BF16))
        return m_new, alpha

    def weighted(j, slot, alpha):
        acc_ref[...] = alpha * acc_ref[...] + _dot(vt_ref[0, 0, j], p_ref[slot])

    def step(j, slot, cmax, m, alpha):
        weighted(j - 1, 1 - slot, alpha)
        cmax_next = scores(j + 1, 1 - slot)
        return (cmax_next,) + softmax(slot, cmax, m)

    acc_ref[...] = jnp.zeros_like(acc_ref)
    m = jnp.full((1, tq), -1e30, F32)
    cmax = scores(0, 0)
    if nk > 1:
        cmax_next = scores(1, 1)
    m, alpha = softmax(0, cmax, m)
    if nk > 1:
        unroll = 4 if nk >= 8 else 2
        trips = (nk - 2) // unroll

        def body(t, carry):
            j = 1 + unroll * t
            for u in range(unroll):
                carry = step(j + u, (1 + u) % 2, *carry)
            return carry

        carry = lax.fori_loop(0, trips, body, (cmax_next, m, alpha))
        for j in range(1 + unroll * trips, nk - 1):
            carry = step(j, j % 2, *carry)
        cmax, m, alpha = carry
        weighted(nk - 2, 0, alpha)
        m, alpha = softmax(1, cmax, m)
    weighted(nk - 1, (nk - 1) % 2, alpha)
    acc = acc_ref[...]
    o_t = acc[:MLA_V] / acc[MLA_V:MLA_V + 1]
    o_ref[0] = o_t.T.astype(o_ref.dtype)


def _flash(qt, k, vt, *, tq):
    b, s, _ = k.shape
    _, _, nk, dv, tk = vt.shape
    return pl.pallas_call(
        _flash_kernel,
        grid=(b, MLA_HEADS, s // tq),
        in_specs=[
            pl.BlockSpec((1, 1, MLA_QK_PAD, tq), lambda bi, h, i: (bi, h, 0, i)),
            pl.BlockSpec((1, s, MLA_QK_PAD), lambda bi, h, i: (bi, 0, h)),
            pl.BlockSpec((1, 1, nk, dv, tk), lambda bi, h, i: (bi, h, 0, 0, 0)),
        ],
        out_specs=pl.BlockSpec((1, tq, MLA_V), lambda bi, h, i: (bi, i, h)),
        out_shape=jax.ShapeDtypeStruct((b, s, MLA_HEADS * MLA_V), BF16),
        scratch_shapes=[
            pltpu.VMEM((2, tk, tq), F32),
            pltpu.VMEM((2, tk, tq), BF16),
            pltpu.VMEM((dv, tq), F32),
        ],
        compiler_params=_cparams(("parallel", "parallel", "arbitrary")),
        name="mla_flash",
    )(qt, k, vt)


def _out_proj_kernel(a_ref, x_ref, mod_ref, gpost_ref, w_ref, o_ref):
    out = _dot(a_ref[...], w_ref[...])
    o_ref[...] = _residual(x_ref[...], out, mod_ref, gpost_ref, 1, 1.0)


def _out_proj(a, x, mod, gpost, w, *, seq, tm):
    t, d = x.shape
    return pl.pallas_call(
        _out_proj_kernel,
        grid=(t // tm,),
        in_specs=[
            pl.BlockSpec((tm, a.shape[1]), lambda i: (i, 0)),
            pl.BlockSpec((tm, d), lambda i: (i, 0)),
            pl.BlockSpec((1, 3 * N_SUB, d), lambda i: ((i * tm) // seq, 0, 0)),
            _const_spec((1, d)),
            _const_spec(w.shape),
        ],
        out_specs=pl.BlockSpec((tm, d), lambda i: (i, 0)),
        out_shape=jax.ShapeDtypeStruct((t, d), F32),
        compiler_params=_cparams(("parallel",)),
        name="out_proj",
    )(a, x, mod, gpost, w)


def _gdn_proj_kernel(xp_ref, x_ref, xn_ref, mod_ref, gpre_ref, wqkv_ref, wg_ref, wab_ref, conv_ref,
                     aneg_ref, dtb_ref, q_ref, k_ref, v_ref, gate_ref, gb_ref, p_ref,
                     *, tiles_per_seq):
    i = pl.program_id(0)
    tm = x_ref.shape[0]
    first = (i % tiles_per_seq) == 0
    last = (i % tiles_per_seq) == tiles_per_seq - 1

    def proj_rows(xr):
        return _dot(_modulated(xr, mod_ref, gpre_ref, 1).astype(BF16), wqkv_ref[...])

    hb = _modulated(x_ref[...], mod_ref, gpre_ref, 1).astype(BF16)
    p_prev = jnp.where(first, 0.0, proj_rows(xp_ref[...]))
    p_main = _dot(hb, wqkv_ref[...])
    p_next = jnp.where(last, 0.0, proj_rows(xn_ref[...]))
    n_slab = p_ref.shape[0]
    for c in range(n_slab):
        cs = slice(c * LANES, (c + 1) * LANES)
        p_ref[c, 0:GDN_HALO, :] = p_prev[:, cs]
        p_ref[c, GDN_HALO:GDN_HALO + tm, :] = p_main[:, cs]
        p_ref[c, GDN_HALO + tm:, :] = p_next[:, cs]

    def l2n(z):
        return z * lax.rsqrt(jnp.sum(z * z, axis=-1, keepdims=True) + NORM_EPS)

    pad = GDN_CONV // 2
    for c in range(n_slab):
        cs = slice(c * LANES, (c + 1) * LANES)
        acc = None
        for tap in range(GDN_CONV):
            r0 = GDN_HALO - pad + tap
            term = p_ref[c, r0:r0 + tm, :] * conv_ref[tap:tap + 1, cs]
            acc = term if acc is None else acc + term
        z = acc * jax.nn.sigmoid(acc)
        if c < GDN_HEADS:
            q_ref[:, cs] = l2n(z) * (GDN_DK ** -0.5)
        elif c < 2 * GDN_HEADS:
            k_ref[:, c * LANES - GDN_QK:(c + 1) * LANES - GDN_QK] = l2n(z)
        else:
            v_ref[:, c * LANES - 2 * GDN_QK:(c + 1) * LANES - 2 * GDN_QK] = z
    gate_ref[...] = _dot(hb, wg_ref[...])
    ab = _dot(hb, wab_ref[...])
    z = ab + dtb_ref[...]
    softplus = jnp.maximum(z, 0.0) + jnp.log(1.0 + jnp.exp(-jnp.abs(z)))
    lane = lax.broadcasted_iota(jnp.int32, ab.shape, 1)
    gb_ref[...] = jnp.where(lane < 2 * GDN_HEADS, aneg_ref[...] * softplus, jax.nn.sigmoid(ab))


def _gdn_proj(x, mod, gpre, wqkv, wg, wab, conv, aneg, dtb, *, seq, tm):
    t, d = x.shape
    hb = tm // GDN_HALO
    nblk8 = t // GDN_HALO
    n_conv = wqkv.shape[1]
    return pl.pallas_call(
        functools.partial(_gdn_proj_kernel, tiles_per_seq=seq // tm),
        grid=(t // tm,),
        in_specs=[
            pl.BlockSpec((GDN_HALO, d), lambda i: (jnp.maximum(i * hb - 1, 0), 0)),
            pl.BlockSpec((tm, d), lambda i: (i, 0)),
            pl.BlockSpec((GDN_HALO, d), lambda i: (jnp.minimum((i + 1) * hb, nblk8 - 1), 0)),
            pl.BlockSpec((1, 3 * N_SUB, d), lambda i: ((i * tm) // seq, 0, 0)),
            _const_spec((1, d)),
            _const_spec(wqkv.shape),
            _const_spec(wg.shape),
            _const_spec(wab.shape),
            _const_spec(conv.shape),
            _const_spec(aneg.shape),
            _const_spec(dtb.shape),
        ],
        out_specs=[
            pl.BlockSpec((tm, GDN_QK), lambda i: (i, 0)),
            pl.BlockSpec((tm, GDN_QK), lambda i: (i, 0)),
            pl.BlockSpec((tm, GDN_VW), lambda i: (i, 0)),
            pl.BlockSpec((tm, GDN_VW), lambda i: (i, 0)),
            pl.BlockSpec((tm, LANES), lambda i: (i, 0)),
        ],
        out_shape=[
            jax.ShapeDtypeStruct((t, GDN_QK), F32),
            jax.ShapeDtypeStruct((t, GDN_QK), F32),
            jax.ShapeDtypeStruct((t, GDN_VW), F32),
            jax.ShapeDtypeStruct((t, GDN_VW), F32),
            jax.ShapeDtypeStruct((t, LANES), F32),
        ],
        scratch_shapes=[pltpu.VMEM((n_conv // LANES, tm + 2 * GDN_HALO, LANES), F32)],
        compiler_params=_cparams(("parallel",)),
        name="gdn_proj",
    )(x, x, x, mod, gpre, wqkv, wg, wab, conv, aneg, dtb)


def _gdn_cumsum_kernel(gb_ref, o_ref):
    g = gb_ref[...]
    tm = g.shape[0]
    row = lax.broadcasted_iota(jnp.int32, (tm, tm), 0)
    col = lax.broadcasted_iota(jnp.int32, (tm, tm), 1)
    sh = int(math.log2(GDN_CHUNK))
    same = lax.shift_right_logical(row, sh) == lax.shift_right_logical(col, sh)
    lower = jnp.where(same & (col <= row), 1.0, 0.0).astype(BF16)
    upper = jnp.where(same & (col >= row), 1.0, 0.0).astype(BF16)
    g1 = g.astype(BF16)
    r1 = g - g1.astype(F32)
    g2 = r1.astype(BF16)
    g3 = (r1 - g2.astype(F32)).astype(BF16)
    pre = _dot(lower, g1) + _dot(lower, g2) + _dot(lower, g3)
    suf = _dot(upper, g1) + _dot(upper, g2) + _dot(upper, g3)
    lane = lax.broadcasted_iota(jnp.int32, g.shape, 1)
    o_ref[...] = jnp.where(lane < GDN_HEADS, pre, jnp.where(lane < 2 * GDN_HEADS, suf, g))


def _gdn_cumsum(gb, *, tm):
    t = gb.shape[0]
    return pl.pallas_call(
        _gdn_cumsum_kernel,
        grid=(t // tm,),
        in_specs=[pl.BlockSpec((tm, LANES), lambda i: (i, 0))],
        out_specs=pl.BlockSpec((tm, LANES), lambda i: (i, 0)),
        out_shape=jax.ShapeDtypeStruct((t, LANES), F32),
        compiler_params=_cparams(("parallel",)),
        name="gdn_cumsum",
    )(gb)


def _gdn_delta_kernel(q_ref, k_ref, v_ref, gcol_ref, bcol_ref, grow_ref, o_ref, state_ref,
                      *, reverse):
    c = GDN_CHUNK
    nchunk = q_ref.shape[0] // c

    @pl.when(pl.program_id(1) == 0)
    def _():
        state_ref[...] = jnp.zeros_like(state_ref)

    row = lax.broadcasted_iota(jnp.int32, (c, c), 0)
    col = lax.broadcasted_iota(jnp.int32, (c, c), 1)
    if reverse:
        row, col = col, row
    tri = row >= col
    strict = row > col
    eye = jnp.where(row == col, 1.0, 0.0)
    level_masks = []
    for lv in range(int(math.log2(c))):
        rb = lax.shift_right_logical(row, lv)
        cb = lax.shift_right_logical(col, lv)
        level_masks.append((lax.shift_right_logical(rb, 1) == lax.shift_right_logical(cb, 1))
                           & ((rb & 1) == 1) & ((cb & 1) == 0))
    last_row = 0 if reverse else c - 1

    heads = range(GDN_HEADS)
    order = list(reversed(range(nchunk))) if reverse else list(range(nchunk))
    probs = [(j, h) for j in order for h in heads]
    rows = lambda j: slice(j * c, (j + 1) * c)
    cols = lambda h: slice(h * GDN_DK, (h + 1) * GDN_DK)

    qs = [q_ref[rows(j), cols(h)] for j, h in probs]
    ks = [k_ref[rows(j), cols(h)] for j, h in probs]
    vs = [v_ref[rows(j), cols(h)] for j, h in probs]
    gcbs = [jnp.broadcast_to(gcol_ref[0, rows(j), h:h + 1], (c, GDN_DK)) for j, h in probs]
    betas = [jnp.broadcast_to(bcol_ref[0, rows(j), h:h + 1], (c, GDN_DK)) for j, h in probs]
    n = len(probs)
    decays = []
    for p, (j, h) in enumerate(probs):
        grb = jnp.broadcast_to(grow_ref[0, j, h:h + 1, :], (c, c))
        decays.append(jnp.where(tri, jnp.exp(jnp.where(tri, gcbs[p][:, :c] - grb, 0.0)), 0.0))
    kbs = [ks[p] * betas[p] for p in range(n)]
    a2s = [_dot_nt(jnp.concatenate([kbs[p], qs[p]], axis=0).astype(BF16), ks[p].astype(BF16))
           for p in range(n)]
    ms = [jnp.where(strict, a2s[p][:c] * decays[p], 0.0) for p in range(n)]
    qks = [(a2s[p][c:] * decays[p]).astype(BF16) for p in range(n)]
    tinvs = [eye - jnp.where(level_masks[0], ms[p], 0.0) for p in range(n)]
    for lm in level_masks[1:]:
        tbs = [t.astype(BF16) for t in tinvs]
        tcs = [_dot(tbs[p], jnp.where(lm, ms[p], 0.0).astype(BF16)).astype(BF16) for p in range(n)]
        tinvs = [tinvs[p] - _dot(tcs[p], tbs[p]) for p in range(n)]
    egs = [jnp.exp(g) for g in gcbs]
    sols = [_dot(tinvs[p].astype(BF16),
                 jnp.concatenate([vs[p] * betas[p], kbs[p] * egs[p]], axis=1).astype(BF16))
            for p in range(n)]
    g_lasts = [g[last_row:last_row + 1, :] for g in gcbs]
    k_decs = [(ks[p] * jnp.exp(g_lasts[p] - gcbs[p])).astype(BF16) for p in range(n)]
    wq_lhs = [jnp.concatenate([sols[p][:, GDN_DV:], qs[p] * egs[p]], axis=0).astype(BF16)
              for p in range(n)]

    sts = [state_ref[h] for h in heads]
    for step, j in enumerate(order):
        ps = [step * GDN_HEADS + h for h in heads]
        wqs = [_dot(wq_lhs[ps[h]], sts[h].astype(BF16)) for h in heads]
        vnbs = [(sols[ps[h]][:, :GDN_DV] - wqs[h][:c]).astype(BF16) for h in heads]
        for h in heads:
            o_ref[rows(j), cols(h)] = wqs[h][c:] + _dot(qks[ps[h]], vnbs[h])
        sts = [sts[h] * jnp.exp(g_lasts[ps[h]]) + _dot_tn(k_decs[ps[h]], vnbs[h]) for h in heads]
    for h in heads:
        state_ref[h] = sts[h]


def _gdn_delta(q, k, v, gcol, bcol, grow, *, batch, seq, tb, reverse):
    t = q.shape[0]
    nb = seq // tb
    cpb = tb // GDN_CHUNK

    def blk(bi, i):
        return bi * nb + ((nb - 1 - i) if reverse else i)

    def seq_blk(bi, i):
        return (nb - 1 - i) if reverse else i

    return pl.pallas_call(
        functools.partial(_gdn_delta_kernel, reverse=reverse),
        grid=(batch, nb),
        in_specs=[
            pl.BlockSpec((tb, GDN_QK), lambda bi, i: (blk(bi, i), 0)),
            pl.BlockSpec((tb, GDN_QK), lambda bi, i: (blk(bi, i), 0)),
            pl.BlockSpec((tb, GDN_VW), lambda bi, i: (blk(bi, i), 0)),
            pl.BlockSpec((1, tb, GDN_HEADS), lambda bi, i: (bi, seq_blk(bi, i), 0)),
            pl.BlockSpec((1, tb, GDN_HEADS), lambda bi, i: (bi, seq_blk(bi, i), 0)),
            pl.BlockSpec((1, cpb, GDN_HEADS, GDN_CHUNK), lambda bi, i: (bi, seq_blk(bi, i), 0, 0)),
        ],
        out_specs=pl.BlockSpec((tb, GDN_VW), lambda bi, i: (blk(bi, i), 0)),
        out_shape=jax.ShapeDtypeStruct((t, GDN_VW), F32),
        scratch_shapes=[pltpu.VMEM((GDN_HEADS, GDN_DK, GDN_DV), F32)],
        compiler_params=_cparams(("parallel", "arbitrary")),
        name="gdn_delta_bw" if reverse else "gdn_delta_fw",
    )(q, k, v, gcol, bcol, grow)


def _gdn_out_kernel(of_ref, ob_ref, gate_ref, x_ref, mod_ref, gpost_ref, onorm_ref, w_ref, o_ref,
                    a_ref):
    o = of_ref[...] + ob_ref[...]
    gate = gate_ref[...]
    og = gate * jax.nn.sigmoid(gate)
    for h in range(GDN_HEADS):
        cs = slice(h * GDN_DV, (h + 1) * GDN_DV)
        a_ref[:, cs] = (_rms(o[:, cs], onorm_ref[...]) * og[:, cs]).astype(BF16)
    out = _dot(a_ref[...], w_ref[...])
    o_ref[...] = _residual(x_ref[...], out, mod_ref, gpost_ref, 1, 1.0)


def _gdn_out(o_fw, o_bw, gate, x, mod, gpost, onorm, w, *, seq, tm):
    t, d = x.shape
    return pl.pallas_call(
        _gdn_out_kernel,
        grid=(t // tm,),
        in_specs=[
            pl.BlockSpec((tm, GDN_VW), lambda i: (i, 0)),
            pl.BlockSpec((tm, GDN_VW), lambda i: (i, 0)),
            pl.BlockSpec((tm, GDN_VW), lambda i: (i, 0)),
            pl.BlockSpec((tm, d), lambda i: (i, 0)),
            pl.BlockSpec((1, 3 * N_SUB, d), lambda i: ((i * tm) // seq, 0, 0)),
            _const_spec((1, d)),
            _const_spec(onorm.shape),
            _const_spec(w.shape),
        ],
        out_specs=pl.BlockSpec((tm, d), lambda i: (i, 0)),
        out_shape=jax.ShapeDtypeStruct((t, d), F32),
        scratch_shapes=[pltpu.VMEM((tm, GDN_VW), BF16)],
        compiler_params=_cparams(("parallel",)),
        name="gdn_out",
    )(o_fw, o_bw, gate, x, mod, gpost, onorm, w)


def _fnet_chan_kernel(x_ref, mod_ref, gpre_ref, cs_ref, o_ref):
    hb = _modulated(x_ref[...], mod_ref, gpre_ref, 1).astype(BF16)
    cg = hb.shape[1] // FNET_GROUPS
    for g in range(FNET_GROUPS):
        ab = _dot(hb[:, g * cg:(g + 1) * cg], cs_ref[...])
        o_ref[0, 0, :, g * cg:(g + 1) * cg] = ab[:, :cg].astype(BF16)
        o_ref[0, 1, :, g * cg:(g + 1) * cg] = ab[:, cg:].astype(BF16)


def _fnet_chan(x, mod, gpre, cs, *, batch, seq, tm):
    t, d = x.shape
    nt = seq // tm
    return pl.pallas_call(
        _fnet_chan_kernel,
        grid=(t // tm,),
        in_specs=[
            pl.BlockSpec((tm, d), lambda i: (i, 0)),
            pl.BlockSpec((1, 3 * N_SUB, d), lambda i: ((i * tm) // seq, 0, 0)),
            _const_spec((1, d)),
            _const_spec(cs.shape),
        ],
        out_specs=pl.BlockSpec((1, 2, tm, d), lambda i: (i // nt, 0, i % nt, 0)),
        out_shape=jax.ShapeDtypeStruct((batch, 2, seq, d), BF16),
        compiler_params=_cparams(("parallel",)),
        name="fnet_chan",
    )(x, mod, gpre, cs)


def _fnet_s1_kernel(m1_ref, ab_ref, y_ref):
    s1 = ab_ref.shape[2]
    ab = jnp.concatenate([ab_ref[0, 0], ab_ref[0, 1]], axis=0)
    y = _dot(m1_ref[...], ab)
    y_ref[0, 0] = y[:s1]
    y_ref[0, 1] = y[s1:]


def _fnet_s1(m1, ab, *, tn):
    batch, _, s1, n = ab.shape
    return pl.pallas_call(
        _fnet_s1_kernel,
        grid=(batch, n // tn),
        in_specs=[
            _const_spec(m1.shape),
            pl.BlockSpec((1, 2, s1, tn), lambda bi, j: (bi, 0, 0, j)),
        ],
        out_specs=pl.BlockSpec((1, 2, s1, tn), lambda bi, j: (bi, 0, 0, j)),
        out_shape=jax.ShapeDtypeStruct((batch, 2, s1, n), F32),
        compiler_params=_cparams(("parallel", "parallel")),
        name="fnet_s1",
    )(m1, ab)


def _fnet_s2_kernel(m2_ref, tw_ref, y_ref, x_ref, mod_ref, gpost_ref, w_ref, b_ref, o_ref):
    d = x_ref.shape[2]
    reps = d // LANES
    cos = jnp.concatenate([tw_ref[0, 0]] * reps, axis=1)
    sin = jnp.concatenate([tw_ref[0, 1]] * reps, axis=1)
    yr = y_ref[0, 0]
    yi = y_ref[0, 1]
    z = jnp.concatenate([yr * cos + yi * sin, yi * cos - yr * sin], axis=0).astype(BF16)
    f = _dot(m2_ref[...], z)
    out = _dot(f.astype(BF16), w_ref[...]) + b_ref[...]
    o_ref[0] = _residual(x_ref[0], out, mod_ref, gpost_ref, 1, 1.0)


def _fnet_s2(m2, tw, y, x, mod, gpost, w, b, *, s1, s2):
    batch = y.shape[0]
    d = w.shape[0]
    return pl.pallas_call(
        _fnet_s2_kernel,
        grid=(batch, s1),
        in_specs=[
            _const_spec(m2.shape),
            pl.BlockSpec((1, 2, s2, LANES), lambda bi, t1: (t1, 0, 0, 0)),
            pl.BlockSpec((1, 2, s2, d), lambda bi, t1: (bi, 0, t1, 0)),
            pl.BlockSpec((1, s2, d), lambda bi, t1: (bi, 0, t1)),
            pl.BlockSpec((1, 3 * N_SUB, d), lambda bi, t1: (bi, 0, 0)),
            _const_spec((1, d)),
            _const_spec(w.shape),
            _const_spec((1, d)),
        ],
        out_specs=pl.BlockSpec((1, s2, d), lambda bi, t1: (bi, 0, t1)),
        out_shape=jax.ShapeDtypeStruct((batch, s2, s1 * d), F32),
        compiler_params=_cparams(("parallel", "parallel")),
        name="fnet_s2",
    )(m2, tw, y, x, mod, gpost, w, b)


def _rope_table(seq):
    half = MLA_ROPE // 2
    pos = jnp.arange(seq, dtype=F32)
    inv_freq = ROPE_THETA ** (-jnp.arange(half, dtype=F32) / half)
    ang = pos[:, None] * inv_freq[None, :]
    cos, sin = jnp.cos(ang), jnp.sin(ang)
    return jnp.concatenate([cos, cos, -sin, sin], axis=1)


def _swap_halves(w):
    half = w.shape[-1] // 2
    return jnp.concatenate([w[..., half:], w[..., :half]], axis=-1)


def _prep_mla(w_down, w_uq, w_ukv):
    d = w_down.shape[0]
    rope = w_down[:, MLA_Q_LORA + MLA_KV_LORA:]
    wd = jnp.concatenate([w_down, _swap_halves(rope)], axis=1).astype(BF16)
    uq = w_uq.reshape(MLA_Q_LORA, MLA_HEADS, MLA_NOPE + MLA_ROPE)
    uq_rope = uq[..., MLA_NOPE:]
    wq = jnp.concatenate([uq, _swap_halves(uq_rope)], axis=-1)
    wq = wq.reshape(MLA_Q_LORA, MLA_HEADS * MLA_QK_PAD).astype(BF16)
    ukv = w_ukv.reshape(MLA_KV_LORA, MLA_HEADS, MLA_NOPE + MLA_V)
    wkv = jnp.concatenate([ukv[..., :MLA_NOPE].reshape(MLA_KV_LORA, -1),
                           ukv[..., MLA_NOPE:].reshape(MLA_KV_LORA, -1)], axis=1).astype(BF16)
    del d
    return wd, wq, wkv


def _fnet_tables(seq, cg):
    def trig(n):
        idx = jnp.arange(n, dtype=jnp.int32)
        ang = ((idx[:, None] * idx[None, :]) % n).astype(F32) * (2.0 * math.pi / n)
        return jnp.cos(ang), jnp.sin(ang)

    cc, sc = trig(cg)
    cs = (jnp.concatenate([cc, sc], axis=1) * (cg ** -0.5)).astype(BF16)
    s2 = min(seq, LANES)
    s1 = seq // s2
    c1, sn1 = trig(s1)
    m1 = (jnp.block([[c1, -sn1], [-sn1, -c1]]) * (s1 ** -0.5)).astype(BF16)
    c2, sn2 = trig(s2)
    m2 = (jnp.concatenate([c2, sn2], axis=1) * (s2 ** -0.5)).astype(BF16)
    t1 = jnp.arange(s1, dtype=jnp.int32)[:, None]
    k2 = jnp.arange(s2, dtype=jnp.int32)[None, :]
    ang = (t1 * k2).astype(F32) * (2.0 * math.pi / seq)
    tw = jnp.stack([jnp.cos(ang), jnp.sin(ang)], axis=1)
    tw = jnp.broadcast_to(tw[..., None], (s1, 2, s2, LANES))
    return cs, m1, m2, tw


def _pick(n, pref):
    return pref if n % pref == 0 else n


def _trunk(x3, mods, prm):
    batch, seq, d = x3.shape
    t = batch * seq
    x = x3.reshape(t, d)
    tm = _pick(seq, 512)
    row = lambda v: v.reshape(1, -1)
    trig = _rope_table(seq)
    for l in range(mods.shape[0]):
        i = l // 3
        kind = l % 3
        mod = mods[l]
        gpre = prm['norm_pre'][l]
        gpost = prm['norm_post'][l]
        x = _ffn(x, mod, row(gpre[0]), row(gpost[0]), prm['ffn_w_in'][l][0], prm['ffn_w_out'][l][0],
                 sub=0, seq=seq, tm=tm)
        if kind == 0:
            wd, wq, wkv = prm['mla'][i]
            qt, k, vt = _mla_proj(x, mod, row(gpre[1]), trig, wd, row(prm['mla_q_norm'][i]),
                                  row(prm['mla_kv_norm'][i]), wq, wkv, seq=seq, tm=tm)
            o = _flash(qt, k.reshape(batch, seq, -1), vt, tq=_pick(seq, 512))
            x = _out_proj(o.reshape(t, -1), x, mod, row(gpost[1]), prm['mla_w_out'][i], seq=seq, tm=tm)
        elif kind == 1:
            g = prm['gdn'][i]
            q, k, v, gate, gb = _gdn_proj(x, mod, row(gpre[1]), g['wqkv'], g['wg'], g['wab'], g['conv'],
                                          g['aneg'], g['dtb'], seq=seq, tm=_pick(seq, 256))
            gc = _gdn_cumsum(gb, tm=tm)
            nh = GDN_HEADS
            gc3 = gc.reshape(batch, seq, LANES)
            outs = []
            for dr in range(2):
                gcol = gc3[:, :, dr * nh:(dr + 1) * nh]
                bcol = gc3[:, :, 2 * nh + dr * nh:2 * nh + (dr + 1) * nh]
                grow = jnp.swapaxes(gcol.reshape(batch, seq // GDN_CHUNK, GDN_CHUNK, nh), 2, 3)
                outs.append(_gdn_delta(q, k, v, gcol, bcol, grow, batch=batch, seq=seq,
                                       tb=_pick(seq, 256), reverse=bool(dr)))
            x = _gdn_out(outs[0], outs[1], gate, x, mod, row(gpost[1]), row(prm['gdn_o_norm'][i]),
                         g['wout'], seq=seq, tm=tm)
        else:
            cs, m1, m2, tw = _fnet_tables(seq, d // FNET_GROUPS)
            s1, _, s2, _ = tw.shape
            ab = _fnet_chan(x, mod, row(gpre[1]), cs, batch=batch, seq=seq, tm=tm)
            y = _fnet_s1(m1, ab.reshape(batch, 2, s1, s2 * d), tn=2 * d)
            x = _fnet_s2(m2, tw, y.reshape(batch, 2, seq, d), x.reshape(batch, s2, s1 * d), mod,
                         row(gpost[1]), prm['fnet_w_out'][i], row(prm['fnet_b_out'][i]),
                         s1=s1, s2=s2).reshape(t, d)
        x = _ffn(x, mod, row(gpre[2]), row(gpost[2]), prm['ffn_w_in'][l][1], prm['ffn_w_out'][l][1],
                 sub=2, seq=seq, tm=tm)
    return x.reshape(batch, seq, d)


def _prep_gdn(w_in, conv, a_log, dt_bias, w_out):
    n_conv = 2 * GDN_QK + GDN_VW
    wab = jnp.pad(w_in[:, n_conv + GDN_VW:], ((0, 0), (0, LANES - 4 * GDN_HEADS)))
    pad16 = lambda v: jnp.pad(v.reshape(1, -1).astype(F32), ((0, 0), (0, LANES - 2 * GDN_HEADS)))
    return {
        'wqkv': w_in[:, :n_conv].astype(BF16),
        'wg': w_in[:, n_conv:n_conv + GDN_VW].astype(BF16),
        'wab': wab.astype(BF16),
        'conv': conv.astype(F32),
        'aneg': pad16(-jnp.exp(a_log.astype(F32))),
        'dtb': pad16(dt_bias),
        'wout': w_out.astype(BF16),
    }


def kernel(x_prompt, x_sample, c_prompt, c_sample, w_ada, b_ada, norm_pre, norm_post, ffn_w_in, ffn_w_out, mla_w_down, mla_q_norm, mla_kv_norm, mla_w_uq, mla_w_ukv, mla_w_out, gdn_w_in, gdn_conv, gdn_a_log, gdn_dt_bias, gdn_o_norm, gdn_w_out, fnet_w_out, fnet_b_out):
    d = x_prompt.shape[-1]
    prm = {
        'norm_pre': norm_pre, 'norm_post': norm_post,
        'ffn_w_in': ffn_w_in.astype(BF16), 'ffn_w_out': ffn_w_out.astype(BF16),
        'mla': [_prep_mla(mla_w_down[i], mla_w_uq[i], mla_w_ukv[i]) for i in range(mla_w_down.shape[0])],
        'mla_q_norm': mla_q_norm, 'mla_kv_norm': mla_kv_norm, 'mla_w_out': mla_w_out.astype(BF16),
        'gdn': [_prep_gdn(gdn_w_in[i], gdn_conv[i], gdn_a_log[i], gdn_dt_bias[i], gdn_w_out[i])
                for i in range(gdn_w_in.shape[0])],
        'gdn_o_norm': gdn_o_norm,
        'fnet_w_out': fnet_w_out.astype(BF16), 'fnet_b_out': fnet_b_out,
    }
    nbp = c_prompt.shape[0]
    c_all = jnp.concatenate([c_prompt, c_sample], axis=0)
    mods = _ada_mod(c_all, w_ada, b_ada).reshape(w_ada.shape[0], c_all.shape[0], 3 * N_SUB, d)
    y_prompt = _trunk(x_prompt, mods[:, :nbp], prm)
    y_sample = _trunk(x_sample, mods[:, nbp:], prm)
    return (y_prompt, y_sample)
```

```python
import functools
import math

import jax
import jax.numpy as jnp
from jax import lax
from jax.experimental import pallas as pl
from jax.experimental.pallas import tpu as pltpu

F32 = jnp.float32
BF16 = jnp.bfloat16

N_SUB = 3
D_FF = 2816
NORM_EPS = 1e-6

MLA_HEADS = 8
MLA_Q_LORA = 384
MLA_KV_LORA = 256
MLA_NOPE = 128
MLA_ROPE = 64
MLA_V = 128
ROPE_THETA = 10000.0
MLA_SCALE = (MLA_NOPE + MLA_ROPE) ** -0.5
MLA_QK_PAD = 256
MLA_V_PAD = MLA_V + 16

GDN_HEADS = 8
GDN_DK = 128
GDN_DV = 128
GDN_QK = GDN_HEADS * GDN_DK
GDN_VW = GDN_HEADS * GDN_DV
GDN_CONV = 5
GDN_CHUNK = 64
GDN_HALO = 8

FNET_GROUPS = 8

LANES = 128
VMEM_LIMIT = 56 * 1024 * 1024


def _cparams(sem):
    return pltpu.CompilerParams(dimension_semantics=sem, vmem_limit_bytes=VMEM_LIMIT)


def _const_spec(shape):
    nd = len(shape)
    return pl.BlockSpec(shape, lambda *_: (0,) * nd, pipeline_mode=pl.Buffered(1))


def _rms(x, g):
    ms = jnp.mean(x * x, axis=-1, keepdims=True)
    return x * lax.rsqrt(ms + NORM_EPS) * g


def _modulated(x, mod_ref, gpre_ref, sub):
    shift = mod_ref[0, 3 * sub:3 * sub + 1, :]
    scale = mod_ref[0, 3 * sub + 1:3 * sub + 2, :]
    return _rms(x, gpre_ref[...]) * (1.0 + scale) + shift


def _residual(x, out, mod_ref, gpost_ref, sub, weight):
    gate = mod_ref[0, 3 * sub + 2:3 * sub + 3, :]
    return x + (weight * gate) * _rms(out, gpost_ref[...])


def _dot(a, b):
    return jnp.dot(a, b, preferred_element_type=F32)


def _dot_nt(a, b):
    return lax.dot_general(a, b, (((1,), (1,)), ((), ())), preferred_element_type=F32)


def _dot_tn(a, b):
    return lax.dot_general(a, b, (((0,), (0,)), ((), ())), preferred_element_type=F32)


def _ada_kernel(c_ref, w_ref, b_ref, o_ref):
    c = c_ref[...]
    sc = (c * jax.nn.sigmoid(c)).astype(BF16)
    o_ref[0] = _dot(sc, w_ref[0].astype(BF16)) + b_ref[0]


def _ada_mod(c_all, w_ada, b_ada):
    nb, d = c_all.shape
    depth, _, n_out = w_ada.shape
    tn = n_out // 8
    return pl.pallas_call(
        _ada_kernel,
        grid=(depth, n_out // tn),
        in_specs=[
            pl.BlockSpec((nb, d), lambda l, j: (0, 0)),
            pl.BlockSpec((1, d, tn), lambda l, j: (l, 0, j)),
            pl.BlockSpec((1, 1, tn), lambda l, j: (l, 0, j)),
        ],
        out_specs=pl.BlockSpec((1, nb, tn), lambda l, j: (l, 0, j)),
        out_shape=jax.ShapeDtypeStruct((depth, nb, n_out), F32),
        compiler_params=_cparams(("arbitrary", "arbitrary")),
        name="ada_mod",
    )(c_all, w_ada, b_ada.reshape(depth, 1, n_out))


def _ffn_kernel(x_ref, mod_ref, gpre_ref, gpost_ref, win_ref, wout_ref, o_ref, act_ref,
                *, sub, ck):
    x = x_ref[...]
    hb = _modulated(x, mod_ref, gpre_ref, sub).astype(BF16)
    for c in range(D_FF // ck):
        g = _dot(hb, win_ref[:, c * ck:(c + 1) * ck])
        u = _dot(hb, win_ref[:, D_FF + c * ck:D_FF + (c + 1) * ck])
        act_ref[:, c * ck:(c + 1) * ck] = (g * jax.nn.sigmoid(g) * u).astype(BF16)
    out = _dot(act_ref[...], wout_ref[...])
    o_ref[...] = _residual(x, out, mod_ref, gpost_ref, sub, 0.5)


def _ffn(x, mod, gpre, gpost, w_in, w_out, *, sub, seq, tm):
    t, d = x.shape
    return pl.pallas_call(
        functools.partial(_ffn_kernel, sub=sub, ck=256),
        grid=(t // tm,),
        in_specs=[
            pl.BlockSpec((tm, d), lambda i: (i, 0)),
            pl.BlockSpec((1, 3 * N_SUB, d), lambda i: ((i * tm) // seq, 0, 0)),
            _const_spec((1, d)),
            _const_spec((1, d)),
            _const_spec(w_in.shape),
            _const_spec(w_out.shape),
        ],
        out_specs=pl.BlockSpec((tm, d), lambda i: (i, 0)),
        out_shape=jax.ShapeDtypeStruct((t, d), F32),
        scratch_shapes=[pltpu.VMEM((tm, D_FF), BF16)],
        compiler_params=_cparams(("parallel",)),
        name="ffn",
    )(x, mod, gpre, gpost, w_in, w_out)


def _rope_pair(t):
    return t + pltpu.roll(t, 64, axis=1)


def _mla_proj_kernel(x_ref, mod_ref, gpre_ref, trig_ref, wd_ref, qn_ref, kvn_ref, wq_ref, wkv_ref,
                     qt_ref, k_ref, vt_ref):
    x = x_ref[...]
    hb = _modulated(x, mod_ref, gpre_ref, 1).astype(BF16)
    down = _dot(hb, wd_ref[...])
    cq = _rms(down[:, :MLA_Q_LORA], qn_ref[...]).astype(BF16)
    ckv = _rms(down[:, MLA_Q_LORA:MLA_Q_LORA + MLA_KV_LORA], kvn_ref[...]).astype(BF16)
    trig = trig_ref[...]
    lane = lax.broadcasted_iota(jnp.int32, trig.shape, 1)
    k_rope = _rope_pair(down[:, MLA_Q_LORA + MLA_KV_LORA:] * trig)
    k_rope = jnp.where(lane < MLA_ROPE, k_rope, 0.0).astype(BF16)
    q = _dot(cq, wq_ref[...])
    kv = _dot(ckv, wkv_ref[...])
    qs = MLA_SCALE * math.log2(math.e)
    for h in range(MLA_HEADS):
        c0 = h * MLA_QK_PAD
        q_rope = _rope_pair(q[:, c0 + LANES:c0 + 2 * LANES] * trig)
        qt_ref[0, h, :LANES, :] = (q[:, c0:c0 + LANES] * qs).T.astype(BF16)
        qt_ref[0, h, LANES:, :] = (q_rope * qs).T.astype(BF16)
        k_ref[:, c0:c0 + LANES] = kv[:, h * MLA_NOPE:(h + 1) * MLA_NOPE].astype(BF16)
        k_ref[:, c0 + LANES:c0 + 2 * LANES] = k_rope
        v0 = MLA_HEADS * MLA_NOPE + h * MLA_V
        vt_ref[0, h, 0, :MLA_V, :] = kv[:, v0:v0 + MLA_V].T.astype(BF16)
        pad_row = lax.broadcasted_iota(jnp.int32, (MLA_V_PAD - MLA_V, x.shape[0]), 0)
        vt_ref[0, h, 0, MLA_V:, :] = jnp.where(pad_row == 0, 1.0, 0.0).astype(BF16)


def _mla_proj(x, mod, gpre, trig, wd, qn, kvn, wq, wkv, *, seq, tm):
    t, d = x.shape
    nq = MLA_HEADS * MLA_QK_PAD
    nt = seq // tm
    return pl.pallas_call(
        _mla_proj_kernel,
        grid=(t // tm,),
        in_specs=[
            pl.BlockSpec((tm, d), lambda i: (i, 0)),
            pl.BlockSpec((1, 3 * N_SUB, d), lambda i: ((i * tm) // seq, 0, 0)),
            _const_spec((1, d)),
            pl.BlockSpec((tm, LANES), lambda i: (i % (seq // tm), 0)),
            _const_spec(wd.shape),
            _const_spec(qn.shape),
            _const_spec(kvn.shape),
            _const_spec(wq.shape),
            _const_spec(wkv.shape),
        ],
        out_specs=[
            pl.BlockSpec((1, MLA_HEADS, MLA_QK_PAD, tm), lambda i: (i // nt, 0, 0, i % nt)),
            pl.BlockSpec((tm, nq), lambda i: (i, 0)),
            pl.BlockSpec((1, MLA_HEADS, 1, MLA_V_PAD, tm), lambda i: (i // nt, 0, i % nt, 0, 0)),
        ],
        out_shape=[
            jax.ShapeDtypeStruct((t // seq, MLA_HEADS, MLA_QK_PAD, seq), BF16),
            jax.ShapeDtypeStruct((t, nq), BF16),
            jax.ShapeDtypeStruct((t // seq, MLA_HEADS, nt, MLA_V_PAD, tm), BF16),
        ],
        compiler_params=_cparams(("parallel",)),
        name="mla_proj",
    )(x, mod, gpre, trig, wd, qn, kvn, wq, wkv)


def _flash_kernel(qt_ref, k_ref, vt_ref, o_ref, s_ref, p_ref, acc_ref):
    qt = qt_ref[0, 0]
    tq = qt.shape[1]
    nk, _, tk = vt_ref.shape[2:]
    assert nk == 1 or nk % 2 == 0

    def scores(j, slot):
        r0 = pl.multiple_of(j * tk, tk)
        s = _dot(k_ref[0, pl.ds(r0, tk), :], qt)
        s_ref[slot] = s
        return jnp.max(s.reshape(tk // 8, 8, tq), axis=0)

    def softmax(slot, cmax, m):
        m_new = jnp.maximum(m, jnp.max(cmax, axis=0, keepdims=True))
        alpha = jnp.exp2(m - m_new)
        p_ref[slot] = jnp.exp2((s_ref[slot] - m_new).astype(BF16))
        return m_new, alpha

    def weighted(j, slot, alpha):
        acc_ref[...] = alpha * acc_ref[...] + _dot(vt_ref[0, 0, j], p_ref[slot])

    def step(j, slot, cmax, m, alpha):
        weighted(j - 1, 1 - slot, alpha)
        cmax_next = scores(j + 1, 1 - slot)
        return (cmax_next,) + softmax(slot, cmax, m)

    acc_ref[...] = jnp.zeros_like(acc_ref)
    m = jnp.full((1, tq), -1e30, F32)
    cmax = scores(0, 0)
    if nk > 1:
        cmax_next = scores(1, 1)
    m, alpha = softmax(0, cmax, m)
    if nk > 1:
        unroll = 4 if nk >= 8 else 2
        trips = (nk - 2) // unroll

        def body(t, carry):
            j = 1 + unroll * t
            for u in range(unroll):
                carry = step(j + u, (1 + u) % 2, *carry)
            return carry

        carry = lax.fori_loop(0, trips, body, (cmax_next, m, alpha))
        for j in range(1 + unroll * trips, nk - 1):
            carry = step(j, j % 2, *carry)
        cmax, m, alpha = carry
        weighted(nk - 2, 0, alpha)
        m, alpha = softmax(1, cmax, m)
    weighted(nk - 1, (nk - 1) % 2, alpha)
    acc = acc_ref[...]
    o_t = acc[:MLA_V] / acc[MLA_V:MLA_V + 1]
    o_ref[0] = o_t.T.astype(o_ref.dtype)


def _flash(qt, k, vt, *, tq):
    b, s, _ = k.shape
    _, _, nk, dv, tk = vt.shape
    return pl.pallas_call(
        _flash_kernel,
        grid=(b, MLA_HEADS, s // tq),
        in_specs=[
            pl.BlockSpec((1, 1, MLA_QK_PAD, tq), lambda bi, h, i: (bi, h, 0, i)),
            pl.BlockSpec((1, s, MLA_QK_PAD), lambda bi, h, i: (bi, 0, h)),
            pl.BlockSpec((1, 1, nk, dv, tk), lambda bi, h, i: (bi, h, 0, 0, 0)),
        ],
        out_specs=pl.BlockSpec((1, tq, MLA_V), lambda bi, h, i: (bi, i, h)),
        out_shape=jax.ShapeDtypeStruct((b, s, MLA_HEADS * MLA_V), BF16),
        scratch_shapes=[
            pltpu.VMEM((2, tk, tq), F32),
            pltpu.VMEM((2, tk, tq), BF16),
            pltpu.VMEM((dv, tq), F32),
        ],
        compiler_params=_cparams(("parallel", "parallel", "arbitrary")),
        name="mla_flash",
    )(qt, k, vt)


def _out_proj_kernel(a_ref, x_ref, mod_ref, gpost_ref, w_ref, b_ref, o_ref):
    out = _dot(a_ref[...].astype(BF16), w_ref[...]) + b_ref[...]
    o_ref[...] = _residual(x_ref[...], out, mod_ref, gpost_ref, 1, 1.0)


def _out_proj(a, x, mod, gpost, w, b, *, seq, tm):
    t, d = x.shape
    return pl.pallas_call(
        _out_proj_kernel,
        grid=(t // tm,),
        in_specs=[
            pl.BlockSpec((tm, a.shape[1]), lambda i: (i, 0)),
            pl.BlockSpec((tm, d), lambda i: (i, 0)),
            pl.BlockSpec((1, 3 * N_SUB, d), lambda i: ((i * tm) // seq, 0, 0)),
            _const_spec((1, d)),
            _const_spec(w.shape),
            _const_spec((1, d)),
        ],
        out_specs=pl.BlockSpec((tm, d), lambda i: (i, 0)),
        out_shape=jax.ShapeDtypeStruct((t, d), F32),
        compiler_params=_cparams(("parallel",)),
        name="out_proj",
    )(a, x, mod, gpost, w, b)


def _gdn_proj_kernel(xp_ref, x_ref, xn_ref, mod_ref, gpre_ref, wqkv_ref, wg_ref, wab_ref, conv_ref,
                     aneg_ref, dtb_ref, q_ref, k_ref, v_ref, gate_ref, gb_ref, p_ref,
                     *, tiles_per_seq):
    i = pl.program_id(0)
    tm = x_ref.shape[0]
    first = (i % tiles_per_seq) == 0
    last = (i % tiles_per_seq) == tiles_per_seq - 1

    def proj_rows(xr):
        return _dot(_modulated(xr, mod_ref, gpre_ref, 1).astype(BF16), wqkv_ref[...])

    hb = _modulated(x_ref[...], mod_ref, gpre_ref, 1).astype(BF16)
    p_prev = jnp.where(first, 0.0, proj_rows(xp_ref[...]))
    p_main = _dot(hb, wqkv_ref[...])
    p_next = jnp.where(last, 0.0, proj_rows(xn_ref[...]))
    n_slab = p_ref.shape[0]
    for c in range(n_slab):
        cs = slice(c * LANES, (c + 1) * LANES)
        p_ref[c, 0:GDN_HALO, :] = p_prev[:, cs]
        p_ref[c, GDN_HALO:GDN_HALO + tm, :] = p_main[:, cs]
        p_ref[c, GDN_HALO + tm:, :] = p_next[:, cs]

    def l2n(z):
        return z * lax.rsqrt(jnp.sum(z * z, axis=-1, keepdims=True) + NORM_EPS)

    pad = GDN_CONV // 2
    for c in range(n_slab):
        cs = slice(c * LANES, (c + 1) * LANES)
        acc = None
        for tap in range(GDN_CONV):
            r0 = GDN_HALO - pad + tap
            term = p_ref[c, r0:r0 + tm, :] * conv_ref[tap:tap + 1, cs]
            acc = term if acc is None else acc + term
        z = acc * jax.nn.sigmoid(acc)
        if c < GDN_HEADS:
            q_ref[:, cs] = l2n(z) * (GDN_DK ** -0.5)
        elif c < 2 * GDN_HEADS:
            k_ref[:, c * LANES - GDN_QK:(c + 1) * LANES - GDN_QK] = l2n(z)
        else:
            v_ref[:, c * LANES - 2 * GDN_QK:(c + 1) * LANES - 2 * GDN_QK] = z
    gate_ref[...] = _dot(hb, wg_ref[...])
    ab = _dot(hb, wab_ref[...])
    z = ab + dtb_ref[...]
    softplus = jnp.maximum(z, 0.0) + jnp.log(1.0 + jnp.exp(-jnp.abs(z)))
    lane = lax.broadcasted_iota(jnp.int32, ab.shape, 1)
    gb_ref[...] = jnp.where(lane < 2 * GDN_HEADS, aneg_ref[...] * softplus, jax.nn.sigmoid(ab))


def _gdn_proj(x, mod, gpre, wqkv, wg, wab, conv, aneg, dtb, *, seq, tm):
    t, d = x.shape
    hb = tm // GDN_HALO
    nblk8 = t // GDN_HALO
    n_conv = wqkv.shape[1]
    return pl.pallas_call(
        functools.partial(_gdn_proj_kernel, tiles_per_seq=seq // tm),
        grid=(t // tm,),
        in_specs=[
            pl.BlockSpec((GDN_HALO, d), lambda i: (jnp.maximum(i * hb - 1, 0), 0)),
            pl.BlockSpec((tm, d), lambda i: (i, 0)),
            pl.BlockSpec((GDN_HALO, d), lambda i: (jnp.minimum((i + 1) * hb, nblk8 - 1), 0)),
            pl.BlockSpec((1, 3 * N_SUB, d), lambda i: ((i * tm) // seq, 0, 0)),
            _const_spec((1, d)),
            _const_spec(wqkv.shape),
            _const_spec(wg.shape),
            _const_spec(wab.shape),
            _const_spec(conv.shape),
            _const_spec(aneg.shape),
            _const_spec(dtb.shape),
        ],
        out_specs=[
            pl.BlockSpec((tm, GDN_QK), lambda i: (i, 0)),
            pl.BlockSpec((tm, GDN_QK), lambda i: (i, 0)),
            pl.BlockSpec((tm, GDN_VW), lambda i: (i, 0)),
            pl.BlockSpec((tm, GDN_VW), lambda i: (i, 0)),
            pl.BlockSpec((tm, LANES), lambda i: (i, 0)),
        ],
        out_shape=[
            jax.ShapeDtypeStruct((t, GDN_QK), F32),
            jax.ShapeDtypeStruct((t, GDN_QK), F32),
            jax.ShapeDtypeStruct((t, GDN_VW), F32),
            jax.ShapeDtypeStruct((t, GDN_VW), F32),
            jax.ShapeDtypeStruct((t, LANES), F32),
        ],
        scratch_shapes=[pltpu.VMEM((n_conv // LANES, tm + 2 * GDN_HALO, LANES), F32)],
        compiler_params=_cparams(("parallel",)),
        name="gdn_proj",
    )(x, x, x, mod, gpre, wqkv, wg, wab, conv, aneg, dtb)


def _gdn_cumsum_kernel(gb_ref, o_ref):
    g = gb_ref[...]
    tm = g.shape[0]
    row = lax.broadcasted_iota(jnp.int32, (tm, tm), 0)
    col = lax.broadcasted_iota(jnp.int32, (tm, tm), 1)
    sh = int(math.log2(GDN_CHUNK))
    same = lax.shift_right_logical(row, sh) == lax.shift_right_logical(col, sh)
    lower = jnp.where(same & (col <= row), 1.0, 0.0).astype(BF16)
    upper = jnp.where(same & (col >= row), 1.0, 0.0).astype(BF16)
    g1 = g.astype(BF16)
    r1 = g - g1.astype(F32)
    g2 = r1.astype(BF16)
    g3 = (r1 - g2.astype(F32)).astype(BF16)
    pre = _dot(lower, g1) + _dot(lower, g2) + _dot(lower, g3)
    suf = _dot(upper, g1) + _dot(upper, g2) + _dot(upper, g3)
    lane = lax.broadcasted_iota(jnp.int32, g.shape, 1)
    o_ref[...] = jnp.where(lane < GDN_HEADS, pre, jnp.where(lane < 2 * GDN_HEADS, suf, g))


def _gdn_cumsum(gb, *, tm):
    t = gb.shape[0]
    return pl.pallas_call(
        _gdn_cumsum_kernel,
        grid=(t // tm,),
        in_specs=[pl.BlockSpec((tm, LANES), lambda i: (i, 0))],
        out_specs=pl.BlockSpec((tm, LANES), lambda i: (i, 0)),
        out_shape=jax.ShapeDtypeStruct((t, LANES), F32),
        compiler_params=_cparams(("parallel",)),
        name="gdn_cumsum",
    )(gb)


def _gdn_delta_kernel(q_ref, k_ref, v_ref, gcol_ref, bcol_ref, grow_ref, o_ref, state_ref,
                      *, reverse):
    c = GDN_CHUNK
    nchunk = q_ref.shape[0] // c

    @pl.when(pl.program_id(1) == 0)
    def _():
        state_ref[...] = jnp.zeros_like(state_ref)

    row = lax.broadcasted_iota(jnp.int32, (c, c), 0)
    col = lax.broadcasted_iota(jnp.int32, (c, c), 1)
    if reverse:
        row, col = col, row
    tri = row >= col
    strict = row > col
    eye = jnp.where(row == col, 1.0, 0.0)
    level_masks = []
    for lv in range(int(math.log2(c))):
        rb = lax.shift_right_logical(row, lv)
        cb = lax.shift_right_logical(col, lv)
        level_masks.append((lax.shift_right_logical(rb, 1) == lax.shift_right_logical(cb, 1))
                           & ((rb & 1) == 1) & ((cb & 1) == 0))
    last_row = 0 if reverse else c - 1

    heads = range(GDN_HEADS)
    order = list(reversed(range(nchunk))) if reverse else list(range(nchunk))
    probs = [(j, h) for j in order for h in heads]
    rows = lambda j: slice(j * c, (j + 1) * c)
    cols = lambda h: slice(h * GDN_DK, (h + 1) * GDN_DK)

    qs = [q_ref[rows(j), cols(h)] for j, h in probs]
    ks = [k_ref[rows(j), cols(h)] for j, h in probs]
    vs = [v_ref[rows(j), cols(h)] for j, h in probs]
    gcbs = [jnp.broadcast_to(gcol_ref[0, rows(j), h:h + 1], (c, GDN_DK)) for j, h in probs]
    betas = [jnp.broadcast_to(bcol_ref[0, rows(j), h:h + 1], (c, GDN_DK)) for j, h in probs]
    n = len(probs)
    decays = []
    for p, (j, h) in enumerate(probs):
        grb = jnp.broadcast_to(grow_ref[0, j, h:h + 1, :], (c, c))
        decays.append(jnp.where(tri, jnp.exp(jnp.where(tri, gcbs[p][:, :c] - grb, 0.0)), 0.0))
    kbs = [ks[p] * betas[p] for p in range(n)]
    a2s = [_dot_nt(jnp.concatenate([kbs[p], qs[p]], axis=0).astype(BF16), ks[p].astype(BF16))
           for p in range(n)]
    ms = [jnp.where(strict, a2s[p][:c] * decays[p], 0.0) for p in range(n)]
    qks = [(a2s[p][c:] * decays[p]).astype(BF16) for p in range(n)]
    tinvs = [eye - jnp.where(level_masks[0], ms[p], 0.0) for p in range(n)]
    for lm in level_masks[1:]:
        tbs = [t.astype(BF16) for t in tinvs]
        tcs = [_dot(tbs[p], jnp.where(lm, ms[p], 0.0).astype(BF16)).astype(BF16) for p in range(n)]
        tinvs = [tinvs[p] - _dot(tcs[p], tbs[p]) for p in range(n)]
    egs = [jnp.exp(g) for g in gcbs]
    sols = [_dot(tinvs[p].astype(BF16),
                 jnp.concatenate([vs[p] * betas[p], kbs[p] * egs[p]], axis=1).astype(BF16))
            for p in range(n)]
    g_lasts = [g[last_row:last_row + 1, :] for g in gcbs]
    k_decs = [(ks[p] * jnp.exp(g_lasts[p] - gcbs[p])).astype(BF16) for p in range(n)]
    wq_lhs = [jnp.concatenate([sols[p][:, GDN_DV:], qs[p] * egs[p]], axis=0).astype(BF16)
              for p in range(n)]

    sts = [state_ref[h] for h in heads]
    for step, j in enumerate(order):
        ps = [step * GDN_HEADS + h for h in heads]
        wqs = [_dot(wq_lhs[ps[h]], sts[h].astype(BF16)) for h in heads]
        vnbs = [(sols[ps[h]][:, :GDN_DV] - wqs[h][:c]).astype(BF16) for h in heads]
        for h in heads:
            o_ref[rows(j), cols(h)] = wqs[h][c:] + _dot(qks[ps[h]], vnbs[h])
        sts = [sts[h] * jnp.exp(g_lasts[ps[h]]) + _dot_tn(k_decs[ps[h]], vnbs[h]) for h in heads]
    for h in heads:
        state_ref[h] = sts[h]


def _gdn_delta(q, k, v, gcol, bcol, grow, *, batch, seq, tb, reverse):
    t = q.shape[0]
    nb = seq // tb
    cpb = tb // GDN_CHUNK

    def blk(bi, i):
        return bi * nb + ((nb - 1 - i) if reverse else i)

    def seq_blk(bi, i):
        return (nb - 1 - i) if reverse else i

    return pl.pallas_call(
        functools.partial(_gdn_delta_kernel, reverse=reverse),
        grid=(batch, nb),
        in_specs=[
            pl.BlockSpec((tb, GDN_QK), lambda bi, i: (blk(bi, i), 0)),
            pl.BlockSpec((tb, GDN_QK), lambda bi, i: (blk(bi, i), 0)),
            pl.BlockSpec((tb, GDN_VW), lambda bi, i: (blk(bi, i), 0)),
            pl.BlockSpec((1, tb, GDN_HEADS), lambda bi, i: (bi, seq_blk(bi, i), 0)),
            pl.BlockSpec((1, tb, GDN_HEADS), lambda bi, i: (bi, seq_blk(bi, i), 0)),
            pl.BlockSpec((1, cpb, GDN_HEADS, GDN_CHUNK), lambda bi, i: (bi, seq_blk(bi, i), 0, 0)),
        ],
        out_specs=pl.BlockSpec((tb, GDN_VW), lambda bi, i: (blk(bi, i), 0)),
        out_shape=jax.ShapeDtypeStruct((t, GDN_VW), F32),
        scratch_shapes=[pltpu.VMEM((GDN_HEADS, GDN_DK, GDN_DV), F32)],
        compiler_params=_cparams(("parallel", "arbitrary")),
        name="gdn_delta_bw" if reverse else "gdn_delta_fw",
    )(q, k, v, gcol, bcol, grow)


def _gdn_out_kernel(of_ref, ob_ref, gate_ref, x_ref, mod_ref, gpost_ref, onorm_ref, w_ref, o_ref,
                    a_ref):
    o = of_ref[...] + ob_ref[...]
    gate = gate_ref[...]
    og = gate * jax.nn.sigmoid(gate)
    for h in range(GDN_HEADS):
        cs = slice(h * GDN_DV, (h + 1) * GDN_DV)
        a_ref[:, cs] = (_rms(o[:, cs], onorm_ref[...]) * og[:, cs]).astype(BF16)
    out = _dot(a_ref[...], w_ref[...])
    o_ref[...] = _residual(x_ref[...], out, mod_ref, gpost_ref, 1, 1.0)


def _gdn_out(o_fw, o_bw, gate, x, mod, gpost, onorm, w, *, seq, tm):
    t, d = x.shape
    return pl.pallas_call(
        _gdn_out_kernel,
        grid=(t // tm,),
        in_specs=[
            pl.BlockSpec((tm, GDN_VW), lambda i: (i, 0)),
            pl.BlockSpec((tm, GDN_VW), lambda i: (i, 0)),
            pl.BlockSpec((tm, GDN_VW), lambda i: (i, 0)),
            pl.BlockSpec((tm, d), lambda i: (i, 0)),
            pl.BlockSpec((1, 3 * N_SUB, d), lambda i: ((i * tm) // seq, 0, 0)),
            _const_spec((1, d)),
            _const_spec(onorm.shape),
            _const_spec(w.shape),
        ],
        out_specs=pl.BlockSpec((tm, d), lambda i: (i, 0)),
        out_shape=jax.ShapeDtypeStruct((t, d), F32),
        scratch_shapes=[pltpu.VMEM((tm, GDN_VW), BF16)],
        compiler_params=_cparams(("parallel",)),
        name="gdn_out",
    )(o_fw, o_bw, gate, x, mod, gpost, onorm, w)


def _fnet_chan_kernel(x_ref, mod_ref, gpre_ref, cs_ref, o_ref):
    hb = _modulated(x_ref[...], mod_ref, gpre_ref, 1).astype(BF16)
    cg = hb.shape[1] // FNET_GROUPS
    for g in range(FNET_GROUPS):
        ab = _dot(hb[:, g * cg:(g + 1) * cg], cs_ref[...])
        o_ref[0, 0, :, g * cg:(g + 1) * cg] = ab[:, :cg]
        o_ref[0, 1, :, g * cg:(g + 1) * cg] = ab[:, cg:]


def _fnet_chan(x, mod, gpre, cs, *, batch, seq, tm):
    t, d = x.shape
    nt = seq // tm
    return pl.pallas_call(
        _fnet_chan_kernel,
        grid=(t // tm,),
        in_specs=[
            pl.BlockSpec((tm, d), lambda i: (i, 0)),
            pl.BlockSpec((1, 3 * N_SUB, d), lambda i: ((i * tm) // seq, 0, 0)),
            _const_spec((1, d)),
            _const_spec(cs.shape),
        ],
        out_specs=pl.BlockSpec((1, 2, tm, d), lambda i: (i // nt, 0, i % nt, 0)),
        out_shape=jax.ShapeDtypeStruct((batch, 2, seq, d), F32),
        compiler_params=_cparams(("parallel",)),
        name="fnet_chan",
    )(x, mod, gpre, cs)


def _fnet_s1_kernel(m1_ref, ab_ref, y_ref):
    _, two, s1, _, rows, d = ab_ref.shape
    ab = ab_ref[0].reshape(two * s1 * rows, d).astype(BF16)
    y = _dot(m1_ref[...], ab)
    y_ref[0] = y.reshape(two, s1, 1, rows, d)


def _fnet_s1(m1, ab):
    batch, _, s1, groups, rows, d = ab.shape
    blk = (1, 2, s1, 1, rows, d)
    return pl.pallas_call(
        _fnet_s1_kernel,
        grid=(batch, groups),
        in_specs=[
            _const_spec(m1.shape),
            pl.BlockSpec(blk, lambda bi, g: (bi, 0, 0, g, 0, 0)),
        ],
        out_specs=pl.BlockSpec(blk, lambda bi, g: (bi, 0, 0, g, 0, 0)),
        out_shape=jax.ShapeDtypeStruct(ab.shape, F32),
        compiler_params=_cparams(("parallel", "parallel")),
        name="fnet_s1",
    )(m1, ab)


def _fnet_s2_kernel(m2_ref, tw_ref, y_ref, f_ref, z_ref):
    n, d = y_ref.shape[2:]
    cos = tw_ref[0]
    sin = tw_ref[1]
    for c0 in range(0, d, LANES):
        yr = y_ref[0, 0, :, c0:c0 + LANES]
        yi = y_ref[0, 1, :, c0:c0 + LANES]
        z_ref[:n, c0:c0 + LANES] = (yr * cos + yi * sin).astype(BF16)
        z_ref[n:, c0:c0 + LANES] = (yi * cos - yr * sin).astype(BF16)
    f = _dot(m2_ref[...], z_ref[...])
    f_ref[0] = f.reshape(f_ref.shape[1:])


def _fnet_s2(m2, tw, y, *, s1, s2):
    batch, _, _, d = y.shape
    n = 8 * s2
    return pl.pallas_call(
        _fnet_s2_kernel,
        grid=(batch, s1 // 8),
        in_specs=[
            _const_spec(m2.shape),
            pl.BlockSpec((2, n, LANES), lambda bi, a: (0, a, 0)),
            pl.BlockSpec((1, 2, n, d), lambda bi, a: (bi, 0, a, 0)),
        ],
        out_specs=pl.BlockSpec((1, s2, 1, 8, d), lambda bi, a: (bi, 0, a, 0, 0)),
        out_shape=jax.ShapeDtypeStruct((batch, s2, s1 // 8, 8, d), F32),
        scratch_shapes=[pltpu.VMEM((2 * n, d), BF16)],
        compiler_params=_cparams(("parallel", "parallel")),
        name="fnet_s2",
    )(m2, tw, y)


def _rope_table(seq):
    half = MLA_ROPE // 2
    pos = jnp.arange(seq, dtype=F32)
    inv_freq = ROPE_THETA ** (-jnp.arange(half, dtype=F32) / half)
    ang = pos[:, None] * inv_freq[None, :]
    cos, sin = jnp.cos(ang), jnp.sin(ang)
    return jnp.concatenate([cos, cos, -sin, sin], axis=1)


def _swap_halves(w):
    half = w.shape[-1] // 2
    return jnp.concatenate([w[..., half:], w[..., :half]], axis=-1)


def _prep_mla(w_down, w_uq, w_ukv):
    d = w_down.shape[0]
    rope = w_down[:, MLA_Q_LORA + MLA_KV_LORA:]
    wd = jnp.concatenate([w_down, _swap_halves(rope)], axis=1).astype(BF16)
    uq = w_uq.reshape(MLA_Q_LORA, MLA_HEADS, MLA_NOPE + MLA_ROPE)
    uq_rope = uq[..., MLA_NOPE:]
    wq = jnp.concatenate([uq, _swap_halves(uq_rope)], axis=-1)
    wq = wq.reshape(MLA_Q_LORA, MLA_HEADS * MLA_QK_PAD).astype(BF16)
    ukv = w_ukv.reshape(MLA_KV_LORA, MLA_HEADS, MLA_NOPE + MLA_V)
    wkv = jnp.concatenate([ukv[..., :MLA_NOPE].reshape(MLA_KV_LORA, -1),
                           ukv[..., MLA_NOPE:].reshape(MLA_KV_LORA, -1)], axis=1).astype(BF16)
    del d
    return wd, wq, wkv


def _fnet_tables(seq, cg):
    def trig(n):
        idx = jnp.arange(n, dtype=jnp.int32)
        ang = ((idx[:, None] * idx[None, :]) % n).astype(F32) * (2.0 * math.pi / n)
        return jnp.cos(ang), jnp.sin(ang)

    cc, sc = trig(cg)
    cs = (jnp.concatenate([cc, sc], axis=1) * (cg ** -0.5)).astype(BF16)
    s2 = min(seq, LANES)
    s1 = seq // s2
    eye8 = jnp.eye(8, dtype=F32)
    c1, sn1 = trig(s1)
    m1 = jnp.block([[c1, -sn1], [-sn1, -c1]]) * (s1 ** -0.5)
    m1 = jnp.kron(m1, eye8).astype(BF16)
    c2, sn2 = trig(s2)
    m2 = jnp.stack([c2, sn2]) * (s2 ** -0.5)
    m2 = jnp.einsum('cts,rq->trcqs', m2, eye8).reshape(8 * s2, 16 * s2).astype(BF16)
    t1 = jnp.arange(s1, dtype=jnp.int32)[:, None]
    k2 = jnp.arange(s2, dtype=jnp.int32)[None, :]
    ang = ((t1 * k2).astype(F32) * (2.0 * math.pi / seq)).reshape(seq)
    tw = jnp.stack([jnp.cos(ang), jnp.sin(ang)])
    tw = jnp.broadcast_to(tw[..., None], (2, seq, LANES))
    return cs, m1, m2, tw


def _pick(n, pref):
    return pref if n % pref == 0 else n


def _trunk(x3, mods, prm):
    batch, seq, d = x3.shape
    t = batch * seq
    x = x3.reshape(t, d)
    tm = _pick(seq, 512)
    row = lambda v: v.reshape(1, -1)
    trig = _rope_table(seq)
    for l in range(mods.shape[0]):
        i = l // 3
        kind = l % 3
        mod = mods[l]
        gpre = prm['norm_pre'][l]
        gpost = prm['norm_post'][l]
        x = _ffn(x, mod, row(gpre[0]), row(gpost[0]), prm['ffn_w_in'][l][0], prm['ffn_w_out'][l][0],
                 sub=0, seq=seq, tm=tm)
        if kind == 0:
            wd, wq, wkv = prm['mla'][i]
            qt, k, vt = _mla_proj(x, mod, row(gpre[1]), trig, wd, row(prm['mla_q_norm'][i]),
                                  row(prm['mla_kv_norm'][i]), wq, wkv, seq=seq, tm=tm)
            o = _flash(qt, k.reshape(batch, seq, -1), vt, tq=_pick(seq, 512))
            x = _out_proj(o.reshape(t, -1), x, mod, row(gpost[1]), prm['mla_w_out'][i],
                          jnp.zeros((1, d), F32), seq=seq, tm=tm)
        elif kind == 1:
            g = prm['gdn'][i]
            q, k, v, gate, gb = _gdn_proj(x, mod, row(gpre[1]), g['wqkv'], g['wg'], g['wab'], g['conv'],
                                          g['aneg'], g['dtb'], seq=seq, tm=_pick(seq, 256))
            gc = _gdn_cumsum(gb, tm=tm)
            nh = GDN_HEADS
            gc3 = gc.reshape(batch, seq, LANES)
            outs = []
            for dr in range(2):
                gcol = gc3[:, :, dr * nh:(dr + 1) * nh]
                bcol = gc3[:, :, 2 * nh + dr * nh:2 * nh + (dr + 1) * nh]
                grow = jnp.swapaxes(gcol.reshape(batch, seq // GDN_CHUNK, GDN_CHUNK, nh), 2, 3)
                outs.append(_gdn_delta(q, k, v, gcol, bcol, grow, batch=batch, seq=seq,
                                       tb=_pick(seq, 256), reverse=bool(dr)))
            x = _gdn_out(outs[0], outs[1], gate, x, mod, row(gpost[1]), row(prm['gdn_o_norm'][i]),
                         g['wout'], seq=seq, tm=tm)
        else:
            cs, m1, m2, tw = _fnet_tables(seq, d // FNET_GROUPS)
            s2 = min(seq, LANES)
            s1 = seq // s2
            ab = _fnet_chan(x, mod, row(gpre[1]), cs, batch=batch, seq=seq, tm=tm)
            y = _fnet_s1(m1, ab.reshape(batch, 2, s1, s2 // 8, 8, d))
            f = _fnet_s2(m2, tw, y.reshape(batch, 2, seq, d), s1=s1, s2=s2)
            x = _out_proj(f.reshape(t, d), x, mod, row(gpost[1]), prm['fnet_w_out'][i],
                          row(prm['fnet_b_out'][i]), seq=seq, tm=tm)
        x = _ffn(x, mod, row(gpre[2]), row(gpost[2]), prm['ffn_w_in'][l][1], prm['ffn_w_out'][l][1],
                 sub=2, seq=seq, tm=tm)
    return x.reshape(batch, seq, d)


def _prep_gdn(w_in, conv, a_log, dt_bias, w_out):
    n_conv = 2 * GDN_QK + GDN_VW
    wab = jnp.pad(w_in[:, n_conv + GDN_VW:], ((0, 0), (0, LANES - 4 * GDN_HEADS)))
    pad16 = lambda v: jnp.pad(v.reshape(1, -1).astype(F32), ((0, 0), (0, LANES - 2 * GDN_HEADS)))
    return {
        'wqkv': w_in[:, :n_conv].astype(BF16),
        'wg': w_in[:, n_conv:n_conv + GDN_VW].astype(BF16),
        'wab': wab.astype(BF16),
        'conv': conv.astype(F32),
        'aneg': pad16(-jnp.exp(a_log.astype(F32))),
        'dtb': pad16(dt_bias),
        'wout': w_out.astype(BF16),
    }


def kernel(x_prompt, x_sample, c_prompt, c_sample, w_ada, b_ada, norm_pre, norm_post, ffn_w_in, ffn_w_out, mla_w_down, mla_q_norm, mla_kv_norm, mla_w_uq, mla_w_ukv, mla_w_out, gdn_w_in, gdn_conv, gdn_a_log, gdn_dt_bias, gdn_o_norm, gdn_w_out, fnet_w_out, fnet_b_out):
    d = x_prompt.shape[-1]
    prm = {
        'norm_pre': norm_pre, 'norm_post': norm_post,
        'ffn_w_in': ffn_w_in.astype(BF16), 'ffn_w_out': ffn_w_out.astype(BF16),
        'mla': [_prep_mla(mla_w_down[i], mla_w_uq[i], mla_w_ukv[i]) for i in range(mla_w_down.shape[0])],
        'mla_q_norm': mla_q_norm, 'mla_kv_norm': mla_kv_norm, 'mla_w_out': mla_w_out.astype(BF16),
        'gdn': [_prep_gdn(gdn_w_in[i], gdn_conv[i], gdn_a_log[i], gdn_dt_bias[i], gdn_w_out[i])
                for i in range(gdn_w_in.shape[0])],
        'gdn_o_norm': gdn_o_norm,
        'fnet_w_out': fnet_w_out.astype(BF16), 'fnet_b_out': fnet_b_out,
    }
    nbp = c_prompt.shape[0]
    c_all = jnp.concatenate([c_prompt, c_sample], axis=0)
    mods = _ada_mod(c_all, w_ada, b_ada).reshape(w_ada.shape[0], c_all.shape[0], 3 * N_SUB, d)
    y_prompt = _trunk(x_prompt, mods[:, :nbp], prm)
    y_sample = _trunk(x_sample, mods[:, nbp:], prm)
    return (y_prompt, y_sample)
```

```python
import functools
import math

import jax
import jax.numpy as jnp
from jax import lax
from jax.experimental import pallas as pl
from jax.experimental.pallas import tpu as pltpu

F32 = jnp.float32
BF16 = jnp.bfloat16

N_SUB = 3
D_FF = 2816
NORM_EPS = 1e-6

MLA_HEADS = 8
MLA_Q_LORA = 384
MLA_KV_LORA = 256
MLA_NOPE = 128
MLA_ROPE = 64
MLA_V = 128
ROPE_THETA = 10000.0
MLA_SCALE = (MLA_NOPE + MLA_ROPE) ** -0.5
MLA_QK_PAD = 256
MLA_V_PAD = MLA_V + 16

GDN_HEADS = 8
GDN_DK = 128
GDN_DV = 128
GDN_QK = GDN_HEADS * GDN_DK
GDN_VW = GDN_HEADS * GDN_DV
GDN_CONV = 5
GDN_CHUNK = 64
GDN_HALO = 8

FNET_GROUPS = 8

LANES = 128
VMEM_LIMIT = 56 * 1024 * 1024


def _cparams(sem):
    return pltpu.CompilerParams(dimension_semantics=sem, vmem_limit_bytes=VMEM_LIMIT)


def _const_spec(shape):
    nd = len(shape)
    return pl.BlockSpec(shape, lambda *_: (0,) * nd, pipeline_mode=pl.Buffered(1))


def _rms(x, g):
    ms = jnp.mean(x * x, axis=-1, keepdims=True)
    return x * lax.rsqrt(ms + NORM_EPS) * g


def _modulated(x, mod_ref, gpre_ref, sub):
    shift = mod_ref[0, 3 * sub:3 * sub + 1, :]
    scale = mod_ref[0, 3 * sub + 1:3 * sub + 2, :]
    return _rms(x, gpre_ref[...]) * (1.0 + scale) + shift


def _residual(x, out, mod_ref, gpost_ref, sub, weight):
    gate = mod_ref[0, 3 * sub + 2:3 * sub + 3, :]
    return x + (weight * gate) * _rms(out, gpost_ref[...])


def _dot(a, b):
    return jnp.dot(a, b, preferred_element_type=F32)


def _dot_nt(a, b):
    return lax.dot_general(a, b, (((1,), (1,)), ((), ())), preferred_element_type=F32)


def _dot_tn(a, b):
    return lax.dot_general(a, b, (((0,), (0,)), ((), ())), preferred_element_type=F32)


def _ada_kernel(c_ref, w_ref, b_ref, o_ref):
    c = c_ref[...]
    sc = (c * jax.nn.sigmoid(c)).astype(BF16)
    o_ref[0] = _dot(sc, w_ref[0].astype(BF16)) + b_ref[0]


def _ada_mod(c_all, w_ada, b_ada):
    nb, d = c_all.shape
    depth, _, n_out = w_ada.shape
    tn = n_out // 8
    return pl.pallas_call(
        _ada_kernel,
        grid=(depth, n_out // tn),
        in_specs=[
            pl.BlockSpec((nb, d), lambda l, j: (0, 0)),
            pl.BlockSpec((1, d, tn), lambda l, j: (l, 0, j)),
            pl.BlockSpec((1, 1, tn), lambda l, j: (l, 0, j)),
        ],
        out_specs=pl.BlockSpec((1, nb, tn), lambda l, j: (l, 0, j)),
        out_shape=jax.ShapeDtypeStruct((depth, nb, n_out), F32),
        compiler_params=_cparams(("arbitrary", "arbitrary")),
        name="ada_mod",
    )(c_all, w_ada, b_ada.reshape(depth, 1, n_out))


def _ffn_kernel(x_ref, mod_ref, gpre_ref, gpost_ref, win_ref, wout_ref, o_ref, act_ref,
                *, sub, ck):
    x = x_ref[...]
    hb = _modulated(x, mod_ref, gpre_ref, sub).astype(BF16)
    for c in range(D_FF // ck):
        g = _dot(hb, win_ref[:, c * ck:(c + 1) * ck])
        u = _dot(hb, win_ref[:, D_FF + c * ck:D_FF + (c + 1) * ck])
        act_ref[:, c * ck:(c + 1) * ck] = (g * jax.nn.sigmoid(g) * u).astype(BF16)
    out = _dot(act_ref[...], wout_ref[...])
    o_ref[...] = _residual(x, out, mod_ref, gpost_ref, sub, 0.5)


def _ffn(x, mod, gpre, gpost, w_in, w_out, *, sub, seq, tm):
    t, d = x.shape
    return pl.pallas_call(
        functools.partial(_ffn_kernel, sub=sub, ck=256),
        grid=(t // tm,),
        in_specs=[
            pl.BlockSpec((tm, d), lambda i: (i, 0)),
            pl.BlockSpec((1, 3 * N_SUB, d), lambda i: ((i * tm) // seq, 0, 0)),
            _const_spec((1, d)),
            _const_spec((1, d)),
            _const_spec(w_in.shape),
            _const_spec(w_out.shape),
        ],
        out_specs=pl.BlockSpec((tm, d), lambda i: (i, 0)),
        out_shape=jax.ShapeDtypeStruct((t, d), F32),
        scratch_shapes=[pltpu.VMEM((tm, D_FF), BF16)],
        compiler_params=_cparams(("parallel",)),
        name="ffn",
    )(x, mod, gpre, gpost, w_in, w_out)


def _rope_pair(t):
    return t + pltpu.roll(t, 64, axis=1)


def _mla_proj_kernel(x_ref, mod_ref, gpre_ref, trig_ref, wd_ref, qn_ref, kvn_ref, wq_ref, wkv_ref,
                     qt_ref, k_ref, vt_ref):
    x = x_ref[...]
    hb = _modulated(x, mod_ref, gpre_ref, 1).astype(BF16)
    down = _dot(hb, wd_ref[...])
    cq = _rms(down[:, :MLA_Q_LORA], qn_ref[...]).astype(BF16)
    ckv = _rms(down[:, MLA_Q_LORA:MLA_Q_LORA + MLA_KV_LORA], kvn_ref[...]).astype(BF16)
    trig = trig_ref[...]
    lane = lax.broadcasted_iota(jnp.int32, trig.shape, 1)
    k_rope = _rope_pair(down[:, MLA_Q_LORA + MLA_KV_LORA:] * trig)
    k_rope = jnp.where(lane < MLA_ROPE, k_rope, 0.0).astype(BF16)
    q = _dot(cq, wq_ref[...])
    kv = _dot(ckv, wkv_ref[...])
    qs = MLA_SCALE * math.log2(math.e)
    for h in range(MLA_HEADS):
        c0 = h * MLA_QK_PAD
        q_rope = _rope_pair(q[:, c0 + LANES:c0 + 2 * LANES] * trig)
        qt_ref[0, h, :LANES, :] = (q[:, c0:c0 + LANES] * qs).T.astype(BF16)
        qt_ref[0, h, LANES:, :] = (q_rope * qs).T.astype(BF16)
        k_ref[:, c0:c0 + LANES] = kv[:, h * MLA_NOPE:(h + 1) * MLA_NOPE].astype(BF16)
        k_ref[:, c0 + LANES:c0 + 2 * LANES] = k_rope
        v0 = MLA_HEADS * MLA_NOPE + h * MLA_V
        vt_ref[0, h, 0, :MLA_V, :] = kv[:, v0:v0 + MLA_V].T.astype(BF16)
        pad_row = lax.broadcasted_iota(jnp.int32, (MLA_V_PAD - MLA_V, x.shape[0]), 0)
        vt_ref[0, h, 0, MLA_V:, :] = jnp.where(pad_row == 0, 1.0, 0.0).astype(BF16)


def _mla_proj(x, mod, gpre, trig, wd, qn, kvn, wq, wkv, *, seq, tm):
    t, d = x.shape
    nq = MLA_HEADS * MLA_QK_PAD
    nt = seq // tm
    return pl.pallas_call(
        _mla_proj_kernel,
        grid=(t // tm,),
        in_specs=[
            pl.BlockSpec((tm, d), lambda i: (i, 0)),
            pl.BlockSpec((1, 3 * N_SUB, d), lambda i: ((i * tm) // seq, 0, 0)),
            _const_spec((1, d)),
            pl.BlockSpec((tm, LANES), lambda i: (i % (seq // tm), 0)),
            _const_spec(wd.shape),
            _const_spec(qn.shape),
            _const_spec(kvn.shape),
            _const_spec(wq.shape),
            _const_spec(wkv.shape),
        ],
        out_specs=[
            pl.BlockSpec((1, MLA_HEADS, MLA_QK_PAD, tm), lambda i: (i // nt, 0, 0, i % nt)),
            pl.BlockSpec((tm, nq), lambda i: (i, 0)),
            pl.BlockSpec((1, MLA_HEADS, 1, MLA_V_PAD, tm), lambda i: (i // nt, 0, i % nt, 0, 0)),
        ],
        out_shape=[
            jax.ShapeDtypeStruct((t // seq, MLA_HEADS, MLA_QK_PAD, seq), BF16),
            jax.ShapeDtypeStruct((t, nq), BF16),
            jax.ShapeDtypeStruct((t // seq, MLA_HEADS, nt, MLA_V_PAD, tm), BF16),
        ],
        compiler_params=_cparams(("parallel",)),
        name="mla_proj",
    )(x, mod, gpre, trig, wd, qn, kvn, wq, wkv)


def _flash_kernel(qt_ref, k_ref, vt_ref, o_ref, s_ref):
    qt = qt_ref[0, 0]
    tq = qt.shape[1]
    nk, _, tk = vt_ref.shape[2:]
    unroll = 8 if nk % 8 == 0 else (2 if nk % 2 == 0 else 1)

    def score_chunk(j, cmax):
        r0 = pl.multiple_of(j * tk, tk)
        s = _dot(k_ref[0, pl.ds(r0, tk), :], qt)
        s_ref[j] = s
        return jnp.maximum(cmax, jnp.max(s.reshape(tk // 8, 8, tq), axis=0))

    cmax = lax.fori_loop(0, nk, score_chunk, jnp.full((8, tq), -1e30, F32), unroll=unroll)
    m = jnp.max(cmax, axis=0, keepdims=True)

    def value_chunk(j, acc):
        p = jnp.exp2((s_ref[j] - m).astype(BF16))
        return acc + _dot(vt_ref[0, 0, j], p)

    acc = lax.fori_loop(0, nk, value_chunk, jnp.zeros((vt_ref.shape[3], tq), F32), unroll=unroll)
    o_t = acc[:MLA_V] / acc[MLA_V:MLA_V + 1]
    o_ref[0] = o_t.T.astype(o_ref.dtype)


def _flash(qt, k, vt, *, tq):
    b, s, _ = k.shape
    _, _, nk, dv, tk = vt.shape
    return pl.pallas_call(
        _flash_kernel,
        grid=(b, MLA_HEADS, s // tq),
        in_specs=[
            pl.BlockSpec((1, 1, MLA_QK_PAD, tq), lambda bi, h, i: (bi, h, 0, i)),
            pl.BlockSpec((1, s, MLA_QK_PAD), lambda bi, h, i: (bi, 0, h)),
            pl.BlockSpec((1, 1, nk, dv, tk), lambda bi, h, i: (bi, h, 0, 0, 0)),
        ],
        out_specs=pl.BlockSpec((1, tq, MLA_V), lambda bi, h, i: (bi, i, h)),
        out_shape=jax.ShapeDtypeStruct((b, s, MLA_HEADS * MLA_V), BF16),
        scratch_shapes=[
            pltpu.VMEM((nk, tk, tq), F32),
        ],
        compiler_params=_cparams(("parallel", "parallel", "arbitrary")),
        name="mla_flash",
    )(qt, k, vt)


def _out_proj_kernel(a_ref, x_ref, mod_ref, gpost_ref, w_ref, b_ref, o_ref):
    out = _dot(a_ref[...].astype(BF16), w_ref[...]) + b_ref[...]
    o_ref[...] = _residual(x_ref[...], out, mod_ref, gpost_ref, 1, 1.0)


def _out_proj(a, x, mod, gpost, w, b, *, seq, tm):
    t, d = x.shape
    return pl.pallas_call(
        _out_proj_kernel,
        grid=(t // tm,),
        in_specs=[
            pl.BlockSpec((tm, a.shape[1]), lambda i: (i, 0)),
            pl.BlockSpec((tm, d), lambda i: (i, 0)),
            pl.BlockSpec((1, 3 * N_SUB, d), lambda i: ((i * tm) // seq, 0, 0)),
            _const_spec((1, d)),
            _const_spec(w.shape),
            _const_spec((1, d)),
        ],
        out_specs=pl.BlockSpec((tm, d), lambda i: (i, 0)),
        out_shape=jax.ShapeDtypeStruct((t, d), F32),
        compiler_params=_cparams(("parallel",)),
        name="out_proj",
    )(a, x, mod, gpost, w, b)


def _gdn_proj_kernel(xp_ref, x_ref, xn_ref, mod_ref, gpre_ref, wqkv_ref, wg_ref, wab_ref, conv_ref,
                     aneg_ref, dtb_ref, q_ref, k_ref, v_ref, gate_ref, gb_ref, p_ref,
                     *, tiles_per_seq):
    i = pl.program_id(0)
    tm = x_ref.shape[0]
    first = (i % tiles_per_seq) == 0
    last = (i % tiles_per_seq) == tiles_per_seq - 1

    def proj_rows(xr):
        return _dot(_modulated(xr, mod_ref, gpre_ref, 1).astype(BF16), wqkv_ref[...])

    hb = _modulated(x_ref[...], mod_ref, gpre_ref, 1).astype(BF16)
    p_prev = jnp.where(first, 0.0, proj_rows(xp_ref[...]))
    p_main = _dot(hb, wqkv_ref[...])
    p_next = jnp.where(last, 0.0, proj_rows(xn_ref[...]))
    n_slab = p_ref.shape[0]
    for c in range(n_slab):
        cs = slice(c * LANES, (c + 1) * LANES)
        p_ref[c, 0:GDN_HALO, :] = p_prev[:, cs]
        p_ref[c, GDN_HALO:GDN_HALO + tm, :] = p_main[:, cs]
        p_ref[c, GDN_HALO + tm:, :] = p_next[:, cs]

    def l2n(z):
        return z * lax.rsqrt(jnp.sum(z * z, axis=-1, keepdims=True) + NORM_EPS)

    pad = GDN_CONV // 2
    for c in range(n_slab):
        cs = slice(c * LANES, (c + 1) * LANES)
        acc = None
        for tap in range(GDN_CONV):
            r0 = GDN_HALO - pad + tap
            term = p_ref[c, r0:r0 + tm, :] * conv_ref[tap:tap + 1, cs]
            acc = term if acc is None else acc + term
        z = acc * jax.nn.sigmoid(acc)
        if c < GDN_HEADS:
            q_ref[:, cs] = l2n(z) * (GDN_DK ** -0.5)
        elif c < 2 * GDN_HEADS:
            k_ref[:, c * LANES - GDN_QK:(c + 1) * LANES - GDN_QK] = l2n(z)
        else:
            v_ref[:, c * LANES - 2 * GDN_QK:(c + 1) * LANES - 2 * GDN_QK] = z
    gate_ref[...] = _dot(hb, wg_ref[...])
    ab = _dot(hb, wab_ref[...])
    z = ab + dtb_ref[...]
    softplus = jnp.maximum(z, 0.0) + jnp.log(1.0 + jnp.exp(-jnp.abs(z)))
    lane = lax.broadcasted_iota(jnp.int32, ab.shape, 1)
    gb_ref[...] = jnp.where(lane < 2 * GDN_HEADS, aneg_ref[...] * softplus, jax.nn.sigmoid(ab))


def _gdn_proj(x, mod, gpre, wqkv, wg, wab, conv, aneg, dtb, *, seq, tm):
    t, d = x.shape
    hb = tm // GDN_HALO
    nblk8 = t // GDN_HALO
    n_conv = wqkv.shape[1]
    return pl.pallas_call(
        functools.partial(_gdn_proj_kernel, tiles_per_seq=seq // tm),
        grid=(t // tm,),
        in_specs=[
            pl.BlockSpec((GDN_HALO, d), lambda i: (jnp.maximum(i * hb - 1, 0), 0)),
            pl.BlockSpec((tm, d), lambda i: (i, 0)),
            pl.BlockSpec((GDN_HALO, d), lambda i: (jnp.minimum((i + 1) * hb, nblk8 - 1), 0)),
            pl.BlockSpec((1, 3 * N_SUB, d), lambda i: ((i * tm) // seq, 0, 0)),
            _const_spec((1, d)),
            _const_spec(wqkv.shape),
            _const_spec(wg.shape),
            _const_spec(wab.shape),
            _const_spec(conv.shape),
            _const_spec(aneg.shape),
            _const_spec(dtb.shape),
        ],
        out_specs=[
            pl.BlockSpec((tm, GDN_QK), lambda i: (i, 0)),
            pl.BlockSpec((tm, GDN_QK), lambda i: (i, 0)),
            pl.BlockSpec((tm, GDN_VW), lambda i: (i, 0)),
            pl.BlockSpec((tm, GDN_VW), lambda i: (i, 0)),
            pl.BlockSpec((tm, LANES), lambda i: (i, 0)),
        ],
        out_shape=[
            jax.ShapeDtypeStruct((t, GDN_QK), F32),
            jax.ShapeDtypeStruct((t, GDN_QK), F32),
            jax.ShapeDtypeStruct((t, GDN_VW), F32),
            jax.ShapeDtypeStruct((t, GDN_VW), F32),
            jax.ShapeDtypeStruct((t, LANES), F32),
        ],
        scratch_shapes=[pltpu.VMEM((n_conv // LANES, tm + 2 * GDN_HALO, LANES), F32)],
        compiler_params=_cparams(("parallel",)),
        name="gdn_proj",
    )(x, x, x, mod, gpre, wqkv, wg, wab, conv, aneg, dtb)


def _gdn_cumsum_kernel(gb_ref, o_ref):
    g = gb_ref[...]
    tm = g.shape[0]
    row = lax.broadcasted_iota(jnp.int32, (tm, tm), 0)
    col = lax.broadcasted_iota(jnp.int32, (tm, tm), 1)
    sh = int(math.log2(GDN_CHUNK))
    same = lax.shift_right_logical(row, sh) == lax.shift_right_logical(col, sh)
    lower = jnp.where(same & (col <= row), 1.0, 0.0).astype(BF16)
    upper = jnp.where(same & (col >= row), 1.0, 0.0).astype(BF16)
    g1 = g.astype(BF16)
    r1 = g - g1.astype(F32)
    g2 = r1.astype(BF16)
    g3 = (r1 - g2.astype(F32)).astype(BF16)
    pre = _dot(lower, g1) + _dot(lower, g2) + _dot(lower, g3)
    suf = _dot(upper, g1) + _dot(upper, g2) + _dot(upper, g3)
    lane = lax.broadcasted_iota(jnp.int32, g.shape, 1)
    o_ref[...] = jnp.where(lane < GDN_HEADS, pre, jnp.where(lane < 2 * GDN_HEADS, suf, g))


def _gdn_cumsum(gb, *, tm):
    t = gb.shape[0]
    return pl.pallas_call(
        _gdn_cumsum_kernel,
        grid=(t // tm,),
        in_specs=[pl.BlockSpec((tm, LANES), lambda i: (i, 0))],
        out_specs=pl.BlockSpec((tm, LANES), lambda i: (i, 0)),
        out_shape=jax.ShapeDtypeStruct((t, LANES), F32),
        compiler_params=_cparams(("parallel",)),
        name="gdn_cumsum",
    )(gb)


def _gdn_delta_kernel(q_ref, k_ref, v_ref, gcol_ref, bcol_ref, grow_ref, o_ref, state_ref,
                      *, reverse):
    c = GDN_CHUNK
    nchunk = q_ref.shape[0] // c

    @pl.when(pl.program_id(1) == 0)
    def _():
        state_ref[...] = jnp.zeros_like(state_ref)

    row = lax.broadcasted_iota(jnp.int32, (c, c), 0)
    col = lax.broadcasted_iota(jnp.int32, (c, c), 1)
    if reverse:
        row, col = col, row
    tri = row >= col
    strict = row > col
    eye = jnp.where(row == col, 1.0, 0.0)
    level_masks = []
    for lv in range(int(math.log2(c))):
        rb = lax.shift_right_logical(row, lv)
        cb = lax.shift_right_logical(col, lv)
        level_masks.append((lax.shift_right_logical(rb, 1) == lax.shift_right_logical(cb, 1))
                           & ((rb & 1) == 1) & ((cb & 1) == 0))
    last_row = 0 if reverse else c - 1

    heads = range(GDN_HEADS)
    order = list(reversed(range(nchunk))) if reverse else list(range(nchunk))
    probs = [(j, h) for j in order for h in heads]
    rows = lambda j: slice(j * c, (j + 1) * c)
    cols = lambda h: slice(h * GDN_DK, (h + 1) * GDN_DK)

    qs = [q_ref[rows(j), cols(h)] for j, h in probs]
    ks = [k_ref[rows(j), cols(h)] for j, h in probs]
    vs = [v_ref[rows(j), cols(h)] for j, h in probs]
    gcbs = [jnp.broadcast_to(gcol_ref[0, rows(j), h:h + 1], (c, GDN_DK)) for j, h in probs]
    betas = [jnp.broadcast_to(bcol_ref[0, rows(j), h:h + 1], (c, GDN_DK)) for j, h in probs]
    n = len(probs)
    decays = []
    for p, (j, h) in enumerate(probs):
        grb = jnp.broadcast_to(grow_ref[0, j, h:h + 1, :], (c, c))
        decays.append(jnp.where(tri, jnp.exp(jnp.where(tri, gcbs[p][:, :c] - grb, 0.0)), 0.0))
    kbs = [ks[p] * betas[p] for p in range(n)]
    a2s = [_dot_nt(jnp.concatenate([kbs[p], qs[p]], axis=0).astype(BF16), ks[p].astype(BF16))
           for p in range(n)]
    ms = [jnp.where(strict, a2s[p][:c] * decays[p], 0.0) for p in range(n)]
    qks = [(a2s[p][c:] * decays[p]).astype(BF16) for p in range(n)]
    tinvs = [eye - jnp.where(level_masks[0], ms[p], 0.0) for p in range(n)]
    for lm in level_masks[1:]:
        tbs = [t.astype(BF16) for t in tinvs]
        tcs = [_dot(tbs[p], jnp.where(lm, ms[p], 0.0).astype(BF16)).astype(BF16) for p in range(n)]
        tinvs = [tinvs[p] - _dot(tcs[p], tbs[p]) for p in range(n)]
    egs = [jnp.exp(g) for g in gcbs]
    sols = [_dot(tinvs[p].astype(BF16),
                 jnp.concatenate([vs[p] * betas[p], kbs[p] * egs[p]], axis=1).astype(BF16))
            for p in range(n)]
    g_lasts = [g[last_row:last_row + 1, :] for g in gcbs]
    k_decs = [(ks[p] * jnp.exp(g_lasts[p] - gcbs[p])).astype(BF16) for p in range(n)]
    wq_lhs = [jnp.concatenate([sols[p][:, GDN_DV:], qs[p] * egs[p]], axis=0).astype(BF16)
              for p in range(n)]

    sts = [state_ref[h] for h in heads]
    for step, j in enumerate(order):
        ps = [step * GDN_HEADS + h for h in heads]
        wqs = [_dot(wq_lhs[ps[h]], sts[h].astype(BF16)) for h in heads]
        vnbs = [(sols[ps[h]][:, :GDN_DV] - wqs[h][:c]).astype(BF16) for h in heads]
        for h in heads:
            o_ref[rows(j), cols(h)] = wqs[h][c:] + _dot(qks[ps[h]], vnbs[h])
        sts = [sts[h] * jnp.exp(g_lasts[ps[h]]) + _dot_tn(k_decs[ps[h]], vnbs[h]) for h in heads]
    for h in heads:
        state_ref[h] = sts[h]


def _gdn_delta(q, k, v, gcol, bcol, grow, *, batch, seq, tb, reverse):
    t = q.shape[0]
    nb = seq // tb
    cpb = tb // GDN_CHUNK

    def blk(bi, i):
        return bi * nb + ((nb - 1 - i) if reverse else i)

    def seq_blk(bi, i):
        return (nb - 1 - i) if reverse else i

    return pl.pallas_call(
        functools.partial(_gdn_delta_kernel, reverse=reverse),
        grid=(batch, nb),
        in_specs=[
            pl.BlockSpec((tb, GDN_QK), lambda bi, i: (blk(bi, i), 0)),
            pl.BlockSpec((tb, GDN_QK), lambda bi, i: (blk(bi, i), 0)),
            pl.BlockSpec((tb, GDN_VW), lambda bi, i: (blk(bi, i), 0)),
            pl.BlockSpec((1, tb, GDN_HEADS), lambda bi, i: (bi, seq_blk(bi, i), 0)),
            pl.BlockSpec((1, tb, GDN_HEADS), lambda bi, i: (bi, seq_blk(bi, i), 0)),
            pl.BlockSpec((1, cpb, GDN_HEADS, GDN_CHUNK), lambda bi, i: (bi, seq_blk(bi, i), 0, 0)),
        ],
        out_specs=pl.BlockSpec((tb, GDN_VW), lambda bi, i: (blk(bi, i), 0)),
        out_shape=jax.ShapeDtypeStruct((t, GDN_VW), F32),
        scratch_shapes=[pltpu.VMEM((GDN_HEADS, GDN_DK, GDN_DV), F32)],
        compiler_params=_cparams(("parallel", "arbitrary")),
        name="gdn_delta_bw" if reverse else "gdn_delta_fw",
    )(q, k, v, gcol, bcol, grow)


def _gdn_out_kernel(of_ref, ob_ref, gate_ref, x_ref, mod_ref, gpost_ref, onorm_ref, w_ref, o_ref,
                    a_ref):
    o = of_ref[...] + ob_ref[...]
    gate = gate_ref[...]
    og = gate * jax.nn.sigmoid(gate)
    for h in range(GDN_HEADS):
        cs = slice(h * GDN_DV, (h + 1) * GDN_DV)
        a_ref[:, cs] = (_rms(o[:, cs], onorm_ref[...]) * og[:, cs]).astype(BF16)
    out = _dot(a_ref[...], w_ref[...])
    o_ref[...] = _residual(x_ref[...], out, mod_ref, gpost_ref, 1, 1.0)


def _gdn_out(o_fw, o_bw, gate, x, mod, gpost, onorm, w, *, seq, tm):
    t, d = x.shape
    return pl.pallas_call(
        _gdn_out_kernel,
        grid=(t // tm,),
        in_specs=[
            pl.BlockSpec((tm, GDN_VW), lambda i: (i, 0)),
            pl.BlockSpec((tm, GDN_VW), lambda i: (i, 0)),
            pl.BlockSpec((tm, GDN_VW), lambda i: (i, 0)),
            pl.BlockSpec((tm, d), lambda i: (i, 0)),
            pl.BlockSpec((1, 3 * N_SUB, d), lambda i: ((i * tm) // seq, 0, 0)),
            _const_spec((1, d)),
            _const_spec(onorm.shape),
            _const_spec(w.shape),
        ],
        out_specs=pl.BlockSpec((tm, d), lambda i: (i, 0)),
        out_shape=jax.ShapeDtypeStruct((t, d), F32),
        scratch_shapes=[pltpu.VMEM((tm, GDN_VW), BF16)],
        compiler_params=_cparams(("parallel",)),
        name="gdn_out",
    )(o_fw, o_bw, gate, x, mod, gpost, onorm, w)


def _fnet_chan_kernel(x_ref, mod_ref, gpre_ref, cs_ref, o_ref):
    hb = _modulated(x_ref[...], mod_ref, gpre_ref, 1).astype(BF16)
    cg = hb.shape[1] // FNET_GROUPS
    for g in range(FNET_GROUPS):
        ab = _dot(hb[:, g * cg:(g + 1) * cg], cs_ref[...])
        o_ref[0, 0, :, g * cg:(g + 1) * cg] = ab[:, :cg]
        o_ref[0, 1, :, g * cg:(g + 1) * cg] = ab[:, cg:]


def _fnet_chan(x, mod, gpre, cs, *, batch, seq, tm):
    t, d = x.shape
    nt = seq // tm
    return pl.pallas_call(
        _fnet_chan_kernel,
        grid=(t // tm,),
        in_specs=[
            pl.BlockSpec((tm, d), lambda i: (i, 0)),
            pl.BlockSpec((1, 3 * N_SUB, d), lambda i: ((i * tm) // seq, 0, 0)),
            _const_spec((1, d)),
            _const_spec(cs.shape),
        ],
        out_specs=pl.BlockSpec((1, 2, tm, d), lambda i: (i // nt, 0, i % nt, 0)),
        out_shape=jax.ShapeDtypeStruct((batch, 2, seq, d), F32),
        compiler_params=_cparams(("parallel",)),
        name="fnet_chan",
    )(x, mod, gpre, cs)


def _fnet_s1_kernel(m1_ref, ab_ref, y_ref):
    _, two, s1, _, rows, d = ab_ref.shape
    ab = ab_ref[0].reshape(two * s1 * rows, d).astype(BF16)
    y = _dot(m1_ref[...], ab)
    y_ref[0] = y.reshape(two, s1, 1, rows, d)


def _fnet_s1(m1, ab):
    batch, _, s1, groups, rows, d = ab.shape
    blk = (1, 2, s1, 1, rows, d)
    return pl.pallas_call(
        _fnet_s1_kernel,
        grid=(batch, groups),
        in_specs=[
            _const_spec(m1.shape),
            pl.BlockSpec(blk, lambda bi, g: (bi, 0, 0, g, 0, 0)),
        ],
        out_specs=pl.BlockSpec(blk, lambda bi, g: (bi, 0, 0, g, 0, 0)),
        out_shape=jax.ShapeDtypeStruct(ab.shape, F32),
        compiler_params=_cparams(("parallel", "parallel")),
        name="fnet_s1",
    )(m1, ab)


def _fnet_s2_kernel(m2_ref, tw_ref, y_ref, f_ref, z_ref):
    n, d = y_ref.shape[2:]
    cos = tw_ref[0]
    sin = tw_ref[1]
    for c0 in range(0, d, LANES):
        yr = y_ref[0, 0, :, c0:c0 + LANES]
        yi = y_ref[0, 1, :, c0:c0 + LANES]
        z_ref[:n, c0:c0 + LANES] = (yr * cos + yi * sin).astype(BF16)
        z_ref[n:, c0:c0 + LANES] = (yi * cos - yr * sin).astype(BF16)
    f = _dot(m2_ref[...], z_ref[...])
    f_ref[0] = f.reshape(f_ref.shape[1:])


def _fnet_s2(m2, tw, y, *, s1, s2):
    batch, _, _, d = y.shape
    n = 8 * s2
    return pl.pallas_call(
        _fnet_s2_kernel,
        grid=(batch, s1 // 8),
        in_specs=[
            _const_spec(m2.shape),
            pl.BlockSpec((2, n, LANES), lambda bi, a: (0, a, 0)),
            pl.BlockSpec((1, 2, n, d), lambda bi, a: (bi, 0, a, 0)),
        ],
        out_specs=pl.BlockSpec((1, s2, 1, 8, d), lambda bi, a: (bi, 0, a, 0, 0)),
        out_shape=jax.ShapeDtypeStruct((batch, s2, s1 // 8, 8, d), F32),
        scratch_shapes=[pltpu.VMEM((2 * n, d), BF16)],
        compiler_params=_cparams(("parallel", "parallel")),
        name="fnet_s2",
    )(m2, tw, y)


def _rope_table(seq):
    half = MLA_ROPE // 2
    pos = jnp.arange(seq, dtype=F32)
    inv_freq = ROPE_THETA ** (-jnp.arange(half, dtype=F32) / half)
    ang = pos[:, None] * inv_freq[None, :]
    cos, sin = jnp.cos(ang), jnp.sin(ang)
    return jnp.concatenate([cos, cos, -sin, sin], axis=1)


def _swap_halves(w):
    half = w.shape[-1] // 2
    return jnp.concatenate([w[..., half:], w[..., :half]], axis=-1)


def _prep_mla(w_down, w_uq, w_ukv):
    d = w_down.shape[0]
    rope = w_down[:, MLA_Q_LORA + MLA_KV_LORA:]
    wd = jnp.concatenate([w_down, _swap_halves(rope)], axis=1).astype(BF16)
    uq = w_uq.reshape(MLA_Q_LORA, MLA_HEADS, MLA_NOPE + MLA_ROPE)
    uq_rope = uq[..., MLA_NOPE:]
    wq = jnp.concatenate([uq, _swap_halves(uq_rope)], axis=-1)
    wq = wq.reshape(MLA_Q_LORA, MLA_HEADS * MLA_QK_PAD).astype(BF16)
    ukv = w_ukv.reshape(MLA_KV_LORA, MLA_HEADS, MLA_NOPE + MLA_V)
    wkv = jnp.concatenate([ukv[..., :MLA_NOPE].reshape(MLA_KV_LORA, -1),
                           ukv[..., MLA_NOPE:].reshape(MLA_KV_LORA, -1)], axis=1).astype(BF16)
    del d
    return wd, wq, wkv


def _fnet_tables(seq, cg):
    def trig(n):
        idx = jnp.arange(n, dtype=jnp.int32)
        ang = ((idx[:, None] * idx[None, :]) % n).astype(F32) * (2.0 * math.pi / n)
        return jnp.cos(ang), jnp.sin(ang)

    cc, sc = trig(cg)
    cs = (jnp.concatenate([cc, sc], axis=1) * (cg ** -0.5)).astype(BF16)
    s2 = min(seq, LANES)
    s1 = seq // s2
    eye8 = jnp.eye(8, dtype=F32)
    c1, sn1 = trig(s1)
    m1 = jnp.block([[c1, -sn1], [-sn1, -c1]]) * (s1 ** -0.5)
    m1 = jnp.kron(m1, eye8).astype(BF16)
    c2, sn2 = trig(s2)
    m2 = jnp.stack([c2, sn2]) * (s2 ** -0.5)
    m2 = jnp.einsum('cts,rq->trcqs', m2, eye8).reshape(8 * s2, 16 * s2).astype(BF16)
    t1 = jnp.arange(s1, dtype=jnp.int32)[:, None]
    k2 = jnp.arange(s2, dtype=jnp.int32)[None, :]
    ang = ((t1 * k2).astype(F32) * (2.0 * math.pi / seq)).reshape(seq)
    tw = jnp.stack([jnp.cos(ang), jnp.sin(ang)])
    tw = jnp.broadcast_to(tw[..., None], (2, seq, LANES))
    return cs, m1, m2, tw


def _pick(n, pref):
    return pref if n % pref == 0 else n


def _trunk(x3, mods, prm):
    batch, seq, d = x3.shape
    t = batch * seq
    x = x3.reshape(t, d)
    tm = _pick(seq, 512)
    row = lambda v: v.reshape(1, -1)
    trig = _rope_table(seq)
    for l in range(mods.shape[0]):
        i = l // 3
        kind = l % 3
        mod = mods[l]
        gpre = prm['norm_pre'][l]
        gpost = prm['norm_post'][l]
        x = _ffn(x, mod, row(gpre[0]), row(gpost[0]), prm['ffn_w_in'][l][0], prm['ffn_w_out'][l][0],
                 sub=0, seq=seq, tm=tm)
        if kind == 0:
            wd, wq, wkv = prm['mla'][i]
            qt, k, vt = _mla_proj(x, mod, row(gpre[1]), trig, wd, row(prm['mla_q_norm'][i]),
                                  row(prm['mla_kv_norm'][i]), wq, wkv, seq=seq, tm=tm)
            o = _flash(qt, k.reshape(batch, seq, -1), vt, tq=_pick(seq, 512))
            x = _out_proj(o.reshape(t, -1), x, mod, row(gpost[1]), prm['mla_w_out'][i],
                          jnp.zeros((1, d), F32), seq=seq, tm=tm)
        elif kind == 1:
            g = prm['gdn'][i]
            q, k, v, gate, gb = _gdn_proj(x, mod, row(gpre[1]), g['wqkv'], g['wg'], g['wab'], g['conv'],
                                          g['aneg'], g['dtb'], seq=seq, tm=_pick(seq, 256))
            gc = _gdn_cumsum(gb, tm=tm)
            nh = GDN_HEADS
            gc3 = gc.reshape(batch, seq, LANES)
            outs = []
            for dr in range(2):
                gcol = gc3[:, :, dr * nh:(dr + 1) * nh]
                bcol = gc3[:, :, 2 * nh + dr * nh:2 * nh + (dr + 1) * nh]
                grow = jnp.swapaxes(gcol.reshape(batch, seq // GDN_CHUNK, GDN_CHUNK, nh), 2, 3)
                outs.append(_gdn_delta(q, k, v, gcol, bcol, grow, batch=batch, seq=seq,
                                       tb=_pick(seq, 256), reverse=bool(dr)))
            x = _gdn_out(outs[0], outs[1], gate, x, mod, row(gpost[1]), row(prm['gdn_o_norm'][i]),
                         g['wout'], seq=seq, tm=tm)
        else:
            cs, m1, m2, tw = _fnet_tables(seq, d // FNET_GROUPS)
            s2 = min(seq, LANES)
            s1 = seq // s2
            ab = _fnet_chan(x, mod, row(gpre[1]), cs, batch=batch, seq=seq, tm=tm)
            y = _fnet_s1(m1, ab.reshape(batch, 2, s1, s2 // 8, 8, d))
            f = _fnet_s2(m2, tw, y.reshape(batch, 2, seq, d), s1=s1, s2=s2)
            x = _out_proj(f.reshape(t, d), x, mod, row(gpost[1]), prm['fnet_w_out'][i],
                          row(prm['fnet_b_out'][i]), seq=seq, tm=tm)
        x = _ffn(x, mod, row(gpre[2]), row(gpost[2]), prm['ffn_w_in'][l][1], prm['ffn_w_out'][l][1],
                 sub=2, seq=seq, tm=tm)
    return x.reshape(batch, seq, d)


def _prep_gdn(w_in, conv, a_log, dt_bias, w_out):
    n_conv = 2 * GDN_QK + GDN_VW
    wab = jnp.pad(w_in[:, n_conv + GDN_VW:], ((0, 0), (0, LANES - 4 * GDN_HEADS)))
    pad16 = lambda v: jnp.pad(v.reshape(1, -1).astype(F32), ((0, 0), (0, LANES - 2 * GDN_HEADS)))
    return {
        'wqkv': w_in[:, :n_conv].astype(BF16),
        'wg': w_in[:, n_conv:n_conv + GDN_VW].astype(BF16),
        'wab': wab.astype(BF16),
        'conv': conv.astype(F32),
        'aneg': pad16(-jnp.exp(a_log.astype(F32))),
        'dtb': pad16(dt_bias),
        'wout': w_out.astype(BF16),
    }


def kernel(x_prompt, x_sample, c_prompt, c_sample, w_ada, b_ada, norm_pre, norm_post, ffn_w_in, ffn_w_out, mla_w_down, mla_q_norm, mla_kv_norm, mla_w_uq, mla_w_ukv, mla_w_out, gdn_w_in, gdn_conv, gdn_a_log, gdn_dt_bias, gdn_o_norm, gdn_w_out, fnet_w_out, fnet_b_out):
    d = x_prompt.shape[-1]
    prm = {
        'norm_pre': norm_pre, 'norm_post': norm_post,
        'ffn_w_in': ffn_w_in.astype(BF16), 'ffn_w_out': ffn_w_out.astype(BF16),
        'mla': [_prep_mla(mla_w_down[i], mla_w_uq[i], mla_w_ukv[i]) for i in range(mla_w_down.shape[0])],
        'mla_q_norm': mla_q_norm, 'mla_kv_norm': mla_kv_norm, 'mla_w_out': mla_w_out.astype(BF16),
        'gdn': [_prep_gdn(gdn_w_in[i], gdn_conv[i], gdn_a_log[i], gdn_dt_bias[i], gdn_w_out[i])
                for i in range(gdn_w_in.shape[0])],
        'gdn_o_norm': gdn_o_norm,
        'fnet_w_out': fnet_w_out.astype(BF16), 'fnet_b_out': fnet_b_out,
    }
    nbp = c_prompt.shape[0]
    c_all = jnp.concatenate([c_prompt, c_sample], axis=0)
    mods = _ada_mod(c_all, w_ada, b_ada).reshape(w_ada.shape[0], c_all.shape[0], 3 * N_SUB, d)
    y_prompt = _trunk(x_prompt, mods[:, :nbp], prm)
    y_sample = _trunk(x_sample, mods[:, nbp:], prm)
    return (y_prompt, y_sample)
```

```python
import functools
import math

import jax
import jax.numpy as jnp
from jax import lax
from jax.experimental import pallas as pl
from jax.experimental.pallas import tpu as pltpu

F32 = jnp.float32
BF16 = jnp.bfloat16

N_SUB = 3
D_FF = 2816
NORM_EPS = 1e-6

MLA_HEADS = 8
MLA_Q_LORA = 384
MLA_KV_LORA = 256
MLA_NOPE = 128
MLA_ROPE = 64
MLA_V = 128
ROPE_THETA = 10000.0
MLA_SCALE = (MLA_NOPE + MLA_ROPE) ** -0.5
MLA_QK_PAD = 256
MLA_V_PAD = MLA_V + 16

GDN_HEADS = 8
GDN_DK = 128
GDN_DV = 128
GDN_QK = GDN_HEADS * GDN_DK
GDN_VW = GDN_HEADS * GDN_DV
GDN_CONV = 5
GDN_CHUNK = 64
GDN_HALO = 8

FNET_GROUPS = 8

LANES = 128
VMEM_LIMIT = 56 * 1024 * 1024


def _cparams(sem):
    return pltpu.CompilerParams(dimension_semantics=sem, vmem_limit_bytes=VMEM_LIMIT)


def _const_spec(shape):
    nd = len(shape)
    return pl.BlockSpec(shape, lambda *_: (0,) * nd, pipeline_mode=pl.Buffered(1))


def _rms(x, g):
    ms = jnp.mean(x * x, axis=-1, keepdims=True)
    return x * lax.rsqrt(ms + NORM_EPS) * g


def _modulated(x, mod_ref, gpre_ref, sub):
    shift = mod_ref[0, 3 * sub:3 * sub + 1, :]
    scale = mod_ref[0, 3 * sub + 1:3 * sub + 2, :]
    return _rms(x, gpre_ref[...]) * (1.0 + scale) + shift


def _residual(x, out, mod_ref, gpost_ref, sub, weight):
    gate = mod_ref[0, 3 * sub + 2:3 * sub + 3, :]
    return x + (weight * gate) * _rms(out, gpost_ref[...])


def _dot(a, b):
    return jnp.dot(a, b, preferred_element_type=F32)


def _dot_nt(a, b):
    return lax.dot_general(a, b, (((1,), (1,)), ((), ())), preferred_element_type=F32)


def _dot_tn(a, b):
    return lax.dot_general(a, b, (((0,), (0,)), ((), ())), preferred_element_type=F32)


def _ada_kernel(c_ref, w_ref, b_ref, o_ref):
    c = c_ref[...]
    sc = (c * jax.nn.sigmoid(c)).astype(BF16)
    o_ref[0] = _dot(sc, w_ref[0].astype(BF16)) + b_ref[0]


def _ada_mod(c_all, w_ada, b_ada):
    nb, d = c_all.shape
    depth, _, n_out = w_ada.shape
    tn = n_out // 8
    return pl.pallas_call(
        _ada_kernel,
        grid=(depth, n_out // tn),
        in_specs=[
            pl.BlockSpec((nb, d), lambda l, j: (0, 0)),
            pl.BlockSpec((1, d, tn), lambda l, j: (l, 0, j)),
            pl.BlockSpec((1, 1, tn), lambda l, j: (l, 0, j)),
        ],
        out_specs=pl.BlockSpec((1, nb, tn), lambda l, j: (l, 0, j)),
        out_shape=jax.ShapeDtypeStruct((depth, nb, n_out), F32),
        compiler_params=_cparams(("arbitrary", "arbitrary")),
        name="ada_mod",
    )(c_all, w_ada, b_ada.reshape(depth, 1, n_out))


def _ffn_kernel(x_ref, mod_ref, gpre_ref, gpost_ref, win_ref, wout_ref, o_ref, act_ref,
                *, sub, ck):
    x = x_ref[...]
    hb = _modulated(x, mod_ref, gpre_ref, sub).astype(BF16)
    for c in range(D_FF // ck):
        g = _dot(hb, win_ref[:, c * ck:(c + 1) * ck])
        u = _dot(hb, win_ref[:, D_FF + c * ck:D_FF + (c + 1) * ck])
        act_ref[:, c * ck:(c + 1) * ck] = (g * jax.nn.sigmoid(g) * u).astype(BF16)
    out = _dot(act_ref[...], wout_ref[...])
    o_ref[...] = _residual(x, out, mod_ref, gpost_ref, sub, 0.5)


def _ffn(x, mod, gpre, gpost, w_in, w_out, *, sub, seq, tm):
    t, d = x.shape
    return pl.pallas_call(
        functools.partial(_ffn_kernel, sub=sub, ck=256),
        grid=(t // tm,),
        in_specs=[
            pl.BlockSpec((tm, d), lambda i: (i, 0)),
            pl.BlockSpec((1, 3 * N_SUB, d), lambda i: ((i * tm) // seq, 0, 0)),
            _const_spec((1, d)),
            _const_spec((1, d)),
            _const_spec(w_in.shape),
            _const_spec(w_out.shape),
        ],
        out_specs=pl.BlockSpec((tm, d), lambda i: (i, 0)),
        out_shape=jax.ShapeDtypeStruct((t, d), F32),
        scratch_shapes=[pltpu.VMEM((tm, D_FF), BF16)],
        compiler_params=_cparams(("parallel",)),
        name="ffn",
    )(x, mod, gpre, gpost, w_in, w_out)


def _rope_pair(t):
    return t + pltpu.roll(t, 64, axis=1)


def _mla_proj_kernel(x_ref, mod_ref, gpre_ref, trig_ref, wd_ref, qn_ref, kvn_ref, wq_ref, wkv_ref,
                     qt_ref, k_ref, vt_ref):
    x = x_ref[...]
    hb = _modulated(x, mod_ref, gpre_ref, 1).astype(BF16)
    down = _dot(hb, wd_ref[...])
    cq = _rms(down[:, :MLA_Q_LORA], qn_ref[...]).astype(BF16)
    ckv = _rms(down[:, MLA_Q_LORA:MLA_Q_LORA + MLA_KV_LORA], kvn_ref[...]).astype(BF16)
    trig = trig_ref[...]
    lane = lax.broadcasted_iota(jnp.int32, trig.shape, 1)
    k_rope = _rope_pair(down[:, MLA_Q_LORA + MLA_KV_LORA:] * trig)
    k_rope = jnp.where(lane < MLA_ROPE, k_rope, 0.0).astype(BF16)
    q = _dot(cq, wq_ref[...])
    kv = _dot(ckv, wkv_ref[...])
    qs = MLA_SCALE * math.log2(math.e)
    for h in range(MLA_HEADS):
        c0 = h * MLA_QK_PAD
        q_rope = _rope_pair(q[:, c0 + LANES:c0 + 2 * LANES] * trig)
        qt_ref[0, h, 0, :LANES, :] = (q[:, c0:c0 + LANES] * qs).T.astype(BF16)
        qt_ref[0, h, 0, LANES:, :] = (q_rope * qs).T.astype(BF16)
        k_ref[:, c0:c0 + LANES] = kv[:, h * MLA_NOPE:(h + 1) * MLA_NOPE].astype(BF16)
        k_ref[:, c0 + LANES:c0 + 2 * LANES] = k_rope
        v0 = MLA_HEADS * MLA_NOPE + h * MLA_V
        vt_ref[0, h, 0, :MLA_V, :] = kv[:, v0:v0 + MLA_V].T.astype(BF16)
        pad_row = lax.broadcasted_iota(jnp.int32, (MLA_V_PAD - MLA_V, x.shape[0]), 0)
        vt_ref[0, h, 0, MLA_V:, :] = jnp.where(pad_row == 0, 1.0, 0.0).astype(BF16)


def _mla_proj(x, mod, gpre, trig, wd, qn, kvn, wq, wkv, *, seq, tm):
    t, d = x.shape
    nq = MLA_HEADS * MLA_QK_PAD
    nt = seq // tm
    return pl.pallas_call(
        _mla_proj_kernel,
        grid=(t // tm,),
        in_specs=[
            pl.BlockSpec((tm, d), lambda i: (i, 0)),
            pl.BlockSpec((1, 3 * N_SUB, d), lambda i: ((i * tm) // seq, 0, 0)),
            _const_spec((1, d)),
            pl.BlockSpec((tm, LANES), lambda i: (i % (seq // tm), 0)),
            _const_spec(wd.shape),
            _const_spec(qn.shape),
            _const_spec(kvn.shape),
            _const_spec(wq.shape),
            _const_spec(wkv.shape),
        ],
        out_specs=[
            pl.BlockSpec((1, MLA_HEADS, 1, MLA_QK_PAD, tm), lambda i: (i // nt, 0, i % nt, 0, 0)),
            pl.BlockSpec((tm, nq), lambda i: (i, 0)),
            pl.BlockSpec((1, MLA_HEADS, 1, MLA_V_PAD, tm), lambda i: (i // nt, 0, i % nt, 0, 0)),
        ],
        out_shape=[
            jax.ShapeDtypeStruct((t // seq, MLA_HEADS, nt, MLA_QK_PAD, tm), BF16),
            jax.ShapeDtypeStruct((t, nq), BF16),
            jax.ShapeDtypeStruct((t // seq, MLA_HEADS, nt, MLA_V_PAD, tm), BF16),
        ],
        compiler_params=_cparams(("parallel",)),
        name="mla_proj",
    )(x, mod, gpre, trig, wd, qn, kvn, wq, wkv)


def _flash_kernel(qt_ref, k_ref, vt_ref, o_ref, s_ref):
    nq, _, tq = qt_ref.shape[2:]
    nk, dvp, tk = vt_ref.shape[2:]
    unroll = next((u for u in (8, 4, 2) if nk % u == 0 and nk >= 2 * u), 1)
    cmax0 = jnp.full((8, tq), -1e30, F32)
    acc0 = jnp.zeros((dvp, tq), F32)

    def score_part(j, slot, qt, cmax):
        r0 = pl.multiple_of(j * tk, tk)
        s = _dot(k_ref[0, pl.ds(r0, tk), :], qt)
        s_ref[slot, j] = s
        return jnp.maximum(cmax, jnp.max(s.reshape(tk // 8, 8, tq), axis=0))

    def value_part(j, slot, m, acc):
        p = jnp.exp2((s_ref[slot, j] - m).astype(BF16))
        return acc + _dot(vt_ref[0, 0, j], p)

    def scores_only(i, slot):
        qt = qt_ref[0, 0, i]
        cmax = lax.fori_loop(0, nk, lambda j, c: score_part(j, slot, qt, c), cmax0, unroll=unroll)
        return jnp.max(cmax, axis=0, keepdims=True)

    def values_only(slot, m):
        return lax.fori_loop(0, nk, lambda j, a: value_part(j, slot, m, a), acc0, unroll=unroll)

    def both(i_next, cur, m):
        qt = qt_ref[0, 0, i_next]

        def body(j, carry):
            cmax, acc = carry
            return score_part(j, 1 - cur, qt, cmax), value_part(j, cur, m, acc)

        cmax, acc = lax.fori_loop(0, nk, body, (cmax0, acc0), unroll=unroll)
        return jnp.max(cmax, axis=0, keepdims=True), acc

    def emit(i, acc):
        o_t = acc[:MLA_V] / acc[MLA_V:MLA_V + 1]
        o_ref[0, pl.ds(pl.multiple_of(i * tq, tq), tq), :] = o_t.T.astype(o_ref.dtype)

    m = scores_only(0, 0)
    if nq == 1:
        emit(0, values_only(0, m))
        return
    assert nq % 2 == 0

    def pair(ip, m):
        i = 2 * ip
        m_odd, acc = both(i + 1, 0, m)
        emit(i, acc)
        m_even, acc = both(i + 2, 1, m_odd)
        emit(i + 1, acc)
        return m_even

    m = lax.fori_loop(0, nq // 2 - 1, pair, m)
    m_last, acc = both(nq - 1, 0, m)
    emit(nq - 2, acc)
    emit(nq - 1, values_only(1, m_last))


def _flash(qt, k, vt):
    b, s, _ = k.shape
    _, _, nq, _, tq = qt.shape
    _, _, nk, dv, tk = vt.shape
    whole = lambda shape: pl.BlockSpec(shape, lambda bi, h: (bi, h) + (0,) * (len(shape) - 2),
                                       pipeline_mode=pl.Buffered(1))
    return pl.pallas_call(
        _flash_kernel,
        grid=(b, MLA_HEADS),
        in_specs=[
            whole((1, 1, nq, MLA_QK_PAD, tq)),
            pl.BlockSpec((1, s, MLA_QK_PAD), lambda bi, h: (bi, 0, h), pipeline_mode=pl.Buffered(1)),
            whole((1, 1, nk, dv, tk)),
        ],
        out_specs=pl.BlockSpec((1, s, MLA_V), lambda bi, h: (bi, 0, h)),
        out_shape=jax.ShapeDtypeStruct((b, s, MLA_HEADS * MLA_V), BF16),
        scratch_shapes=[pltpu.VMEM((2, nk, tk, tq), F32)],
        compiler_params=_cparams(("parallel", "parallel")),
        name="mla_flash",
    )(qt, k, vt)


def _out_proj_kernel(a_ref, x_ref, mod_ref, gpost_ref, w_ref, b_ref, o_ref):
    out = _dot(a_ref[...].astype(BF16), w_ref[...]) + b_ref[...]
    o_ref[...] = _residual(x_ref[...], out, mod_ref, gpost_ref, 1, 1.0)


def _out_proj(a, x, mod, gpost, w, b, *, seq, tm):
    t, d = x.shape
    return pl.pallas_call(
        _out_proj_kernel,
        grid=(t // tm,),
        in_specs=[
            pl.BlockSpec((tm, a.shape[1]), lambda i: (i, 0)),
            pl.BlockSpec((tm, d), lambda i: (i, 0)),
            pl.BlockSpec((1, 3 * N_SUB, d), lambda i: ((i * tm) // seq, 0, 0)),
            _const_spec((1, d)),
            _const_spec(w.shape),
            _const_spec((1, d)),
        ],
        out_specs=pl.BlockSpec((tm, d), lambda i: (i, 0)),
        out_shape=jax.ShapeDtypeStruct((t, d), F32),
        compiler_params=_cparams(("parallel",)),
        name="out_proj",
    )(a, x, mod, gpost, w, b)


def _gdn_proj_kernel(xp_ref, x_ref, xn_ref, mod_ref, gpre_ref, wqkv_ref, wg_ref, wab_ref, conv_ref,
                     aneg_ref, dtb_ref, q_ref, k_ref, v_ref, gate_ref, gb_ref, p_ref,
                     *, tiles_per_seq):
    i = pl.program_id(0)
    tm = x_ref.shape[0]
    first = (i % tiles_per_seq) == 0
    last = (i % tiles_per_seq) == tiles_per_seq - 1

    def proj_rows(xr):
        return _dot(_modulated(xr, mod_ref, gpre_ref, 1).astype(BF16), wqkv_ref[...])

    hb = _modulated(x_ref[...], mod_ref, gpre_ref, 1).astype(BF16)
    p_prev = jnp.where(first, 0.0, proj_rows(xp_ref[...]))
    p_main = _dot(hb, wqkv_ref[...])
    p_next = jnp.where(last, 0.0, proj_rows(xn_ref[...]))
    n_slab = p_ref.shape[0]
    for c in range(n_slab):
        cs = slice(c * LANES, (c + 1) * LANES)
        p_ref[c, 0:GDN_HALO, :] = p_prev[:, cs]
        p_ref[c, GDN_HALO:GDN_HALO + tm, :] = p_main[:, cs]
        p_ref[c, GDN_HALO + tm:, :] = p_next[:, cs]

    def l2n(z):
        return z * lax.rsqrt(jnp.sum(z * z, axis=-1, keepdims=True) + NORM_EPS)

    pad = GDN_CONV // 2
    for c in range(n_slab):
        cs = slice(c * LANES, (c + 1) * LANES)
        acc = None
        for tap in range(GDN_CONV):
            r0 = GDN_HALO - pad + tap
            term = p_ref[c, r0:r0 + tm, :] * conv_ref[tap:tap + 1, cs]
            acc = term if acc is None else acc + term
        z = acc * jax.nn.sigmoid(acc)
        if c < GDN_HEADS:
            q_ref[:, cs] = l2n(z) * (GDN_DK ** -0.5)
        elif c < 2 * GDN_HEADS:
            k_ref[:, c * LANES - GDN_QK:(c + 1) * LANES - GDN_QK] = l2n(z)
        else:
            v_ref[:, c * LANES - 2 * GDN_QK:(c + 1) * LANES - 2 * GDN_QK] = z
    gate_ref[...] = _dot(hb, wg_ref[...])
    ab = _dot(hb, wab_ref[...])
    z = ab + dtb_ref[...]
    softplus = jnp.maximum(z, 0.0) + jnp.log(1.0 + jnp.exp(-jnp.abs(z)))
    lane = lax.broadcasted_iota(jnp.int32, ab.shape, 1)
    gb_ref[...] = jnp.where(lane < 2 * GDN_HEADS, aneg_ref[...] * softplus, jax.nn.sigmoid(ab))


def _gdn_proj(x, mod, gpre, wqkv, wg, wab, conv, aneg, dtb, *, seq, tm):
    t, d = x.shape
    hb = tm // GDN_HALO
    nblk8 = t // GDN_HALO
    n_conv = wqkv.shape[1]
    return pl.pallas_call(
        functools.partial(_gdn_proj_kernel, tiles_per_seq=seq // tm),
        grid=(t // tm,),
        in_specs=[
            pl.BlockSpec((GDN_HALO, d), lambda i: (jnp.maximum(i * hb - 1, 0), 0)),
            pl.BlockSpec((tm, d), lambda i: (i, 0)),
            pl.BlockSpec((GDN_HALO, d), lambda i: (jnp.minimum((i + 1) * hb, nblk8 - 1), 0)),
            pl.BlockSpec((1, 3 * N_SUB, d), lambda i: ((i * tm) // seq, 0, 0)),
            _const_spec((1, d)),
            _const_spec(wqkv.shape),
            _const_spec(wg.shape),
            _const_spec(wab.shape),
            _const_spec(conv.shape),
            _const_spec(aneg.shape),
            _const_spec(dtb.shape),
        ],
        out_specs=[
            pl.BlockSpec((tm, GDN_QK), lambda i: (i, 0)),
            pl.BlockSpec((tm, GDN_QK), lambda i: (i, 0)),
            pl.BlockSpec((tm, GDN_VW), lambda i: (i, 0)),
            pl.BlockSpec((tm, GDN_VW), lambda i: (i, 0)),
            pl.BlockSpec((tm, LANES), lambda i: (i, 0)),
        ],
        out_shape=[
            jax.ShapeDtypeStruct((t, GDN_QK), F32),
            jax.ShapeDtypeStruct((t, GDN_QK), F32),
            jax.ShapeDtypeStruct((t, GDN_VW), F32),
            jax.ShapeDtypeStruct((t, GDN_VW), F32),
            jax.ShapeDtypeStruct((t, LANES), F32),
        ],
        scratch_shapes=[pltpu.VMEM((n_conv // LANES, tm + 2 * GDN_HALO, LANES), F32)],
        compiler_params=_cparams(("parallel",)),
        name="gdn_proj",
    )(x, x, x, mod, gpre, wqkv, wg, wab, conv, aneg, dtb)


def _gdn_cumsum_kernel(gb_ref, o_ref):
    g = gb_ref[...]
    tm = g.shape[0]
    row = lax.broadcasted_iota(jnp.int32, (tm, tm), 0)
    col = lax.broadcasted_iota(jnp.int32, (tm, tm), 1)
    sh = int(math.log2(GDN_CHUNK))
    same = lax.shift_right_logical(row, sh) == lax.shift_right_logical(col, sh)
    lower = jnp.where(same & (col <= row), 1.0, 0.0).astype(BF16)
    upper = jnp.where(same & (col >= row), 1.0, 0.0).astype(BF16)
    g1 = g.astype(BF16)
    r1 = g - g1.astype(F32)
    g2 = r1.astype(BF16)
    g3 = (r1 - g2.astype(F32)).astype(BF16)
    pre = _dot(lower, g1) + _dot(lower, g2) + _dot(lower, g3)
    suf = _dot(upper, g1) + _dot(upper, g2) + _dot(upper, g3)
    lane = lax.broadcasted_iota(jnp.int32, g.shape, 1)
    o_ref[...] = jnp.where(lane < GDN_HEADS, pre, jnp.where(lane < 2 * GDN_HEADS, suf, g))


def _gdn_cumsum(gb, *, tm):
    t = gb.shape[0]
    return pl.pallas_call(
        _gdn_cumsum_kernel,
        grid=(t // tm,),
        in_specs=[pl.BlockSpec((tm, LANES), lambda i: (i, 0))],
        out_specs=pl.BlockSpec((tm, LANES), lambda i: (i, 0)),
        out_shape=jax.ShapeDtypeStruct((t, LANES), F32),
        compiler_params=_cparams(("parallel",)),
        name="gdn_cumsum",
    )(gb)


def _gdn_delta_kernel(q_ref, k_ref, v_ref, gcol_ref, bcol_ref, grow_ref, o_ref, state_ref,
                      *, reverse):
    c = GDN_CHUNK
    nchunk = q_ref.shape[0] // c

    @pl.when(pl.program_id(1) == 0)
    def _():
        state_ref[...] = jnp.zeros_like(state_ref)

    row = lax.broadcasted_iota(jnp.int32, (c, c), 0)
    col = lax.broadcasted_iota(jnp.int32, (c, c), 1)
    if reverse:
        row, col = col, row
    tri = row >= col
    strict = row > col
    eye = jnp.where(row == col, 1.0, 0.0)
    level_masks = []
    for lv in range(int(math.log2(c))):
        rb = lax.shift_right_logical(row, lv)
        cb = lax.shift_right_logical(col, lv)
        level_masks.append((lax.shift_right_logical(rb, 1) == lax.shift_right_logical(cb, 1))
                           & ((rb & 1) == 1) & ((cb & 1) == 0))
    last_row = 0 if reverse else c - 1

    heads = range(GDN_HEADS)
    order = list(reversed(range(nchunk))) if reverse else list(range(nchunk))
    probs = [(j, h) for j in order for h in heads]
    rows = lambda j: slice(j * c, (j + 1) * c)
    cols = lambda h: slice(h * GDN_DK, (h + 1) * GDN_DK)

    qs = [q_ref[rows(j), cols(h)] for j, h in probs]
    ks = [k_ref[rows(j), cols(h)] for j, h in probs]
    vs = [v_ref[rows(j), cols(h)] for j, h in probs]
    gcbs = [jnp.broadcast_to(gcol_ref[0, rows(j), h:h + 1], (c, GDN_DK)) for j, h in probs]
    betas = [jnp.broadcast_to(bcol_ref[0, rows(j), h:h + 1], (c, GDN_DK)) for j, h in probs]
    n = len(probs)
    decays = []
    for p, (j, h) in enumerate(probs):
        grb = jnp.broadcast_to(grow_ref[0, j, h:h + 1, :], (c, c))
        decays.append(jnp.where(tri, jnp.exp(jnp.where(tri, gcbs[p][:, :c] - grb, 0.0)), 0.0))
    kbs = [ks[p] * betas[p] for p in range(n)]
    a2s = [_dot_nt(jnp.concatenate([kbs[p], qs[p]], axis=0).astype(BF16), ks[p].astype(BF16))
           for p in range(n)]
    ms = [jnp.where(strict, a2s[p][:c] * decays[p], 0.0) for p in range(n)]
    qks = [(a2s[p][c:] * decays[p]).astype(BF16) for p in range(n)]
    tinvs = [eye - jnp.where(level_masks[0], ms[p], 0.0) for p in range(n)]
    for lm in level_masks[1:]:
        tbs = [t.astype(BF16) for t in tinvs]
        tcs = [_dot(tbs[p], jnp.where(lm, ms[p], 0.0).astype(BF16)).astype(BF16) for p in range(n)]
        tinvs = [tinvs[p] - _dot(tcs[p], tbs[p]) for p in range(n)]
    egs = [jnp.exp(g) for g in gcbs]
    sols = [_dot(tinvs[p].astype(BF16),
                 jnp.concatenate([vs[p] * betas[p], kbs[p] * egs[p]], axis=1).astype(BF16))
            for p in range(n)]
    g_lasts = [g[last_row:last_row + 1, :] for g in gcbs]
    k_decs = [(ks[p] * jnp.exp(g_lasts[p] - gcbs[p])).astype(BF16) for p in range(n)]
    wq_lhs = [jnp.concatenate([sols[p][:, GDN_DV:], qs[p] * egs[p]], axis=0).astype(BF16)
              for p in range(n)]

    sts = [state_ref[h] for h in heads]
    for step, j in enumerate(order):
        ps = [step * GDN_HEADS + h for h in heads]
        wqs = [_dot(wq_lhs[ps[h]], sts[h].astype(BF16)) for h in heads]
        vnbs = [(sols[ps[h]][:, :GDN_DV] - wqs[h][:c]).astype(BF16) for h in heads]
        for h in heads:
            o_ref[rows(j), cols(h)] = wqs[h][c:] + _dot(qks[ps[h]], vnbs[h])
        sts = [sts[h] * jnp.exp(g_lasts[ps[h]]) + _dot_tn(k_decs[ps[h]], vnbs[h]) for h in heads]
    for h in heads:
        state_ref[h] = sts[h]


def _gdn_delta(q, k, v, gcol, bcol, grow, *, batch, seq, tb, reverse):
    t = q.shape[0]
    nb = seq // tb
    cpb = tb // GDN_CHUNK

    def blk(bi, i):
        return bi * nb + ((nb - 1 - i) if reverse else i)

    def seq_blk(bi, i):
        return (nb - 1 - i) if reverse else i

    return pl.pallas_call(
        functools.partial(_gdn_delta_kernel, reverse=reverse),
        grid=(batch, nb),
        in_specs=[
            pl.BlockSpec((tb, GDN_QK), lambda bi, i: (blk(bi, i), 0)),
            pl.BlockSpec((tb, GDN_QK), lambda bi, i: (blk(bi, i), 0)),
            pl.BlockSpec((tb, GDN_VW), lambda bi, i: (blk(bi, i), 0)),
            pl.BlockSpec((1, tb, GDN_HEADS), lambda bi, i: (bi, seq_blk(bi, i), 0)),
            pl.BlockSpec((1, tb, GDN_HEADS), lambda bi, i: (bi, seq_blk(bi, i), 0)),
            pl.BlockSpec((1, cpb, GDN_HEADS, GDN_CHUNK), lambda bi, i: (bi, seq_blk(bi, i), 0, 0)),
        ],
        out_specs=pl.BlockSpec((tb, GDN_VW), lambda bi, i: (blk(bi, i), 0)),
        out_shape=jax.ShapeDtypeStruct((t, GDN_VW), F32),
        scratch_shapes=[pltpu.VMEM((GDN_HEADS, GDN_DK, GDN_DV), F32)],
        compiler_params=_cparams(("parallel", "arbitrary")),
        name="gdn_delta_bw" if reverse else "gdn_delta_fw",
    )(q, k, v, gcol, bcol, grow)


def _gdn_out_kernel(of_ref, ob_ref, gate_ref, x_ref, mod_ref, gpost_ref, onorm_ref, w_ref, o_ref,
                    a_ref):
    o = of_ref[...] + ob_ref[...]
    gate = gate_ref[...]
    og = gate * jax.nn.sigmoid(gate)
    for h in range(GDN_HEADS):
        cs = slice(h * GDN_DV, (h + 1) * GDN_DV)
        a_ref[:, cs] = (_rms(o[:, cs], onorm_ref[...]) * og[:, cs]).astype(BF16)
    out = _dot(a_ref[...], w_ref[...])
    o_ref[...] = _residual(x_ref[...], out, mod_ref, gpost_ref, 1, 1.0)


def _gdn_out(o_fw, o_bw, gate, x, mod, gpost, onorm, w, *, seq, tm):
    t, d = x.shape
    return pl.pallas_call(
        _gdn_out_kernel,
        grid=(t // tm,),
        in_specs=[
            pl.BlockSpec((tm, GDN_VW), lambda i: (i, 0)),
            pl.BlockSpec((tm, GDN_VW), lambda i: (i, 0)),
            pl.BlockSpec((tm, GDN_VW), lambda i: (i, 0)),
            pl.BlockSpec((tm, d), lambda i: (i, 0)),
            pl.BlockSpec((1, 3 * N_SUB, d), lambda i: ((i * tm) // seq, 0, 0)),
            _const_spec((1, d)),
            _const_spec(onorm.shape),
            _const_spec(w.shape),
        ],
        out_specs=pl.BlockSpec((tm, d), lambda i: (i, 0)),
        out_shape=jax.ShapeDtypeStruct((t, d), F32),
        scratch_shapes=[pltpu.VMEM((tm, GDN_VW), BF16)],
        compiler_params=_cparams(("parallel",)),
        name="gdn_out",
    )(o_fw, o_bw, gate, x, mod, gpost, onorm, w)


def _fnet_chan_kernel(x_ref, mod_ref, gpre_ref, cs_ref, o_ref):
    hb = _modulated(x_ref[...], mod_ref, gpre_ref, 1).astype(BF16)
    cg = hb.shape[1] // FNET_GROUPS
    for g in range(FNET_GROUPS):
        ab = _dot(hb[:, g * cg:(g + 1) * cg], cs_ref[...])
        o_ref[0, 0, :, g * cg:(g + 1) * cg] = ab[:, :cg]
        o_ref[0, 1, :, g * cg:(g + 1) * cg] = ab[:, cg:]


def _fnet_chan(x, mod, gpre, cs, *, batch, seq, tm):
    t, d = x.shape
    nt = seq // tm
    return pl.pallas_call(
        _fnet_chan_kernel,
        grid=(t // tm,),
        in_specs=[
            pl.BlockSpec((tm, d), lambda i: (i, 0)),
            pl.BlockSpec((1, 3 * N_SUB, d), lambda i: ((i * tm) // seq, 0, 0)),
            _const_spec((1, d)),
            _const_spec(cs.shape),
        ],
        out_specs=pl.BlockSpec((1, 2, tm, d), lambda i: (i // nt, 0, i % nt, 0)),
        out_shape=jax.ShapeDtypeStruct((batch, 2, seq, d), F32),
        compiler_params=_cparams(("parallel",)),
        name="fnet_chan",
    )(x, mod, gpre, cs)


def _fnet_s1_kernel(m1_ref, ab_ref, y_ref):
    _, two, s1, _, rows, d = ab_ref.shape
    ab = ab_ref[0].reshape(two * s1 * rows, d).astype(BF16)
    y = _dot(m1_ref[...], ab)
    y_ref[0] = y.reshape(two, s1, 1, rows, d)


def _fnet_s1(m1, ab):
    batch, _, s1, groups, rows, d = ab.shape
    blk = (1, 2, s1, 1, rows, d)
    return pl.pallas_call(
        _fnet_s1_kernel,
        grid=(batch, groups),
        in_specs=[
            _const_spec(m1.shape),
            pl.BlockSpec(blk, lambda bi, g: (bi, 0, 0, g, 0, 0)),
        ],
        out_specs=pl.BlockSpec(blk, lambda bi, g: (bi, 0, 0, g, 0, 0)),
        out_shape=jax.ShapeDtypeStruct(ab.shape, F32),
        compiler_params=_cparams(("parallel", "parallel")),
        name="fnet_s1",
    )(m1, ab)


def _fnet_s2_kernel(m2_ref, tw_ref, y_ref, f_ref, z_ref):
    n, d = y_ref.shape[2:]
    cos = tw_ref[0]
    sin = tw_ref[1]
    for c0 in range(0, d, LANES):
        yr = y_ref[0, 0, :, c0:c0 + LANES]
        yi = y_ref[0, 1, :, c0:c0 + LANES]
        z_ref[:n, c0:c0 + LANES] = (yr * cos + yi * sin).astype(BF16)
        z_ref[n:, c0:c0 + LANES] = (yi * cos - yr * sin).astype(BF16)
    f = _dot(m2_ref[...], z_ref[...])
    f_ref[0] = f.reshape(f_ref.shape[1:])


def _fnet_s2(m2, tw, y, *, s1, s2):
    batch, _, _, d = y.shape
    n = 8 * s2
    return pl.pallas_call(
        _fnet_s2_kernel,
        grid=(batch, s1 // 8),
        in_specs=[
            _const_spec(m2.shape),
            pl.BlockSpec((2, n, LANES), lambda bi, a: (0, a, 0)),
            pl.BlockSpec((1, 2, n, d), lambda bi, a: (bi, 0, a, 0)),
        ],
        out_specs=pl.BlockSpec((1, s2, 1, 8, d), lambda bi, a: (bi, 0, a, 0, 0)),
        out_shape=jax.ShapeDtypeStruct((batch, s2, s1 // 8, 8, d), F32),
        scratch_shapes=[pltpu.VMEM((2 * n, d), BF16)],
        compiler_params=_cparams(("parallel", "parallel")),
        name="fnet_s2",
    )(m2, tw, y)


def _rope_table(seq):
    half = MLA_ROPE // 2
    pos = jnp.arange(seq, dtype=F32)
    inv_freq = ROPE_THETA ** (-jnp.arange(half, dtype=F32) / half)
    ang = pos[:, None] * inv_freq[None, :]
    cos, sin = jnp.cos(ang), jnp.sin(ang)
    return jnp.concatenate([cos, cos, -sin, sin], axis=1)


def _swap_halves(w):
    half = w.shape[-1] // 2
    return jnp.concatenate([w[..., half:], w[..., :half]], axis=-1)


def _prep_mla(w_down, w_uq, w_ukv):
    d = w_down.shape[0]
    rope = w_down[:, MLA_Q_LORA + MLA_KV_LORA:]
    wd = jnp.concatenate([w_down, _swap_halves(rope)], axis=1).astype(BF16)
    uq = w_uq.reshape(MLA_Q_LORA, MLA_HEADS, MLA_NOPE + MLA_ROPE)
    uq_rope = uq[..., MLA_NOPE:]
    wq = jnp.concatenate([uq, _swap_halves(uq_rope)], axis=-1)
    wq = wq.reshape(MLA_Q_LORA, MLA_HEADS * MLA_QK_PAD).astype(BF16)
    ukv = w_ukv.reshape(MLA_KV_LORA, MLA_HEADS, MLA_NOPE + MLA_V)
    wkv = jnp.concatenate([ukv[..., :MLA_NOPE].reshape(MLA_KV_LORA, -1),
                           ukv[..., MLA_NOPE:].reshape(MLA_KV_LORA, -1)], axis=1).astype(BF16)
    del d
    return wd, wq, wkv


def _fnet_tables(seq, cg):
    def trig(n):
        idx = jnp.arange(n, dtype=jnp.int32)
        ang = ((idx[:, None] * idx[None, :]) % n).astype(F32) * (2.0 * math.pi / n)
        return jnp.cos(ang), jnp.sin(ang)

    cc, sc = trig(cg)
    cs = (jnp.concatenate([cc, sc], axis=1) * (cg ** -0.5)).astype(BF16)
    s2 = min(seq, LANES)
    s1 = seq // s2
    eye8 = jnp.eye(8, dtype=F32)
    c1, sn1 = trig(s1)
    m1 = jnp.block([[c1, -sn1], [-sn1, -c1]]) * (s1 ** -0.5)
    m1 = jnp.kron(m1, eye8).astype(BF16)
    c2, sn2 = trig(s2)
    m2 = jnp.stack([c2, sn2]) * (s2 ** -0.5)
    m2 = jnp.einsum('cts,rq->trcqs', m2, eye8).reshape(8 * s2, 16 * s2).astype(BF16)
    t1 = jnp.arange(s1, dtype=jnp.int32)[:, None]
    k2 = jnp.arange(s2, dtype=jnp.int32)[None, :]
    ang = ((t1 * k2).astype(F32) * (2.0 * math.pi / seq)).reshape(seq)
    tw = jnp.stack([jnp.cos(ang), jnp.sin(ang)])
    tw = jnp.broadcast_to(tw[..., None], (2, seq, LANES))
    return cs, m1, m2, tw


def _pick(n, pref):
    return pref if n % pref == 0 else n


def _trunk(x3, mods, prm):
    batch, seq, d = x3.shape
    t = batch * seq
    x = x3.reshape(t, d)
    tm = _pick(seq, 512)
    row = lambda v: v.reshape(1, -1)
    trig = _rope_table(seq)
    for l in range(mods.shape[0]):
        i = l // 3
        kind = l % 3
        mod = mods[l]
        gpre = prm['norm_pre'][l]
        gpost = prm['norm_post'][l]
        x = _ffn(x, mod, row(gpre[0]), row(gpost[0]), prm['ffn_w_in'][l][0], prm['ffn_w_out'][l][0],
                 sub=0, seq=seq, tm=tm)
        if kind == 0:
            wd, wq, wkv = prm['mla'][i]
            qt, k, vt = _mla_proj(x, mod, row(gpre[1]), trig, wd, row(prm['mla_q_norm'][i]),
                                  row(prm['mla_kv_norm'][i]), wq, wkv, seq=seq, tm=tm)
            o = _flash(qt, k.reshape(batch, seq, -1), vt)
            x = _out_proj(o.reshape(t, -1), x, mod, row(gpost[1]), prm['mla_w_out'][i],
                          jnp.zeros((1, d), F32), seq=seq, tm=tm)
        elif kind == 1:
            g = prm['gdn'][i]
            q, k, v, gate, gb = _gdn_proj(x, mod, row(gpre[1]), g['wqkv'], g['wg'], g['wab'], g['conv'],
                                          g['aneg'], g['dtb'], seq=seq, tm=_pick(seq, 256))
            gc = _gdn_cumsum(gb, tm=tm)
            nh = GDN_HEADS
            gc3 = gc.reshape(batch, seq, LANES)
            outs = []
            for dr in range(2):
                gcol = gc3[:, :, dr * nh:(dr + 1) * nh]
                bcol = gc3[:, :, 2 * nh + dr * nh:2 * nh + (dr + 1) * nh]
                grow = jnp.swapaxes(gcol.reshape(batch, seq // GDN_CHUNK, GDN_CHUNK, nh), 2, 3)
                outs.append(_gdn_delta(q, k, v, gcol, bcol, grow, batch=batch, seq=seq,
                                       tb=_pick(seq, 256), reverse=bool(dr)))
            x = _gdn_out(outs[0], outs[1], gate, x, mod, row(gpost[1]), row(prm['gdn_o_norm'][i]),
                         g['wout'], seq=seq, tm=tm)
        else:
            cs, m1, m2, tw = _fnet_tables(seq, d // FNET_GROUPS)
            s2 = min(seq, LANES)
            s1 = seq // s2
            ab = _fnet_chan(x, mod, row(gpre[1]), cs, batch=batch, seq=seq, tm=tm)
            y = _fnet_s1(m1, ab.reshape(batch, 2, s1, s2 // 8, 8, d))
            f = _fnet_s2(m2, tw, y.reshape(batch, 2, seq, d), s1=s1, s2=s2)
            x = _out_proj(f.reshape(t, d), x, mod, row(gpost[1]), prm['fnet_w_out'][i],
                          row(prm['fnet_b_out'][i]), seq=seq, tm=tm)
        x = _ffn(x, mod, row(gpre[2]), row(gpost[2]), prm['ffn_w_in'][l][1], prm['ffn_w_out'][l][1],
                 sub=2, seq=seq, tm=tm)
    return x.reshape(batch, seq, d)


def _prep_gdn(w_in, conv, a_log, dt_bias, w_out):
    n_conv = 2 * GDN_QK + GDN_VW
    wab = jnp.pad(w_in[:, n_conv + GDN_VW:], ((0, 0), (0, LANES - 4 * GDN_HEADS)))
    pad16 = lambda v: jnp.pad(v.reshape(1, -1).astype(F32), ((0, 0), (0, LANES - 2 * GDN_HEADS)))
    return {
        'wqkv': w_in[:, :n_conv].astype(BF16),
        'wg': w_in[:, n_conv:n_conv + GDN_VW].astype(BF16),
        'wab': wab.astype(BF16),
        'conv': conv.astype(F32),
        'aneg': pad16(-jnp.exp(a_log.astype(F32))),
        'dtb': pad16(dt_bias),
        'wout': w_out.astype(BF16),
    }


def kernel(x_prompt, x_sample, c_prompt, c_sample, w_ada, b_ada, norm_pre, norm_post, ffn_w_in, ffn_w_out, mla_w_down, mla_q_norm, mla_kv_norm, mla_w_uq, mla_w_ukv, mla_w_out, gdn_w_in, gdn_conv, gdn_a_log, gdn_dt_bias, gdn_o_norm, gdn_w_out, fnet_w_out, fnet_b_out):
    d = x_prompt.shape[-1]
    prm = {
        'norm_pre': norm_pre, 'norm_post': norm_post,
        'ffn_w_in': ffn_w_in.astype(BF16), 'ffn_w_out': ffn_w_out.astype(BF16),
        'mla': [_prep_mla(mla_w_down[i], mla_w_uq[i], mla_w_ukv[i]) for i in range(mla_w_down.shape[0])],
        'mla_q_norm': mla_q_norm, 'mla_kv_norm': mla_kv_norm, 'mla_w_out': mla_w_out.astype(BF16),
        'gdn': [_prep_gdn(gdn_w_in[i], gdn_conv[i], gdn_a_log[i], gdn_dt_bias[i], gdn_w_out[i])
                for i in range(gdn_w_in.shape[0])],
        'gdn_o_norm': gdn_o_norm,
        'fnet_w_out': fnet_w_out.astype(BF16), 'fnet_b_out': fnet_b_out,
    }
    nbp = c_prompt.shape[0]
    c_all = jnp.concatenate([c_prompt, c_sample], axis=0)
    mods = _ada_mod(c_all, w_ada, b_ada).reshape(w_ada.shape[0], c_all.shape[0], 3 * N_SUB, d)
    y_prompt = _trunk(x_prompt, mods[:, :nbp], prm)
    y_sample = _trunk(x_sample, mods[:, nbp:], prm)
    return (y_prompt, y_sample)
```

```python
import functools
import math

import jax
import jax.numpy as jnp
from jax import lax
from jax.experimental import pallas as pl
from jax.experimental.pallas import tpu as pltpu

F32 = jnp.float32
BF16 = jnp.bfloat16

N_SUB = 3
D_FF = 2816
NORM_EPS = 1e-6

MLA_HEADS = 8
MLA_Q_LORA = 384
MLA_KV_LORA = 256
MLA_NOPE = 128
MLA_ROPE = 64
MLA_V = 128
ROPE_THETA = 10000.0
MLA_SCALE = (MLA_NOPE + MLA_ROPE) ** -0.5
MLA_QK_PAD = 256
MLA_V_PAD = MLA_V + 16

GDN_HEADS = 8
GDN_DK = 128
GDN_DV = 128
GDN_QK = GDN_HEADS * GDN_DK
GDN_VW = GDN_HEADS * GDN_DV
GDN_CONV = 5
GDN_CHUNK = 64
GDN_HALO = 8

FNET_GROUPS = 8

LANES = 128
VMEM_LIMIT = 56 * 1024 * 1024


def _cparams(sem):
    return pltpu.CompilerParams(dimension_semantics=sem, vmem_limit_bytes=VMEM_LIMIT)


def _const_spec(shape):
    nd = len(shape)
    return pl.BlockSpec(shape, lambda *_: (0,) * nd, pipeline_mode=pl.Buffered(1))


def _rms(x, g):
    ms = jnp.mean(x * x, axis=-1, keepdims=True)
    return x * lax.rsqrt(ms + NORM_EPS) * g


def _modulated(x, mod_ref, gpre_ref, sub):
    shift = mod_ref[0, 3 * sub:3 * sub + 1, :]
    scale = mod_ref[0, 3 * sub + 1:3 * sub + 2, :]
    return _rms(x, gpre_ref[...]) * (1.0 + scale) + shift


def _residual(x, out, mod_ref, gpost_ref, sub, weight):
    gate = mod_ref[0, 3 * sub + 2:3 * sub + 3, :]
    return x + (weight * gate) * _rms(out, gpost_ref[...])


def _dot(a, b):
    return jnp.dot(a, b, preferred_element_type=F32)


def _dot_nt(a, b):
    return lax.dot_general(a, b, (((1,), (1,)), ((), ())), preferred_element_type=F32)


def _dot_tn(a, b):
    return lax.dot_general(a, b, (((0,), (0,)), ((), ())), preferred_element_type=F32)


def _ada_kernel(c_ref, w_ref, b_ref, o_ref):
    c = c_ref[...]
    sc = (c * jax.nn.sigmoid(c)).astype(BF16)
    o_ref[0] = _dot(sc, w_ref[0].astype(BF16)) + b_ref[0]


def _ada_mod(c_all, w_ada, b_ada):
    nb, d = c_all.shape
    depth, _, n_out = w_ada.shape
    tn = n_out // 8
    return pl.pallas_call(
        _ada_kernel,
        grid=(depth, n_out // tn),
        in_specs=[
            pl.BlockSpec((nb, d), lambda l, j: (0, 0)),
            pl.BlockSpec((1, d, tn), lambda l, j: (l, 0, j)),
            pl.BlockSpec((1, 1, tn), lambda l, j: (l, 0, j)),
        ],
        out_specs=pl.BlockSpec((1, nb, tn), lambda l, j: (l, 0, j)),
        out_shape=jax.ShapeDtypeStruct((depth, nb, n_out), F32),
        compiler_params=_cparams(("arbitrary", "arbitrary")),
        name="ada_mod",
    )(c_all, w_ada, b_ada.reshape(depth, 1, n_out))


FFN_CHUNK = 256


FFN_SPLIT = 2


def _ffn_step(xs, mod_ref, gpre_ref, gpost_ref, win_ref, wout_ref, act_ref, sub):
    n = xs[0].shape[0]
    hbs = [_modulated(x, mod_ref, gpre_ref, sub).astype(BF16) for x in xs]
    for c0 in range(0, D_FF, FFN_CHUNK):
        for r, hb in enumerate(hbs):
            g = _dot(hb, win_ref[:, c0:c0 + FFN_CHUNK])
            u = _dot(hb, win_ref[:, D_FF + c0:D_FF + c0 + FFN_CHUNK])
            act_ref[r * n:(r + 1) * n, c0:c0 + FFN_CHUNK] = (g * jax.nn.sigmoid(g) * u).astype(BF16)
    outs = [_dot(act_ref[r * n:(r + 1) * n, :], wout_ref[...]) for r in range(len(xs))]
    return [_residual(x, out, mod_ref, gpost_ref, sub, 0.5) for x, out in zip(xs, outs)]


def _row_groups(ref):
    n = ref.shape[0] // FFN_SPLIT
    return [ref[r * n:(r + 1) * n, :] for r in range(FFN_SPLIT)]


def _store_row_groups(ref, vals):
    n = ref.shape[0] // FFN_SPLIT
    for r, v in enumerate(vals):
        ref[r * n:(r + 1) * n, :] = v


def _ffn_kernel(x_ref, mod_ref, gpre_ref, gpost_ref, win_ref, wout_ref, o_ref, act_ref, *, sub):
    _store_row_groups(o_ref, _ffn_step(_row_groups(x_ref), mod_ref, gpre_ref, gpost_ref,
                                        win_ref, wout_ref, act_ref, sub))


def _mix_ffn_kernel(a_ref, x_ref, mod_ref, gpost_mix_ref, wmix_ref, bmix_ref,
                    gpre_ref, gpost_ref, win_ref, wout_ref, o_ref, act_ref):
    xs = [_residual(x, _dot(a.astype(BF16), wmix_ref[...]) + bmix_ref[...], mod_ref, gpost_mix_ref, 1, 1.0)
          for a, x in zip(_row_groups(a_ref), _row_groups(x_ref))]
    _store_row_groups(o_ref, _ffn_step(xs, mod_ref, gpre_ref, gpost_ref, win_ref, wout_ref, act_ref, 2))


def _mix_ffn(a, x, mod, gpost_mix, w_mix, b_mix, gpre, gpost, w_in, w_out, *, seq, tm):
    t, d = x.shape
    return pl.pallas_call(
        _mix_ffn_kernel,
        grid=(t // tm,),
        in_specs=[
            pl.BlockSpec((tm, a.shape[1]), lambda i: (i, 0)),
            pl.BlockSpec((tm, d), lambda i: (i, 0)),
            pl.BlockSpec((1, 3 * N_SUB, d), lambda i: ((i * tm) // seq, 0, 0)),
            _const_spec((1, d)),
            _const_spec(w_mix.shape),
            _const_spec((1, d)),
            _const_spec((1, d)),
            _const_spec((1, d)),
            _const_spec(w_in.shape),
            _const_spec(w_out.shape),
        ],
        out_specs=pl.BlockSpec((tm, d), lambda i: (i, 0)),
        out_shape=jax.ShapeDtypeStruct((t, d), F32),
        scratch_shapes=[pltpu.VMEM((tm, D_FF), BF16)],
        compiler_params=_cparams(("parallel",)),
        name="mix_ffn",
    )(a, x, mod, gpost_mix, w_mix, b_mix, gpre, gpost, w_in, w_out)


def _ffn(x, mod, gpre, gpost, w_in, w_out, *, sub, seq, tm):
    t, d = x.shape
    return pl.pallas_call(
        functools.partial(_ffn_kernel, sub=sub),
        grid=(t // tm,),
        in_specs=[
            pl.BlockSpec((tm, d), lambda i: (i, 0)),
            pl.BlockSpec((1, 3 * N_SUB, d), lambda i: ((i * tm) // seq, 0, 0)),
            _const_spec((1, d)),
            _const_spec((1, d)),
            _const_spec(w_in.shape),
            _const_spec(w_out.shape),
        ],
        out_specs=pl.BlockSpec((tm, d), lambda i: (i, 0)),
        out_shape=jax.ShapeDtypeStruct((t, d), F32),
        scratch_shapes=[pltpu.VMEM((tm, D_FF), BF16)],
        compiler_params=_cparams(("parallel",)),
        name="ffn",
    )(x, mod, gpre, gpost, w_in, w_out)


def _rope_pair(t):
    return t + pltpu.roll(t, 64, axis=1)


def _mla_proj_kernel(x_ref, mod_ref, gpre_ref, trig_ref, wd_ref, qn_ref, kvn_ref, wq_ref, wkv_ref,
                     qt_ref, k_ref, vt_ref):
    x = x_ref[...]
    hb = _modulated(x, mod_ref, gpre_ref, 1).astype(BF16)
    down = _dot(hb, wd_ref[...])
    cq = _rms(down[:, :MLA_Q_LORA], qn_ref[...]).astype(BF16)
    ckv = _rms(down[:, MLA_Q_LORA:MLA_Q_LORA + MLA_KV_LORA], kvn_ref[...]).astype(BF16)
    trig = trig_ref[...]
    lane = lax.broadcasted_iota(jnp.int32, trig.shape, 1)
    k_rope = _rope_pair(down[:, MLA_Q_LORA + MLA_KV_LORA:] * trig)
    k_rope = jnp.where(lane < MLA_ROPE, k_rope, 0.0).astype(BF16)
    q = _dot(cq, wq_ref[...])
    kv = _dot(ckv, wkv_ref[...])
    qs = MLA_SCALE * math.log2(math.e)
    for h in range(MLA_HEADS):
        c0 = h * MLA_QK_PAD
        q_rope = _rope_pair(q[:, c0 + LANES:c0 + 2 * LANES] * trig)
        qt_ref[0, h, 0, :LANES, :] = (q[:, c0:c0 + LANES] * qs).T.astype(BF16)
        qt_ref[0, h, 0, LANES:, :] = (q_rope * qs).T.astype(BF16)
        k_ref[:, c0:c0 + LANES] = kv[:, h * MLA_NOPE:(h + 1) * MLA_NOPE].astype(BF16)
        k_ref[:, c0 + LANES:c0 + 2 * LANES] = k_rope
        v0 = MLA_HEADS * MLA_NOPE + h * MLA_V
        vt_ref[0, h, 0, :MLA_V, :] = kv[:, v0:v0 + MLA_V].T.astype(BF16)
        pad_row = lax.broadcasted_iota(jnp.int32, (MLA_V_PAD - MLA_V, x.shape[0]), 0)
        vt_ref[0, h, 0, MLA_V:, :] = jnp.where(pad_row == 0, 1.0, 0.0).astype(BF16)


def _mla_proj(x, mod, gpre, trig, wd, qn, kvn, wq, wkv, *, seq, tm):
    t, d = x.shape
    nq = MLA_HEADS * MLA_QK_PAD
    nt = seq // tm
    return pl.pallas_call(
        _mla_proj_kernel,
        grid=(t // tm,),
        in_specs=[
            pl.BlockSpec((tm, d), lambda i: (i, 0)),
            pl.BlockSpec((1, 3 * N_SUB, d), lambda i: ((i * tm) // seq, 0, 0)),
            _const_spec((1, d)),
            pl.BlockSpec((tm, LANES), lambda i: (i % (seq // tm), 0)),
            _const_spec(wd.shape),
            _const_spec(qn.shape),
            _const_spec(kvn.shape),
            _const_spec(wq.shape),
            _const_spec(wkv.shape),
        ],
        out_specs=[
            pl.BlockSpec((1, MLA_HEADS, 1, MLA_QK_PAD, tm), lambda i: (i // nt, 0, i % nt, 0, 0)),
            pl.BlockSpec((tm, nq), lambda i: (i, 0)),
            pl.BlockSpec((1, MLA_HEADS, 1, MLA_V_PAD, tm), lambda i: (i // nt, 0, i % nt, 0, 0)),
        ],
        out_shape=[
            jax.ShapeDtypeStruct((t // seq, MLA_HEADS, nt, MLA_QK_PAD, tm), BF16),
            jax.ShapeDtypeStruct((t, nq), BF16),
            jax.ShapeDtypeStruct((t // seq, MLA_HEADS, nt, MLA_V_PAD, tm), BF16),
        ],
        compiler_params=_cparams(("parallel",)),
        name="mla_proj",
    )(x, mod, gpre, trig, wd, qn, kvn, wq, wkv)


def _flash_kernel(qt_ref, k_ref, vt_ref, o_ref, s_ref):
    nq, _, tq = qt_ref.shape[2:]
    nk, dvp, tk = vt_ref.shape[2:]
    unroll = next((u for u in (8, 4, 2) if nk % u == 0 and nk >= 2 * u), 1)
    cmax0 = jnp.full((8, tq), -1e30, F32)
    acc0 = jnp.zeros((dvp, tq), F32)

    def score_part(j, slot, qt, cmax):
        r0 = pl.multiple_of(j * tk, tk)
        s = _dot(k_ref[0, pl.ds(r0, tk), :], qt)
        s_ref[slot, j] = s
        return jnp.maximum(cmax, jnp.max(s.reshape(tk // 8, 8, tq), axis=0))

    def value_part(j, slot, m, acc):
        p = jnp.exp2((s_ref[slot, j] - m).astype(BF16))
        return acc + _dot(vt_ref[0, 0, j], p)

    def scores_only(i, slot):
        qt = qt_ref[0, 0, i]
        cmax = lax.fori_loop(0, nk, lambda j, c: score_part(j, slot, qt, c), cmax0, unroll=unroll)
        return jnp.max(cmax, axis=0, keepdims=True)

    def values_only(slot, m):
        return lax.fori_loop(0, nk, lambda j, a: value_part(j, slot, m, a), acc0, unroll=unroll)

    def both(i_next, cur, m):
        qt = qt_ref[0, 0, i_next]

        def body(j, carry):
            cmax, acc = carry
            return score_part(j, 1 - cur, qt, cmax), value_part(j, cur, m, acc)

        cmax, acc = lax.fori_loop(0, nk, body, (cmax0, acc0), unroll=unroll)
        return jnp.max(cmax, axis=0, keepdims=True), acc

    def emit(i, acc):
        o_t = acc[:MLA_V] / acc[MLA_V:MLA_V + 1]
        o_ref[0, pl.ds(pl.multiple_of(i * tq, tq), tq), :] = o_t.T.astype(o_ref.dtype)

    m = scores_only(0, 0)
    if nq == 1:
        emit(0, values_only(0, m))
        return
    assert nq % 2 == 0

    def pair(ip, m):
        i = 2 * ip
        m_odd, acc = both(i + 1, 0, m)
        emit(i, acc)
        m_even, acc = both(i + 2, 1, m_odd)
        emit(i + 1, acc)
        return m_even

    m = lax.fori_loop(0, nq // 2 - 1, pair, m)
    m_last, acc = both(nq - 1, 0, m)
    emit(nq - 2, acc)
    emit(nq - 1, values_only(1, m_last))


def _flash(qt, k, vt):
    b, s, _ = k.shape
    _, _, nq, _, tq = qt.shape
    _, _, nk, dv, tk = vt.shape
    whole = lambda shape: pl.BlockSpec(shape, lambda bi, h: (bi, h) + (0,) * (len(shape) - 2),
                                       pipeline_mode=pl.Buffered(1))
    return pl.pallas_call(
        _flash_kernel,
        grid=(b, MLA_HEADS),
        in_specs=[
            whole((1, 1, nq, MLA_QK_PAD, tq)),
            pl.BlockSpec((1, s, MLA_QK_PAD), lambda bi, h: (bi, 0, h), pipeline_mode=pl.Buffered(1)),
            whole((1, 1, nk, dv, tk)),
        ],
        out_specs=pl.BlockSpec((1, s, MLA_V), lambda bi, h: (bi, 0, h)),
        out_shape=jax.ShapeDtypeStruct((b, s, MLA_HEADS * MLA_V), BF16),
        scratch_shapes=[pltpu.VMEM((2, nk, tk, tq), F32)],
        compiler_params=_cparams(("parallel", "parallel")),
        name="mla_flash",
    )(qt, k, vt)


def _gdn_proj_kernel(xp_ref, x_ref, xn_ref, mod_ref, gpre_ref, wqkv_ref, wg_ref, wab_ref, conv_ref,
                     aneg_ref, dtb_ref, q_ref, k_ref, v_ref, gate_ref, gb_ref, p_ref,
                     *, tiles_per_seq):
    i = pl.program_id(0)
    tm = x_ref.shape[0]
    first = (i % tiles_per_seq) == 0
    last = (i % tiles_per_seq) == tiles_per_seq - 1

    def proj_rows(xr):
        return _dot(_modulated(xr, mod_ref, gpre_ref, 1).astype(BF16), wqkv_ref[...])

    hb = _modulated(x_ref[...], mod_ref, gpre_ref, 1).astype(BF16)
    p_prev = jnp.where(first, 0.0, proj_rows(xp_ref[...]))
    p_main = _dot(hb, wqkv_ref[...])
    p_next = jnp.where(last, 0.0, proj_rows(xn_ref[...]))
    n_slab = p_ref.shape[0]
    for c in range(n_slab):
        cs = slice(c * LANES, (c + 1) * LANES)
        p_ref[c, 0:GDN_HALO, :] = p_prev[:, cs]
        p_ref[c, GDN_HALO:GDN_HALO + tm, :] = p_main[:, cs]
        p_ref[c, GDN_HALO + tm:, :] = p_next[:, cs]

    def l2n(z):
        return z * lax.rsqrt(jnp.sum(z * z, axis=-1, keepdims=True) + NORM_EPS)

    pad = GDN_CONV // 2
    for c in range(n_slab):
        cs = slice(c * LANES, (c + 1) * LANES)
        acc = None
        for tap in range(GDN_CONV):
            r0 = GDN_HALO - pad + tap
            term = p_ref[c, r0:r0 + tm, :] * conv_ref[tap:tap + 1, cs]
            acc = term if acc is None else acc + term
        z = acc * jax.nn.sigmoid(acc)
        if c < GDN_HEADS:
            q_ref[:, cs] = l2n(z) * (GDN_DK ** -0.5)
        elif c < 2 * GDN_HEADS:
            k_ref[:, c * LANES - GDN_QK:(c + 1) * LANES - GDN_QK] = l2n(z)
        else:
            v_ref[:, c * LANES - 2 * GDN_QK:(c + 1) * LANES - 2 * GDN_QK] = z
    gate_ref[...] = _dot(hb, wg_ref[...])
    ab = _dot(hb, wab_ref[...])
    z = ab + dtb_ref[...]
    softplus = jnp.maximum(z, 0.0) + jnp.log(1.0 + jnp.exp(-jnp.abs(z)))
    lane = lax.broadcasted_iota(jnp.int32, ab.shape, 1)
    gb_ref[...] = jnp.where(lane < 2 * GDN_HEADS, aneg_ref[...] * softplus, jax.nn.sigmoid(ab))


def _gdn_proj(x, mod, gpre, wqkv, wg, wab, conv, aneg, dtb, *, seq, tm):
    t, d = x.shape
    hb = tm // GDN_HALO
    nblk8 = t // GDN_HALO
    n_conv = wqkv.shape[1]
    return pl.pallas_call(
        functools.partial(_gdn_proj_kernel, tiles_per_seq=seq // tm),
        grid=(t // tm,),
        in_specs=[
            pl.BlockSpec((GDN_HALO, d), lambda i: (jnp.maximum(i * hb - 1, 0), 0)),
            pl.BlockSpec((tm, d), lambda i: (i, 0)),
            pl.BlockSpec((GDN_HALO, d), lambda i: (jnp.minimum((i + 1) * hb, nblk8 - 1), 0)),
            pl.BlockSpec((1, 3 * N_SUB, d), lambda i: ((i * tm) // seq, 0, 0)),
            _const_spec((1, d)),
            _const_spec(wqkv.shape),
            _const_spec(wg.shape),
            _const_spec(wab.shape),
            _const_spec(conv.shape),
            _const_spec(aneg.shape),
            _const_spec(dtb.shape),
        ],
        out_specs=[
            pl.BlockSpec((tm, GDN_QK), lambda i: (i, 0)),
            pl.BlockSpec((tm, GDN_QK), lambda i: (i, 0)),
            pl.BlockSpec((tm, GDN_VW), lambda i: (i, 0)),
            pl.BlockSpec((tm, GDN_VW), lambda i: (i, 0)),
            pl.BlockSpec((tm, LANES), lambda i: (i, 0)),
        ],
        out_shape=[
            jax.ShapeDtypeStruct((t, GDN_QK), F32),
            jax.ShapeDtypeStruct((t, GDN_QK), F32),
            jax.ShapeDtypeStruct((t, GDN_VW), F32),
            jax.ShapeDtypeStruct((t, GDN_VW), F32),
            jax.ShapeDtypeStruct((t, LANES), F32),
        ],
        scratch_shapes=[pltpu.VMEM((n_conv // LANES, tm + 2 * GDN_HALO, LANES), F32)],
        compiler_params=_cparams(("parallel",)),
        name="gdn_proj",
    )(x, x, x, mod, gpre, wqkv, wg, wab, conv, aneg, dtb)


def _gdn_cumsum_kernel(gb_ref, o_ref):
    g = gb_ref[...]
    tm = g.shape[0]
    row = lax.broadcasted_iota(jnp.int32, (tm, tm), 0)
    col = lax.broadcasted_iota(jnp.int32, (tm, tm), 1)
    sh = int(math.log2(GDN_CHUNK))
    same = lax.shift_right_logical(row, sh) == lax.shift_right_logical(col, sh)
    lower = jnp.where(same & (col <= row), 1.0, 0.0).astype(BF16)
    upper = jnp.where(same & (col >= row), 1.0, 0.0).astype(BF16)
    g1 = g.astype(BF16)
    r1 = g - g1.astype(F32)
    g2 = r1.astype(BF16)
    g3 = (r1 - g2.astype(F32)).astype(BF16)
    pre = _dot(lower, g1) + _dot(lower, g2) + _dot(lower, g3)
    suf = _dot(upper, g1) + _dot(upper, g2) + _dot(upper, g3)
    lane = lax.broadcasted_iota(jnp.int32, g.shape, 1)
    o_ref[...] = jnp.where(lane < GDN_HEADS, pre, jnp.where(lane < 2 * GDN_HEADS, suf, g))


def _gdn_cumsum(gb, *, tm):
    t = gb.shape[0]
    return pl.pallas_call(
        _gdn_cumsum_kernel,
        grid=(t // tm,),
        in_specs=[pl.BlockSpec((tm, LANES), lambda i: (i, 0))],
        out_specs=pl.BlockSpec((tm, LANES), lambda i: (i, 0)),
        out_shape=jax.ShapeDtypeStruct((t, LANES), F32),
        compiler_params=_cparams(("parallel",)),
        name="gdn_cumsum",
    )(gb)


def _gdn_delta_kernel(q_ref, k_ref, v_ref, gcol_ref, bcol_ref, grow_ref, o_ref, state_ref,
                      *, reverse):
    c = GDN_CHUNK
    nchunk = q_ref.shape[0] // c

    @pl.when(pl.program_id(1) == 0)
    def _():
        state_ref[...] = jnp.zeros_like(state_ref)

    row = lax.broadcasted_iota(jnp.int32, (c, c), 0)
    col = lax.broadcasted_iota(jnp.int32, (c, c), 1)
    if reverse:
        row, col = col, row
    tri = row >= col
    strict = row > col
    eye = jnp.where(row == col, 1.0, 0.0)
    level_masks = []
    for lv in range(int(math.log2(c))):
        rb = lax.shift_right_logical(row, lv)
        cb = lax.shift_right_logical(col, lv)
        level_masks.append((lax.shift_right_logical(rb, 1) == lax.shift_right_logical(cb, 1))
                           & ((rb & 1) == 1) & ((cb & 1) == 0))
    last_row = 0 if reverse else c - 1

    heads = range(GDN_HEADS)
    order = list(reversed(range(nchunk))) if reverse else list(range(nchunk))
    probs = [(j, h) for j in order for h in heads]
    rows = lambda j: slice(j * c, (j + 1) * c)
    cols = lambda h: slice(h * GDN_DK, (h + 1) * GDN_DK)

    qs = [q_ref[rows(j), cols(h)] for j, h in probs]
    ks = [k_ref[rows(j), cols(h)] for j, h in probs]
    vs = [v_ref[rows(j), cols(h)] for j, h in probs]
    gcbs = [jnp.broadcast_to(gcol_ref[0, rows(j), h:h + 1], (c, GDN_DK)) for j, h in probs]
    betas = [jnp.broadcast_to(bcol_ref[0, rows(j), h:h + 1], (c, GDN_DK)) for j, h in probs]
    n = len(probs)
    decays = []
    for p, (j, h) in enumerate(probs):
        grb = jnp.broadcast_to(grow_ref[0, j, h:h + 1, :], (c, c))
        decays.append(jnp.where(tri, jnp.exp(jnp.where(tri, gcbs[p][:, :c] - grb, 0.0)), 0.0))
    kbs = [ks[p] * betas[p] for p in range(n)]
    a2s = [_dot_nt(jnp.concatenate([kbs[p], qs[p]], axis=0).astype(BF16), ks[p].astype(BF16))
           for p in range(n)]
    ms = [jnp.where(strict, a2s[p][:c] * decays[p], 0.0) for p in range(n)]
    qks = [(a2s[p][c:] * decays[p]).astype(BF16) for p in range(n)]
    tinvs = [eye - jnp.where(level_masks[0], ms[p], 0.0) for p in range(n)]
    for lm in level_masks[1:]:
        tbs = [t.astype(BF16) for t in tinvs]
        tcs = [_dot(tbs[p], jnp.where(lm, ms[p], 0.0).astype(BF16)).astype(BF16) for p in range(n)]
        tinvs = [tinvs[p] - _dot(tcs[p], tbs[p]) for p in range(n)]
    egs = [jnp.exp(g) for g in gcbs]
    sols = [_dot(tinvs[p].astype(BF16),
                 jnp.concatenate([vs[p] * betas[p], kbs[p] * egs[p]], axis=1).astype(BF16))
            for p in range(n)]
    g_lasts = [g[last_row:last_row + 1, :] for g in gcbs]
    k_decs = [(ks[p] * jnp.exp(g_lasts[p] - gcbs[p])).astype(BF16) for p in range(n)]
    wq_lhs = [jnp.concatenate([sols[p][:, GDN_DV:], qs[p] * egs[p]], axis=0).astype(BF16)
              for p in range(n)]

    sts = [state_ref[h] for h in heads]
    for step, j in enumerate(order):
        ps = [step * GDN_HEADS + h for h in heads]
        wqs = [_dot(wq_lhs[ps[h]], sts[h].astype(BF16)) for h in heads]
        vnbs = [(sols[ps[h]][:, :GDN_DV] - wqs[h][:c]).astype(BF16) for h in heads]
        for h in heads:
            o_ref[rows(j), cols(h)] = wqs[h][c:] + _dot(qks[ps[h]], vnbs[h])
        sts = [sts[h] * jnp.exp(g_lasts[ps[h]]) + _dot_tn(k_decs[ps[h]], vnbs[h]) for h in heads]
    for h in heads:
        state_ref[h] = sts[h]


def _gdn_delta(q, k, v, gcol, bcol, grow, *, batch, seq, tb, reverse):
    t = q.shape[0]
    nb = seq // tb
    cpb = tb // GDN_CHUNK

    def blk(bi, i):
        return bi * nb + ((nb - 1 - i) if reverse else i)

    def seq_blk(bi, i):
        return (nb - 1 - i) if reverse else i

    return pl.pallas_call(
        functools.partial(_gdn_delta_kernel, reverse=reverse),
        grid=(batch, nb),
        in_specs=[
            pl.BlockSpec((tb, GDN_QK), lambda bi, i: (blk(bi, i), 0)),
            pl.BlockSpec((tb, GDN_QK), lambda bi, i: (blk(bi, i), 0)),
            pl.BlockSpec((tb, GDN_VW), lambda bi, i: (blk(bi, i), 0)),
            pl.BlockSpec((1, tb, GDN_HEADS), lambda bi, i: (bi, seq_blk(bi, i), 0)),
            pl.BlockSpec((1, tb, GDN_HEADS), lambda bi, i: (bi, seq_blk(bi, i), 0)),
            pl.BlockSpec((1, cpb, GDN_HEADS, GDN_CHUNK), lambda bi, i: (bi, seq_blk(bi, i), 0, 0)),
        ],
        out_specs=pl.BlockSpec((tb, GDN_VW), lambda bi, i: (blk(bi, i), 0)),
        out_shape=jax.ShapeDtypeStruct((t, GDN_VW), F32),
        scratch_shapes=[pltpu.VMEM((GDN_HEADS, GDN_DK, GDN_DV), F32)],
        compiler_params=_cparams(("parallel", "arbitrary")),
        name="gdn_delta_bw" if reverse else "gdn_delta_fw",
    )(q, k, v, gcol, bcol, grow)


def _gdn_out_kernel(of_ref, ob_ref, gate_ref, x_ref, mod_ref, gpost_ref, onorm_ref, w_ref, o_ref,
                    a_ref):
    o = of_ref[...] + ob_ref[...]
    gate = gate_ref[...]
    og = gate * jax.nn.sigmoid(gate)
    for h in range(GDN_HEADS):
        cs = slice(h * GDN_DV, (h + 1) * GDN_DV)
        a_ref[:, cs] = (_rms(o[:, cs], onorm_ref[...]) * og[:, cs]).astype(BF16)
    out = _dot(a_ref[...], w_ref[...])
    o_ref[...] = _residual(x_ref[...], out, mod_ref, gpost_ref, 1, 1.0)


def _gdn_out(o_fw, o_bw, gate, x, mod, gpost, onorm, w, *, seq, tm):
    t, d = x.shape
    return pl.pallas_call(
        _gdn_out_kernel,
        grid=(t // tm,),
        in_specs=[
            pl.BlockSpec((tm, GDN_VW), lambda i: (i, 0)),
            pl.BlockSpec((tm, GDN_VW), lambda i: (i, 0)),
            pl.BlockSpec((tm, GDN_VW), lambda i: (i, 0)),
            pl.BlockSpec((tm, d), lambda i: (i, 0)),
            pl.BlockSpec((1, 3 * N_SUB, d), lambda i: ((i * tm) // seq, 0, 0)),
            _const_spec((1, d)),
            _const_spec(onorm.shape),
            _const_spec(w.shape),
        ],
        out_specs=pl.BlockSpec((tm, d), lambda i: (i, 0)),
        out_shape=jax.ShapeDtypeStruct((t, d), F32),
        scratch_shapes=[pltpu.VMEM((tm, GDN_VW), BF16)],
        compiler_params=_cparams(("parallel",)),
        name="gdn_out",
    )(o_fw, o_bw, gate, x, mod, gpost, onorm, w)


def _fnet_chan_kernel(x_ref, mod_ref, gpre_ref, cs_ref, o_ref):
    hb = _modulated(x_ref[...], mod_ref, gpre_ref, 1).astype(BF16)
    cg = hb.shape[1] // FNET_GROUPS
    for g in range(FNET_GROUPS):
        ab = _dot(hb[:, g * cg:(g + 1) * cg], cs_ref[...])
        o_ref[0, 0, :, g * cg:(g + 1) * cg] = ab[:, :cg]
        o_ref[0, 1, :, g * cg:(g + 1) * cg] = ab[:, cg:]


def _fnet_chan(x, mod, gpre, cs, *, batch, seq, tm):
    t, d = x.shape
    nt = seq // tm
    return pl.pallas_call(
        _fnet_chan_kernel,
        grid=(t // tm,),
        in_specs=[
            pl.BlockSpec((tm, d), lambda i: (i, 0)),
            pl.BlockSpec((1, 3 * N_SUB, d), lambda i: ((i * tm) // seq, 0, 0)),
            _const_spec((1, d)),
            _const_spec(cs.shape),
        ],
        out_specs=pl.BlockSpec((1, 2, tm, d), lambda i: (i // nt, 0, i % nt, 0)),
        out_shape=jax.ShapeDtypeStruct((batch, 2, seq, d), F32),
        compiler_params=_cparams(("parallel",)),
        name="fnet_chan",
    )(x, mod, gpre, cs)


def _fnet_s1_kernel(m1_ref, ab_ref, y_ref):
    _, two, s1, _, rows, d = ab_ref.shape
    ab = ab_ref[0].reshape(two * s1 * rows, d).astype(BF16)
    y = _dot(m1_ref[...], ab)
    y_ref[0] = y.reshape(two, s1, 1, rows, d)


def _fnet_s1(m1, ab):
    batch, _, s1, groups, rows, d = ab.shape
    blk = (1, 2, s1, 1, rows, d)
    return pl.pallas_call(
        _fnet_s1_kernel,
        grid=(batch, groups),
        in_specs=[
            _const_spec(m1.shape),
            pl.BlockSpec(blk, lambda bi, g: (bi, 0, 0, g, 0, 0)),
        ],
        out_specs=pl.BlockSpec(blk, lambda bi, g: (bi, 0, 0, g, 0, 0)),
        out_shape=jax.ShapeDtypeStruct(ab.shape, F32),
        compiler_params=_cparams(("parallel", "parallel")),
        name="fnet_s1",
    )(m1, ab)


def _fnet_s2_kernel(m2_ref, tw_ref, y_ref, f_ref, z_ref):
    n, d = y_ref.shape[2:]
    cos = tw_ref[0]
    sin = tw_ref[1]
    for c0 in range(0, d, LANES):
        yr = y_ref[0, 0, :, c0:c0 + LANES]
        yi = y_ref[0, 1, :, c0:c0 + LANES]
        z_ref[:n, c0:c0 + LANES] = (yr * cos + yi * sin).astype(BF16)
        z_ref[n:, c0:c0 + LANES] = (yi * cos - yr * sin).astype(BF16)
    f = _dot(m2_ref[...], z_ref[...])
    f_ref[0] = f.reshape(f_ref.shape[1:])


def _fnet_s2(m2, tw, y, *, s1, s2):
    batch, _, _, d = y.shape
    n = 8 * s2
    return pl.pallas_call(
        _fnet_s2_kernel,
        grid=(batch, s1 // 8),
        in_specs=[
            _const_spec(m2.shape),
            pl.BlockSpec((2, n, LANES), lambda bi, a: (0, a, 0)),
            pl.BlockSpec((1, 2, n, d), lambda bi, a: (bi, 0, a, 0)),
        ],
        out_specs=pl.BlockSpec((1, s2, 1, 8, d), lambda bi, a: (bi, 0, a, 0, 0)),
        out_shape=jax.ShapeDtypeStruct((batch, s2, s1 // 8, 8, d), F32),
        scratch_shapes=[pltpu.VMEM((2 * n, d), BF16)],
        compiler_params=_cparams(("parallel", "parallel")),
        name="fnet_s2",
    )(m2, tw, y)


def _rope_table(seq):
    half = MLA_ROPE // 2
    pos = jnp.arange(seq, dtype=F32)
    inv_freq = ROPE_THETA ** (-jnp.arange(half, dtype=F32) / half)
    ang = pos[:, None] * inv_freq[None, :]
    cos, sin = jnp.cos(ang), jnp.sin(ang)
    return jnp.concatenate([cos, cos, -sin, sin], axis=1)


def _swap_halves(w):
    half = w.shape[-1] // 2
    return jnp.concatenate([w[..., half:], w[..., :half]], axis=-1)


def _prep_mla(w_down, w_uq, w_ukv):
    d = w_down.shape[0]
    rope = w_down[:, MLA_Q_LORA + MLA_KV_LORA:]
    wd = jnp.concatenate([w_down, _swap_halves(rope)], axis=1).astype(BF16)
    uq = w_uq.reshape(MLA_Q_LORA, MLA_HEADS, MLA_NOPE + MLA_ROPE)
    uq_rope = uq[..., MLA_NOPE:]
    wq = jnp.concatenate([uq, _swap_halves(uq_rope)], axis=-1)
    wq = wq.reshape(MLA_Q_LORA, MLA_HEADS * MLA_QK_PAD).astype(BF16)
    ukv = w_ukv.reshape(MLA_KV_LORA, MLA_HEADS, MLA_NOPE + MLA_V)
    wkv = jnp.concatenate([ukv[..., :MLA_NOPE].reshape(MLA_KV_LORA, -1),
                           ukv[..., MLA_NOPE:].reshape(MLA_KV_LORA, -1)], axis=1).astype(BF16)
    del d
    return wd, wq, wkv


def _fnet_tables(seq, cg):
    def trig(n):
        idx = jnp.arange(n, dtype=jnp.int32)
        ang = ((idx[:, None] * idx[None, :]) % n).astype(F32) * (2.0 * math.pi / n)
        return jnp.cos(ang), jnp.sin(ang)

    cc, sc = trig(cg)
    cs = (jnp.concatenate([cc, sc], axis=1) * (cg ** -0.5)).astype(BF16)
    s2 = min(seq, LANES)
    s1 = seq // s2
    eye8 = jnp.eye(8, dtype=F32)
    c1, sn1 = trig(s1)
    m1 = jnp.block([[c1, -sn1], [-sn1, -c1]]) * (s1 ** -0.5)
    m1 = jnp.kron(m1, eye8).astype(BF16)
    c2, sn2 = trig(s2)
    m2 = jnp.stack([c2, sn2]) * (s2 ** -0.5)
    m2 = jnp.einsum('cts,rq->trcqs', m2, eye8).reshape(8 * s2, 16 * s2).astype(BF16)
    t1 = jnp.arange(s1, dtype=jnp.int32)[:, None]
    k2 = jnp.arange(s2, dtype=jnp.int32)[None, :]
    ang = ((t1 * k2).astype(F32) * (2.0 * math.pi / seq)).reshape(seq)
    tw = jnp.stack([jnp.cos(ang), jnp.sin(ang)])
    tw = jnp.broadcast_to(tw[..., None], (2, seq, LANES))
    return cs, m1, m2, tw


def _pick(n, pref):
    return pref if n % pref == 0 else n


def _trunk(x3, mods, prm):
    batch, seq, d = x3.shape
    t = batch * seq
    x = x3.reshape(t, d)
    tm = _pick(seq, 512)
    row = lambda v: v.reshape(1, -1)
    trig = _rope_table(seq)
    for l in range(mods.shape[0]):
        i = l // 3
        kind = l % 3
        mod = mods[l]
        gpre = prm['norm_pre'][l]
        gpost = prm['norm_post'][l]
        x = _ffn(x, mod, row(gpre[0]), row(gpost[0]), prm['ffn_w_in'][l][0], prm['ffn_w_out'][l][0],
                 sub=0, seq=seq, tm=tm)
        if kind == 0:
            wd, wq, wkv = prm['mla'][i]
            qt, k, vt = _mla_proj(x, mod, row(gpre[1]), trig, wd, row(prm['mla_q_norm'][i]),
                                  row(prm['mla_kv_norm'][i]), wq, wkv, seq=seq, tm=tm)
            o = _flash(qt, k.reshape(batch, seq, -1), vt)
            mixed = (o.reshape(t, -1), prm['mla_w_out'][i], jnp.zeros((1, d), F32))
        elif kind == 1:
            g = prm['gdn'][i]
            q, k, v, gate, gb = _gdn_proj(x, mod, row(gpre[1]), g['wqkv'], g['wg'], g['wab'], g['conv'],
                                          g['aneg'], g['dtb'], seq=seq, tm=_pick(seq, 256))
            gc = _gdn_cumsum(gb, tm=tm)
            nh = GDN_HEADS
            gc3 = gc.reshape(batch, seq, LANES)
            outs = []
            for dr in range(2):
                gcol = gc3[:, :, dr * nh:(dr + 1) * nh]
                bcol = gc3[:, :, 2 * nh + dr * nh:2 * nh + (dr + 1) * nh]
                grow = jnp.swapaxes(gcol.reshape(batch, seq // GDN_CHUNK, GDN_CHUNK, nh), 2, 3)
                outs.append(_gdn_delta(q, k, v, gcol, bcol, grow, batch=batch, seq=seq,
                                       tb=_pick(seq, 256), reverse=bool(dr)))
            x = _gdn_out(outs[0], outs[1], gate, x, mod, row(gpost[1]), row(prm['gdn_o_norm'][i]),
                         g['wout'], seq=seq, tm=tm)
            mixed = None
        else:
            cs, m1, m2, tw = _fnet_tables(seq, d // FNET_GROUPS)
            s2 = min(seq, LANES)
            s1 = seq // s2
            ab = _fnet_chan(x, mod, row(gpre[1]), cs, batch=batch, seq=seq, tm=tm)
            y = _fnet_s1(m1, ab.reshape(batch, 2, s1, s2 // 8, 8, d))
            f = _fnet_s2(m2, tw, y.reshape(batch, 2, seq, d), s1=s1, s2=s2)
            mixed = (f.reshape(t, d), prm['fnet_w_out'][i], row(prm['fnet_b_out'][i]))
        w_in2, w_out2 = prm['ffn_w_in'][l][1], prm['ffn_w_out'][l][1]
        if mixed is None:
            x = _ffn(x, mod, row(gpre[2]), row(gpost[2]), w_in2, w_out2, sub=2, seq=seq, tm=tm)
        else:
            a, w_mix, b_mix = mixed
            x = _mix_ffn(a, x, mod, row(gpost[1]), w_mix, b_mix, row(gpre[2]), row(gpost[2]),
                         w_in2, w_out2, seq=seq, tm=tm)
    return x.reshape(batch, seq, d)


def _prep_gdn(w_in, conv, a_log, dt_bias, w_out):
    n_conv = 2 * GDN_QK + GDN_VW
    wab = jnp.pad(w_in[:, n_conv + GDN_VW:], ((0, 0), (0, LANES - 4 * GDN_HEADS)))
    pad16 = lambda v: jnp.pad(v.reshape(1, -1).astype(F32), ((0, 0), (0, LANES - 2 * GDN_HEADS)))
    return {
        'wqkv': w_in[:, :n_conv].astype(BF16),
        'wg': w_in[:, n_conv:n_conv + GDN_VW].astype(BF16),
        'wab': wab.astype(BF16),
        'conv': conv.astype(F32),
        'aneg': pad16(-jnp.exp(a_log.astype(F32))),
        'dtb': pad16(dt_bias),
        'wout': w_out.astype(BF16),
    }


def kernel(x_prompt, x_sample, c_prompt, c_sample, w_ada, b_ada, norm_pre, norm_post, ffn_w_in, ffn_w_out, mla_w_down, mla_q_norm, mla_kv_norm, mla_w_uq, mla_w_ukv, mla_w_out, gdn_w_in, gdn_conv, gdn_a_log, gdn_dt_bias, gdn_o_norm, gdn_w_out, fnet_w_out, fnet_b_out):
    d = x_prompt.shape[-1]
    prm = {
        'norm_pre': norm_pre, 'norm_post': norm_post,
        'ffn_w_in': ffn_w_in.astype(BF16), 'ffn_w_out': ffn_w_out.astype(BF16),
        'mla': [_prep_mla(mla_w_down[i], mla_w_uq[i], mla_w_ukv[i]) for i in range(mla_w_down.shape[0])],
        'mla_q_norm': mla_q_norm, 'mla_kv_norm': mla_kv_norm, 'mla_w_out': mla_w_out.astype(BF16),
        'gdn': [_prep_gdn(gdn_w_in[i], gdn_conv[i], gdn_a_log[i], gdn_dt_bias[i], gdn_w_out[i])
                for i in range(gdn_w_in.shape[0])],
        'gdn_o_norm': gdn_o_norm,
        'fnet_w_out': fnet_w_out.astype(BF16), 'fnet_b_out': fnet_b_out,
    }
    nbp = c_prompt.shape[0]
    c_all = jnp.concatenate([c_prompt, c_sample], axis=0)
    mods = _ada_mod(c_all, w_ada, b_ada).reshape(w_ada.shape[0], c_all.shape[0], 3 * N_SUB, d)
    y_prompt = _trunk(x_prompt, mods[:, :nbp], prm)
    y_sample = _trunk(x_sample, mods[:, nbp:], prm)
    return (y_prompt, y_sample)
```

```python
import functools
import math

import jax
import jax.numpy as jnp
from jax import lax
from jax.experimental import pallas as pl
from jax.experimental.pallas import tpu as pltpu

F32 = jnp.float32
BF16 = jnp.bfloat16

N_SUB = 3
D_FF = 2816
NORM_EPS = 1e-6

MLA_HEADS = 8
MLA_Q_LORA = 384
MLA_KV_LORA = 256
MLA_NOPE = 128
MLA_ROPE = 64
MLA_V = 128
ROPE_THETA = 10000.0
MLA_SCALE = (MLA_NOPE + MLA_ROPE) ** -0.5
MLA_QK_PAD = 256
MLA_V_PAD = MLA_V + 16

GDN_HEADS = 8
GDN_DK = 128
GDN_DV = 128
GDN_QK = GDN_HEADS * GDN_DK
GDN_VW = GDN_HEADS * GDN_DV
GDN_CONV = 5
GDN_CHUNK = 64
GDN_HALO = 8
GDN_SLABS_PER_DOT = 4

FNET_GROUPS = 8

LANES = 128
VMEM_LIMIT = 56 * 1024 * 1024


def _cparams(sem):
    return pltpu.CompilerParams(dimension_semantics=sem, vmem_limit_bytes=VMEM_LIMIT)


def _const_spec(shape):
    nd = len(shape)
    return pl.BlockSpec(shape, lambda *_: (0,) * nd, pipeline_mode=pl.Buffered(1))


def _rms(x, g):
    ms = jnp.mean(x * x, axis=-1, keepdims=True)
    return x * lax.rsqrt(ms + NORM_EPS) * g


def _modulated(x, mod_ref, gpre_ref, sub):
    shift = mod_ref[0, 3 * sub:3 * sub + 1, :]
    scale = mod_ref[0, 3 * sub + 1:3 * sub + 2, :]
    return _rms(x, gpre_ref[...]) * (1.0 + scale) + shift


def _residual(x, out, mod_ref, gpost_ref, sub, weight):
    gate = mod_ref[0, 3 * sub + 2:3 * sub + 3, :]
    return x + (weight * gate) * _rms(out, gpost_ref[...])


def _dot(a, b):
    return jnp.dot(a, b, preferred_element_type=F32)


def _dot_nt(a, b):
    return lax.dot_general(a, b, (((1,), (1,)), ((), ())), preferred_element_type=F32)


def _dot_tn(a, b):
    return lax.dot_general(a, b, (((0,), (0,)), ((), ())), preferred_element_type=F32)


def _ada_kernel(c_ref, w_ref, b_ref, o_ref):
    c = c_ref[...]
    sc = (c * jax.nn.sigmoid(c)).astype(BF16)
    o_ref[0] = _dot(sc, w_ref[0].astype(BF16)) + b_ref[0]


def _ada_mod(c_all, w_ada, b_ada):
    nb, d = c_all.shape
    depth, _, n_out = w_ada.shape
    tn = n_out // 8
    return pl.pallas_call(
        _ada_kernel,
        grid=(depth, n_out // tn),
        in_specs=[
            pl.BlockSpec((nb, d), lambda l, j: (0, 0)),
            pl.BlockSpec((1, d, tn), lambda l, j: (l, 0, j)),
            pl.BlockSpec((1, 1, tn), lambda l, j: (l, 0, j)),
        ],
        out_specs=pl.BlockSpec((1, nb, tn), lambda l, j: (l, 0, j)),
        out_shape=jax.ShapeDtypeStruct((depth, nb, n_out), F32),
        compiler_params=_cparams(("arbitrary", "arbitrary")),
        name="ada_mod",
    )(c_all, w_ada, b_ada.reshape(depth, 1, n_out))


FFN_CHUNK = 256


FFN_SPLIT = 2


def _ffn_step(xs, mod_ref, gpre_ref, gpost_ref, win_ref, wout_ref, act_ref, sub):
    n = xs[0].shape[0]
    hbs = [_modulated(x, mod_ref, gpre_ref, sub).astype(BF16) for x in xs]
    for c0 in range(0, D_FF, FFN_CHUNK):
        for r, hb in enumerate(hbs):
            g = _dot(hb, win_ref[:, c0:c0 + FFN_CHUNK])
            u = _dot(hb, win_ref[:, D_FF + c0:D_FF + c0 + FFN_CHUNK])
            act_ref[r * n:(r + 1) * n, c0:c0 + FFN_CHUNK] = (g * jax.nn.sigmoid(g) * u).astype(BF16)
    outs = [_dot(act_ref[r * n:(r + 1) * n, :], wout_ref[...]) for r in range(len(xs))]
    return [_residual(x, out, mod_ref, gpost_ref, sub, 0.5) for x, out in zip(xs, outs)]


def _row_groups(ref):
    n = ref.shape[0] // FFN_SPLIT
    return [ref[r * n:(r + 1) * n, :] for r in range(FFN_SPLIT)]


def _store_row_groups(ref, vals):
    n = ref.shape[0] // FFN_SPLIT
    for r, v in enumerate(vals):
        ref[r * n:(r + 1) * n, :] = v


def _ffn_kernel(x_ref, mod_ref, gpre_ref, gpost_ref, win_ref, wout_ref, o_ref, act_ref, *, sub):
    _store_row_groups(o_ref, _ffn_step(_row_groups(x_ref), mod_ref, gpre_ref, gpost_ref,
                                        win_ref, wout_ref, act_ref, sub))


def _mix_ffn_kernel(a_ref, x_ref, mod_ref, gpost_mix_ref, wmix_ref, bmix_ref,
                    gpre_ref, gpost_ref, win_ref, wout_ref, o_ref, act_ref):
    xs = [_residual(x, _dot(a.astype(BF16), wmix_ref[...]) + bmix_ref[...], mod_ref, gpost_mix_ref, 1, 1.0)
          for a, x in zip(_row_groups(a_ref), _row_groups(x_ref))]
    _store_row_groups(o_ref, _ffn_step(xs, mod_ref, gpre_ref, gpost_ref, win_ref, wout_ref, act_ref, 2))


def _mix_ffn(a, x, mod, gpost_mix, w_mix, b_mix, gpre, gpost, w_in, w_out, *, seq, tm):
    t, d = x.shape
    return pl.pallas_call(
        _mix_ffn_kernel,
        grid=(t // tm,),
        in_specs=[
            pl.BlockSpec((tm, a.shape[1]), lambda i: (i, 0)),
            pl.BlockSpec((tm, d), lambda i: (i, 0)),
            pl.BlockSpec((1, 3 * N_SUB, d), lambda i: ((i * tm) // seq, 0, 0)),
            _const_spec((1, d)),
            _const_spec(w_mix.shape),
            _const_spec((1, d)),
            _const_spec((1, d)),
            _const_spec((1, d)),
            _const_spec(w_in.shape),
            _const_spec(w_out.shape),
        ],
        out_specs=pl.BlockSpec((tm, d), lambda i: (i, 0)),
        out_shape=jax.ShapeDtypeStruct((t, d), F32),
        scratch_shapes=[pltpu.VMEM((tm, D_FF), BF16)],
        compiler_params=_cparams(("parallel",)),
        name="mix_ffn",
    )(a, x, mod, gpost_mix, w_mix, b_mix, gpre, gpost, w_in, w_out)


def _ffn(x, mod, gpre, gpost, w_in, w_out, *, sub, seq, tm):
    t, d = x.shape
    return pl.pallas_call(
        functools.partial(_ffn_kernel, sub=sub),
        grid=(t // tm,),
        in_specs=[
            pl.BlockSpec((tm, d), lambda i: (i, 0)),
            pl.BlockSpec((1, 3 * N_SUB, d), lambda i: ((i * tm) // seq, 0, 0)),
            _const_spec((1, d)),
            _const_spec((1, d)),
            _const_spec(w_in.shape),
            _const_spec(w_out.shape),
        ],
        out_specs=pl.BlockSpec((tm, d), lambda i: (i, 0)),
        out_shape=jax.ShapeDtypeStruct((t, d), F32),
        scratch_shapes=[pltpu.VMEM((tm, D_FF), BF16)],
        compiler_params=_cparams(("parallel",)),
        name="ffn",
    )(x, mod, gpre, gpost, w_in, w_out)


def _rope_pair(t):
    return t + pltpu.roll(t, 64, axis=1)


def _mla_proj_kernel(x_ref, mod_ref, gpre_ref, trig_ref, wd_ref, qn_ref, kvn_ref, wq_ref, wkv_ref,
                     qt_ref, k_ref, vt_ref):
    x = x_ref[...]
    hb = _modulated(x, mod_ref, gpre_ref, 1).astype(BF16)
    down = _dot(hb, wd_ref[...])
    cq = _rms(down[:, :MLA_Q_LORA], qn_ref[...]).astype(BF16)
    ckv = _rms(down[:, MLA_Q_LORA:MLA_Q_LORA + MLA_KV_LORA], kvn_ref[...]).astype(BF16)
    trig = trig_ref[...]
    lane = lax.broadcasted_iota(jnp.int32, trig.shape, 1)
    k_rope = _rope_pair(down[:, MLA_Q_LORA + MLA_KV_LORA:] * trig)
    k_rope = jnp.where(lane < MLA_ROPE, k_rope, 0.0).astype(BF16)
    q = _dot(cq, wq_ref[...])
    kv = _dot(ckv, wkv_ref[...])
    qs = MLA_SCALE * math.log2(math.e)
    for h in range(MLA_HEADS):
        c0 = h * MLA_QK_PAD
        q_rope = _rope_pair(q[:, c0 + LANES:c0 + 2 * LANES] * trig)
        qt_ref[0, h, 0, :LANES, :] = (q[:, c0:c0 + LANES] * qs).T.astype(BF16)
        qt_ref[0, h, 0, LANES:, :] = (q_rope * qs).T.astype(BF16)
        k_ref[:, c0:c0 + LANES] = kv[:, h * MLA_NOPE:(h + 1) * MLA_NOPE].astype(BF16)
        k_ref[:, c0 + LANES:c0 + 2 * LANES] = k_rope
        v0 = MLA_HEADS * MLA_NOPE + h * MLA_V
        vt_ref[0, h, 0, :MLA_V, :] = kv[:, v0:v0 + MLA_V].T.astype(BF16)
        pad_row = lax.broadcasted_iota(jnp.int32, (MLA_V_PAD - MLA_V, x.shape[0]), 0)
        vt_ref[0, h, 0, MLA_V:, :] = jnp.where(pad_row == 0, 1.0, 0.0).astype(BF16)


def _mla_proj(x, mod, gpre, trig, wd, qn, kvn, wq, wkv, *, seq, tm):
    t, d = x.shape
    nq = MLA_HEADS * MLA_QK_PAD
    nt = seq // tm
    return pl.pallas_call(
        _mla_proj_kernel,
        grid=(t // tm,),
        in_specs=[
            pl.BlockSpec((tm, d), lambda i: (i, 0)),
            pl.BlockSpec((1, 3 * N_SUB, d), lambda i: ((i * tm) // seq, 0, 0)),
            _const_spec((1, d)),
            pl.BlockSpec((tm, LANES), lambda i: (i % (seq // tm), 0)),
            _const_spec(wd.shape),
            _const_spec(qn.shape),
            _const_spec(kvn.shape),
            _const_spec(wq.shape),
            _const_spec(wkv.shape),
        ],
        out_specs=[
            pl.BlockSpec((1, MLA_HEADS, 1, MLA_QK_PAD, tm), lambda i: (i // nt, 0, i % nt, 0, 0)),
            pl.BlockSpec((tm, nq), lambda i: (i, 0)),
            pl.BlockSpec((1, MLA_HEADS, 1, MLA_V_PAD, tm), lambda i: (i // nt, 0, i % nt, 0, 0)),
        ],
        out_shape=[
            jax.ShapeDtypeStruct((t // seq, MLA_HEADS, nt, MLA_QK_PAD, tm), BF16),
            jax.ShapeDtypeStruct((t, nq), BF16),
            jax.ShapeDtypeStruct((t // seq, MLA_HEADS, nt, MLA_V_PAD, tm), BF16),
        ],
        compiler_params=_cparams(("parallel",)),
        name="mla_proj",
    )(x, mod, gpre, trig, wd, qn, kvn, wq, wkv)


def _flash_kernel(qt_ref, k_ref, vt_ref, o_ref, s_ref):
    nq, _, tq = qt_ref.shape[2:]
    nk, dvp, tk = vt_ref.shape[2:]
    unroll = next((u for u in (8, 4, 2) if nk % u == 0 and nk >= 2 * u), 1)
    cmax0 = jnp.full((8, tq), -1e30, F32)
    acc0 = jnp.zeros((dvp, tq), F32)

    def score_part(j, slot, qt, cmax):
        r0 = pl.multiple_of(j * tk, tk)
        s = _dot(k_ref[0, pl.ds(r0, tk), :], qt)
        s_ref[slot, j] = s
        return jnp.maximum(cmax, jnp.max(s.reshape(tk // 8, 8, tq), axis=0))

    def value_part(j, slot, m, acc):
        p = jnp.exp2((s_ref[slot, j] - m).astype(BF16))
        return acc + _dot(vt_ref[0, 0, j], p)

    def scores_only(i, slot):
        qt = qt_ref[0, 0, i]
        cmax = lax.fori_loop(0, nk, lambda j, c: score_part(j, slot, qt, c), cmax0, unroll=unroll)
        return jnp.max(cmax, axis=0, keepdims=True)

    def values_only(slot, m):
        return lax.fori_loop(0, nk, lambda j, a: value_part(j, slot, m, a), acc0, unroll=unroll)

    def both(i_next, cur, m):
        qt = qt_ref[0, 0, i_next]

        def body(j, carry):
            cmax, acc = carry
            return score_part(j, 1 - cur, qt, cmax), value_part(j, cur, m, acc)

        cmax, acc = lax.fori_loop(0, nk, body, (cmax0, acc0), unroll=unroll)
        return jnp.max(cmax, axis=0, keepdims=True), acc

    def emit(i, acc):
        o_t = acc[:MLA_V] / acc[MLA_V:MLA_V + 1]
        o_ref[0, pl.ds(pl.multiple_of(i * tq, tq), tq), :] = o_t.T.astype(o_ref.dtype)

    m = scores_only(0, 0)
    if nq == 1:
        emit(0, values_only(0, m))
        return
    assert nq % 2 == 0

    def pair(ip, m):
        i = 2 * ip
        m_odd, acc = both(i + 1, 0, m)
        emit(i, acc)
        m_even, acc = both(i + 2, 1, m_odd)
        emit(i + 1, acc)
        return m_even

    m = lax.fori_loop(0, nq // 2 - 1, pair, m)
    m_last, acc = both(nq - 1, 0, m)
    emit(nq - 2, acc)
    emit(nq - 1, values_only(1, m_last))


def _flash(qt, k, vt):
    b, s, _ = k.shape
    _, _, nq, _, tq = qt.shape
    _, _, nk, dv, tk = vt.shape
    whole = lambda shape: pl.BlockSpec(shape, lambda bi, h: (bi, h) + (0,) * (len(shape) - 2),
                                       pipeline_mode=pl.Buffered(1))
    return pl.pallas_call(
        _flash_kernel,
        grid=(b, MLA_HEADS),
        in_specs=[
            whole((1, 1, nq, MLA_QK_PAD, tq)),
            pl.BlockSpec((1, s, MLA_QK_PAD), lambda bi, h: (bi, 0, h), pipeline_mode=pl.Buffered(1)),
            whole((1, 1, nk, dv, tk)),
        ],
        out_specs=pl.BlockSpec((1, s, MLA_V), lambda bi, h: (bi, 0, h)),
        out_shape=jax.ShapeDtypeStruct((b, s, MLA_HEADS * MLA_V), BF16),
        scratch_shapes=[pltpu.VMEM((2, nk, tk, tq), F32)],
        compiler_params=_cparams(("parallel", "parallel")),
        name="mla_flash",
    )(qt, k, vt)


def _gdn_proj_kernel(xp_ref, x_ref, xn_ref, mod_ref, gpre_ref, wqkv_ref, wg_ref, wab_ref, conv_ref,
                     aneg_ref, dtb_ref, q_ref, k_ref, v_ref, gate_ref, gb_ref, p_ref,
                     *, tiles_per_seq):
    i = pl.program_id(0)
    tm = x_ref.shape[0]
    first = (i % tiles_per_seq) == 0
    last = (i % tiles_per_seq) == tiles_per_seq - 1

    def mod_rows(ref):
        return _modulated(ref[...], mod_ref, gpre_ref, 1)

    h = mod_rows(x_ref)
    hb = h.astype(BF16)
    hb_all = jnp.concatenate([mod_rows(xp_ref), h, mod_rows(xn_ref)], axis=0).astype(BF16)

    def l2n(z):
        return z * lax.rsqrt(jnp.sum(z * z, axis=-1, keepdims=True) + NORM_EPS)

    pad = GDN_CONV // 2
    n_slab = p_ref.shape[0]
    for g0 in range(0, n_slab, GDN_SLABS_PER_DOT):
        p = _dot(hb_all, wqkv_ref[:, g0 * LANES:(g0 + GDN_SLABS_PER_DOT) * LANES])
        for c in range(g0, g0 + GDN_SLABS_PER_DOT):
            cs = slice(c * LANES, (c + 1) * LANES)
            pc = p[:, (c - g0) * LANES:(c - g0 + 1) * LANES]
            p_ref[c, 0:GDN_HALO, :] = jnp.where(first, 0.0, pc[0:GDN_HALO])
            p_ref[c, GDN_HALO:GDN_HALO + tm, :] = pc[GDN_HALO:GDN_HALO + tm]
            p_ref[c, GDN_HALO + tm:, :] = jnp.where(last, 0.0, pc[GDN_HALO + tm:])
            acc = None
            for tap in range(GDN_CONV):
                r0 = GDN_HALO - pad + tap
                term = p_ref[c, r0:r0 + tm, :] * conv_ref[tap:tap + 1, cs]
                acc = term if acc is None else acc + term
            z = acc * jax.nn.sigmoid(acc)
            if c < GDN_HEADS:
                q_ref[:, cs] = l2n(z) * (GDN_DK ** -0.5)
            elif c < 2 * GDN_HEADS:
                k_ref[:, c * LANES - GDN_QK:(c + 1) * LANES - GDN_QK] = l2n(z)
            else:
                v_ref[:, c * LANES - 2 * GDN_QK:(c + 1) * LANES - 2 * GDN_QK] = z
    gate_ref[...] = _dot(hb, wg_ref[...])
    ab = _dot(hb, wab_ref[...])
    z = ab + dtb_ref[...]
    softplus = jnp.maximum(z, 0.0) + jnp.log(1.0 + jnp.exp(-jnp.abs(z)))
    lane = lax.broadcasted_iota(jnp.int32, ab.shape, 1)
    gb_ref[...] = jnp.where(lane < 2 * GDN_HEADS, aneg_ref[...] * softplus, jax.nn.sigmoid(ab))


def _gdn_proj(x, mod, gpre, wqkv, wg, wab, conv, aneg, dtb, *, seq, tm):
    t, d = x.shape
    hb = tm // GDN_HALO
    nblk8 = t // GDN_HALO
    n_conv = wqkv.shape[1]
    return pl.pallas_call(
        functools.partial(_gdn_proj_kernel, tiles_per_seq=seq // tm),
        grid=(t // tm,),
        in_specs=[
            pl.BlockSpec((GDN_HALO, d), lambda i: (jnp.maximum(i * hb - 1, 0), 0)),
            pl.BlockSpec((tm, d), lambda i: (i, 0)),
            pl.BlockSpec((GDN_HALO, d), lambda i: (jnp.minimum((i + 1) * hb, nblk8 - 1), 0)),
            pl.BlockSpec((1, 3 * N_SUB, d), lambda i: ((i * tm) // seq, 0, 0)),
            _const_spec((1, d)),
            _const_spec(wqkv.shape),
            _const_spec(wg.shape),
            _const_spec(wab.shape),
            _const_spec(conv.shape),
            _const_spec(aneg.shape),
            _const_spec(dtb.shape),
        ],
        out_specs=[
            pl.BlockSpec((tm, GDN_QK), lambda i: (i, 0)),
            pl.BlockSpec((tm, GDN_QK), lambda i: (i, 0)),
            pl.BlockSpec((tm, GDN_VW), lambda i: (i, 0)),
            pl.BlockSpec((tm, GDN_VW), lambda i: (i, 0)),
            pl.BlockSpec((tm, LANES), lambda i: (i, 0)),
        ],
        out_shape=[
            jax.ShapeDtypeStruct((t, GDN_QK), F32),
            jax.ShapeDtypeStruct((t, GDN_QK), F32),
            jax.ShapeDtypeStruct((t, GDN_VW), F32),
            jax.ShapeDtypeStruct((t, GDN_VW), F32),
            jax.ShapeDtypeStruct((t, LANES), F32),
        ],
        scratch_shapes=[pltpu.VMEM((n_conv // LANES, tm + 2 * GDN_HALO, LANES), F32)],
        compiler_params=_cparams(("parallel",)),
        name="gdn_proj",
    )(x, x, x, mod, gpre, wqkv, wg, wab, conv, aneg, dtb)


def _gdn_cumsum_kernel(gb_ref, o_ref):
    g = gb_ref[...]
    tm = g.shape[0]
    row = lax.broadcasted_iota(jnp.int32, (tm, tm), 0)
    col = lax.broadcasted_iota(jnp.int32, (tm, tm), 1)
    sh = int(math.log2(GDN_CHUNK))
    same = lax.shift_right_logical(row, sh) == lax.shift_right_logical(col, sh)
    lower = jnp.where(same & (col <= row), 1.0, 0.0).astype(BF16)
    upper = jnp.where(same & (col >= row), 1.0, 0.0).astype(BF16)
    g1 = g.astype(BF16)
    r1 = g - g1.astype(F32)
    g2 = r1.astype(BF16)
    g3 = (r1 - g2.astype(F32)).astype(BF16)
    pre = _dot(lower, g1) + _dot(lower, g2) + _dot(lower, g3)
    suf = _dot(upper, g1) + _dot(upper, g2) + _dot(upper, g3)
    lane = lax.broadcasted_iota(jnp.int32, g.shape, 1)
    o_ref[...] = jnp.where(lane < GDN_HEADS, pre, jnp.where(lane < 2 * GDN_HEADS, suf, g))


def _gdn_cumsum(gb, *, tm):
    t = gb.shape[0]
    return pl.pallas_call(
        _gdn_cumsum_kernel,
        grid=(t // tm,),
        in_specs=[pl.BlockSpec((tm, LANES), lambda i: (i, 0))],
        out_specs=pl.BlockSpec((tm, LANES), lambda i: (i, 0)),
        out_shape=jax.ShapeDtypeStruct((t, LANES), F32),
        compiler_params=_cparams(("parallel",)),
        name="gdn_cumsum",
    )(gb)


def _gdn_delta_kernel(q_ref, k_ref, v_ref, gcol_ref, bcol_ref, grow_ref, o_ref, state_ref,
                      *, reverse):
    c = GDN_CHUNK
    nchunk = q_ref.shape[0] // c

    @pl.when(pl.program_id(1) == 0)
    def _():
        state_ref[...] = jnp.zeros_like(state_ref)

    row = lax.broadcasted_iota(jnp.int32, (c, c), 0)
    col = lax.broadcasted_iota(jnp.int32, (c, c), 1)
    if reverse:
        row, col = col, row
    tri = row >= col
    strict = row > col
    eye = jnp.where(row == col, 1.0, 0.0)
    level_masks = []
    for lv in range(int(math.log2(c))):
        rb = lax.shift_right_logical(row, lv)
        cb = lax.shift_right_logical(col, lv)
        level_masks.append((lax.shift_right_logical(rb, 1) == lax.shift_right_logical(cb, 1))
                           & ((rb & 1) == 1) & ((cb & 1) == 0))
    last_row = 0 if reverse else c - 1

    heads = range(GDN_HEADS)
    order = list(reversed(range(nchunk))) if reverse else list(range(nchunk))
    probs = [(j, h) for j in order for h in heads]
    rows = lambda j: slice(j * c, (j + 1) * c)
    cols = lambda h: slice(h * GDN_DK, (h + 1) * GDN_DK)

    qs = [q_ref[rows(j), cols(h)] for j, h in probs]
    ks = [k_ref[rows(j), cols(h)] for j, h in probs]
    vs = [v_ref[rows(j), cols(h)] for j, h in probs]
    gcbs = [jnp.broadcast_to(gcol_ref[0, rows(j), h:h + 1], (c, GDN_DK)) for j, h in probs]
    betas = [jnp.broadcast_to(bcol_ref[0, rows(j), h:h + 1], (c, GDN_DK)) for j, h in probs]
    n = len(probs)
    decays = []
    for p, (j, h) in enumerate(probs):
        grb = jnp.broadcast_to(grow_ref[0, j, h:h + 1, :], (c, c))
        decays.append(jnp.where(tri, jnp.exp(jnp.where(tri, gcbs[p][:, :c] - grb, 0.0)), 0.0))
    kbs = [ks[p] * betas[p] for p in range(n)]
    a2s = [_dot_nt(jnp.concatenate([kbs[p], qs[p]], axis=0).astype(BF16), ks[p].astype(BF16))
           for p in range(n)]
    ms = [jnp.where(strict, a2s[p][:c] * decays[p], 0.0) for p in range(n)]
    qks = [(a2s[p][c:] * decays[p]).astype(BF16) for p in range(n)]
    tinvs = [eye - jnp.where(level_masks[0], ms[p], 0.0) for p in range(n)]
    for lm in level_masks[1:]:
        tbs = [t.astype(BF16) for t in tinvs]
        tcs = [_dot(tbs[p], jnp.where(lm, ms[p], 0.0).astype(BF16)).astype(BF16) for p in range(n)]
        tinvs = [tinvs[p] - _dot(tcs[p], tbs[p]) for p in range(n)]
    egs = [jnp.exp(g) for g in gcbs]
    sols = [_dot(tinvs[p].astype(BF16),
                 jnp.concatenate([vs[p] * betas[p], kbs[p] * egs[p]], axis=1).astype(BF16))
            for p in range(n)]
    g_lasts = [g[last_row:last_row + 1, :] for g in gcbs]
    k_decs = [(ks[p] * jnp.exp(g_lasts[p] - gcbs[p])).astype(BF16) for p in range(n)]
    wq_lhs = [jnp.concatenate([sols[p][:, GDN_DV:], qs[p] * egs[p]], axis=0).astype(BF16)
              for p in range(n)]

    sts = [state_ref[h] for h in heads]
    for step, j in enumerate(order):
        ps = [step * GDN_HEADS + h for h in heads]
        wqs = [_dot(wq_lhs[ps[h]], sts[h].astype(BF16)) for h in heads]
        vnbs = [(sols[ps[h]][:, :GDN_DV] - wqs[h][:c]).astype(BF16) for h in heads]
        for h in heads:
            o_ref[rows(j), cols(h)] = wqs[h][c:] + _dot(qks[ps[h]], vnbs[h])
        sts = [sts[h] * jnp.exp(g_lasts[ps[h]]) + _dot_tn(k_decs[ps[h]], vnbs[h]) for h in heads]
    for h in heads:
        state_ref[h] = sts[h]


def _gdn_delta(q, k, v, gcol, bcol, grow, *, batch, seq, tb, reverse):
    t = q.shape[0]
    nb = seq // tb
    cpb = tb // GDN_CHUNK

    def blk(bi, i):
        return bi * nb + ((nb - 1 - i) if reverse else i)

    def seq_blk(bi, i):
        return (nb - 1 - i) if reverse else i

    return pl.pallas_call(
        functools.partial(_gdn_delta_kernel, reverse=reverse),
        grid=(batch, nb),
        in_specs=[
            pl.BlockSpec((tb, GDN_QK), lambda bi, i: (blk(bi, i), 0)),
            pl.BlockSpec((tb, GDN_QK), lambda bi, i: (blk(bi, i), 0)),
            pl.BlockSpec((tb, GDN_VW), lambda bi, i: (blk(bi, i), 0)),
            pl.BlockSpec((1, tb, GDN_HEADS), lambda bi, i: (bi, seq_blk(bi, i), 0)),
            pl.BlockSpec((1, tb, GDN_HEADS), lambda bi, i: (bi, seq_blk(bi, i), 0)),
            pl.BlockSpec((1, cpb, GDN_HEADS, GDN_CHUNK), lambda bi, i: (bi, seq_blk(bi, i), 0, 0)),
        ],
        out_specs=pl.BlockSpec((tb, GDN_VW), lambda bi, i: (blk(bi, i), 0)),
        out_shape=jax.ShapeDtypeStruct((t, GDN_VW), F32),
        scratch_shapes=[pltpu.VMEM((GDN_HEADS, GDN_DK, GDN_DV), F32)],
        compiler_params=_cparams(("parallel", "arbitrary")),
        name="gdn_delta_bw" if reverse else "gdn_delta_fw",
    )(q, k, v, gcol, bcol, grow)


def _gdn_out_kernel(of_ref, ob_ref, gate_ref, x_ref, mod_ref, gpost_ref, onorm_ref, w_ref, o_ref,
                    a_ref):
    o = of_ref[...] + ob_ref[...]
    gate = gate_ref[...]
    og = gate * jax.nn.sigmoid(gate)
    for h in range(GDN_HEADS):
        cs = slice(h * GDN_DV, (h + 1) * GDN_DV)
        a_ref[:, cs] = (_rms(o[:, cs], onorm_ref[...]) * og[:, cs]).astype(BF16)
    out = _dot(a_ref[...], w_ref[...])
    o_ref[...] = _residual(x_ref[...], out, mod_ref, gpost_ref, 1, 1.0)


def _gdn_out(o_fw, o_bw, gate, x, mod, gpost, onorm, w, *, seq, tm):
    t, d = x.shape
    return pl.pallas_call(
        _gdn_out_kernel,
        grid=(t // tm,),
        in_specs=[
            pl.BlockSpec((tm, GDN_VW), lambda i: (i, 0)),
            pl.BlockSpec((tm, GDN_VW), lambda i: (i, 0)),
            pl.BlockSpec((tm, GDN_VW), lambda i: (i, 0)),
            pl.BlockSpec((tm, d), lambda i: (i, 0)),
            pl.BlockSpec((1, 3 * N_SUB, d), lambda i: ((i * tm) // seq, 0, 0)),
            _const_spec((1, d)),
            _const_spec(onorm.shape),
            _const_spec(w.shape),
        ],
        out_specs=pl.BlockSpec((tm, d), lambda i: (i, 0)),
        out_shape=jax.ShapeDtypeStruct((t, d), F32),
        scratch_shapes=[pltpu.VMEM((tm, GDN_VW), BF16)],
        compiler_params=_cparams(("parallel",)),
        name="gdn_out",
    )(o_fw, o_bw, gate, x, mod, gpost, onorm, w)


def _fnet_chan_kernel(x_ref, mod_ref, gpre_ref, cs_ref, o_ref):
    hb = _modulated(x_ref[...], mod_ref, gpre_ref, 1).astype(BF16)
    cg = hb.shape[1] // FNET_GROUPS
    for g in range(FNET_GROUPS):
        ab = _dot(hb[:, g * cg:(g + 1) * cg], cs_ref[...])
        o_ref[0, 0, :, g * cg:(g + 1) * cg] = ab[:, :cg]
        o_ref[0, 1, :, g * cg:(g + 1) * cg] = ab[:, cg:]


def _fnet_chan(x, mod, gpre, cs, *, batch, seq, tm):
    t, d = x.shape
    nt = seq // tm
    return pl.pallas_call(
        _fnet_chan_kernel,
        grid=(t // tm,),
        in_specs=[
            pl.BlockSpec((tm, d), lambda i: (i, 0)),
            pl.BlockSpec((1, 3 * N_SUB, d), lambda i: ((i * tm) // seq, 0, 0)),
            _const_spec((1, d)),
            _const_spec(cs.shape),
        ],
        out_specs=pl.BlockSpec((1, 2, tm, d), lambda i: (i // nt, 0, i % nt, 0)),
        out_shape=jax.ShapeDtypeStruct((batch, 2, seq, d), F32),
        compiler_params=_cparams(("parallel",)),
        name="fnet_chan",
    )(x, mod, gpre, cs)


def _fnet_s1_kernel(m1_ref, ab_ref, y_ref):
    _, two, s1, _, rows, d = ab_ref.shape
    ab = ab_ref[0].reshape(two * s1 * rows, d).astype(BF16)
    y = _dot(m1_ref[...], ab)
    y_ref[0] = y.reshape(two, s1, 1, rows, d)


def _fnet_s1(m1, ab):
    batch, _, s1, groups, rows, d = ab.shape
    blk = (1, 2, s1, 1, rows, d)
    return pl.pallas_call(
        _fnet_s1_kernel,
        grid=(batch, groups),
        in_specs=[
            _const_spec(m1.shape),
            pl.BlockSpec(blk, lambda bi, g: (bi, 0, 0, g, 0, 0)),
        ],
        out_specs=pl.BlockSpec(blk, lambda bi, g: (bi, 0, 0, g, 0, 0)),
        out_shape=jax.ShapeDtypeStruct(ab.shape, F32),
        compiler_params=_cparams(("parallel", "parallel")),
        name="fnet_s1",
    )(m1, ab)


def _fnet_s2_kernel(m2_ref, tw_ref, y_ref, f_ref, z_ref):
    n, d = y_ref.shape[2:]
    cos = tw_ref[0]
    sin = tw_ref[1]
    for c0 in range(0, d, LANES):
        yr = y_ref[0, 0, :, c0:c0 + LANES]
        yi = y_ref[0, 1, :, c0:c0 + LANES]
        z_ref[:n, c0:c0 + LANES] = (yr * cos + yi * sin).astype(BF16)
        z_ref[n:, c0:c0 + LANES] = (yi * cos - yr * sin).astype(BF16)
    f = _dot(m2_ref[...], z_ref[...])
    f_ref[0] = f.reshape(f_ref.shape[1:])


def _fnet_s2(m2, tw, y, *, s1, s2):
    batch, _, _, d = y.shape
    n = 8 * s2
    return pl.pallas_call(
        _fnet_s2_kernel,
        grid=(batch, s1 // 8),
        in_specs=[
            _const_spec(m2.shape),
            pl.BlockSpec((2, n, LANES), lambda bi, a: (0, a, 0)),
            pl.BlockSpec((1, 2, n, d), lambda bi, a: (bi, 0, a, 0)),
        ],
        out_specs=pl.BlockSpec((1, s2, 1, 8, d), lambda bi, a: (bi, 0, a, 0, 0)),
        out_shape=jax.ShapeDtypeStruct((batch, s2, s1 // 8, 8, d), F32),
        scratch_shapes=[pltpu.VMEM((2 * n, d), BF16)],
        compiler_params=_cparams(("parallel", "parallel")),
        name="fnet_s2",
    )(m2, tw, y)


def _rope_table(seq):
    half = MLA_ROPE // 2
    pos = jnp.arange(seq, dtype=F32)
    inv_freq = ROPE_THETA ** (-jnp.arange(half, dtype=F32) / half)
    ang = pos[:, None] * inv_freq[None, :]
    cos, sin = jnp.cos(ang), jnp.sin(ang)
    return jnp.concatenate([cos, cos, -sin, sin], axis=1)


def _swap_halves(w):
    half = w.shape[-1] // 2
    return jnp.concatenate([w[..., half:], w[..., :half]], axis=-1)


def _prep_mla(w_down, w_uq, w_ukv):
    d = w_down.shape[0]
    rope = w_down[:, MLA_Q_LORA + MLA_KV_LORA:]
    wd = jnp.concatenate([w_down, _swap_halves(rope)], axis=1).astype(BF16)
    uq = w_uq.reshape(MLA_Q_LORA, MLA_HEADS, MLA_NOPE + MLA_ROPE)
    uq_rope = uq[..., MLA_NOPE:]
    wq = jnp.concatenate([uq, _swap_halves(uq_rope)], axis=-1)
    wq = wq.reshape(MLA_Q_LORA, MLA_HEADS * MLA_QK_PAD).astype(BF16)
    ukv = w_ukv.reshape(MLA_KV_LORA, MLA_HEADS, MLA_NOPE + MLA_V)
    wkv = jnp.concatenate([ukv[..., :MLA_NOPE].reshape(MLA_KV_LORA, -1),
                           ukv[..., MLA_NOPE:].reshape(MLA_KV_LORA, -1)], axis=1).astype(BF16)
    del d
    return wd, wq, wkv


def _fnet_tables(seq, cg):
    def trig(n):
        idx = jnp.arange(n, dtype=jnp.int32)
        ang = ((idx[:, None] * idx[None, :]) % n).astype(F32) * (2.0 * math.pi / n)
        return jnp.cos(ang), jnp.sin(ang)

    cc, sc = trig(cg)
    cs = (jnp.concatenate([cc, sc], axis=1) * (cg ** -0.5)).astype(BF16)
    s2 = min(seq, LANES)
    s1 = seq // s2
    eye8 = jnp.eye(8, dtype=F32)
    c1, sn1 = trig(s1)
    m1 = jnp.block([[c1, -sn1], [-sn1, -c1]]) * (s1 ** -0.5)
    m1 = jnp.kron(m1, eye8).astype(BF16)
    c2, sn2 = trig(s2)
    m2 = jnp.stack([c2, sn2]) * (s2 ** -0.5)
    m2 = jnp.einsum('cts,rq->trcqs', m2, eye8).reshape(8 * s2, 16 * s2).astype(BF16)
    t1 = jnp.arange(s1, dtype=jnp.int32)[:, None]
    k2 = jnp.arange(s2, dtype=jnp.int32)[None, :]
    ang = ((t1 * k2).astype(F32) * (2.0 * math.pi / seq)).reshape(seq)
    tw = jnp.stack([jnp.cos(ang), jnp.sin(ang)])
    tw = jnp.broadcast_to(tw[..., None], (2, seq, LANES))
    return cs, m1, m2, tw


def _pick(n, pref):
    return pref if n % pref == 0 else n


def _trunk(x3, mods, prm):
    batch, seq, d = x3.shape
    t = batch * seq
    x = x3.reshape(t, d)
    tm = _pick(seq, 512)
    row = lambda v: v.reshape(1, -1)
    trig = _rope_table(seq)
    for l in range(mods.shape[0]):
        i = l // 3
        kind = l % 3
        mod = mods[l]
        gpre = prm['norm_pre'][l]
        gpost = prm['norm_post'][l]
        x = _ffn(x, mod, row(gpre[0]), row(gpost[0]), prm['ffn_w_in'][l][0], prm['ffn_w_out'][l][0],
                 sub=0, seq=seq, tm=tm)
        if kind == 0:
            wd, wq, wkv = prm['mla'][i]
            qt, k, vt = _mla_proj(x, mod, row(gpre[1]), trig, wd, row(prm['mla_q_norm'][i]),
                                  row(prm['mla_kv_norm'][i]), wq, wkv, seq=seq, tm=tm)
            o = _flash(qt, k.reshape(batch, seq, -1), vt)
            mixed = (o.reshape(t, -1), prm['mla_w_out'][i], jnp.zeros((1, d), F32))
        elif kind == 1:
            g = prm['gdn'][i]
            q, k, v, gate, gb = _gdn_proj(x, mod, row(gpre[1]), g['wqkv'], g['wg'], g['wab'], g['conv'],
                                          g['aneg'], g['dtb'], seq=seq, tm=_pick(seq, 256))
            gc = _gdn_cumsum(gb, tm=tm)
            nh = GDN_HEADS
            gc3 = gc.reshape(batch, seq, LANES)
            outs = []
            for dr in range(2):
                gcol = gc3[:, :, dr * nh:(dr + 1) * nh]
                bcol = gc3[:, :, 2 * nh + dr * nh:2 * nh + (dr + 1) * nh]
                grow = jnp.swapaxes(gcol.reshape(batch, seq // GDN_CHUNK, GDN_CHUNK, nh), 2, 3)
                outs.append(_gdn_delta(q, k, v, gcol, bcol, grow, batch=batch, seq=seq,
                                       tb=_pick(seq, 256), reverse=bool(dr)))
            x = _gdn_out(outs[0], outs[1], gate, x, mod, row(gpost[1]), row(prm['gdn_o_norm'][i]),
                         g['wout'], seq=seq, tm=tm)
            mixed = None
        else:
            cs, m1, m2, tw = _fnet_tables(seq, d // FNET_GROUPS)
            s2 = min(seq, LANES)
            s1 = seq // s2
            ab = _fnet_chan(x, mod, row(gpre[1]), cs, batch=batch, seq=seq, tm=tm)
            y = _fnet_s1(m1, ab.reshape(batch, 2, s1, s2 // 8, 8, d))
            f = _fnet_s2(m2, tw, y.reshape(batch, 2, seq, d), s1=s1, s2=s2)
            mixed = (f.reshape(t, d), prm['fnet_w_out'][i], row(prm['fnet_b_out'][i]))
        w_in2, w_out2 = prm['ffn_w_in'][l][1], prm['ffn_w_out'][l][1]
        if mixed is None:
            x = _ffn(x, mod, row(gpre[2]), row(gpost[2]), w_in2, w_out2, sub=2, seq=seq, tm=tm)
        else:
            a, w_mix, b_mix = mixed
            x = _mix_ffn(a, x, mod, row(gpost[1]), w_mix, b_mix, row(gpre[2]), row(gpost[2]),
                         w_in2, w_out2, seq=seq, tm=tm)
    return x.reshape(batch, seq, d)


def _prep_gdn(w_in, conv, a_log, dt_bias, w_out):
    n_conv = 2 * GDN_QK + GDN_VW
    wab = jnp.pad(w_in[:, n_conv + GDN_VW:], ((0, 0), (0, LANES - 4 * GDN_HEADS)))
    pad16 = lambda v: jnp.pad(v.reshape(1, -1).astype(F32), ((0, 0), (0, LANES - 2 * GDN_HEADS)))
    return {
        'wqkv': w_in[:, :n_conv].astype(BF16),
        'wg': w_in[:, n_conv:n_conv + GDN_VW].astype(BF16),
        'wab': wab.astype(BF16),
        'conv': conv.astype(F32),
        'aneg': pad16(-jnp.exp(a_log.astype(F32))),
        'dtb': pad16(dt_bias),
        'wout': w_out.astype(BF16),
    }


def kernel(x_prompt, x_sample, c_prompt, c_sample, w_ada, b_ada, norm_pre, norm_post, ffn_w_in, ffn_w_out, mla_w_down, mla_q_norm, mla_kv_norm, mla_w_uq, mla_w_ukv, mla_w_out, gdn_w_in, gdn_conv, gdn_a_log, gdn_dt_bias, gdn_o_norm, gdn_w_out, fnet_w_out, fnet_b_out):
    d = x_prompt.shape[-1]
    prm = {
        'norm_pre': norm_pre, 'norm_post': norm_post,
        'ffn_w_in': ffn_w_in.astype(BF16), 'ffn_w_out': ffn_w_out.astype(BF16),
        'mla': [_prep_mla(mla_w_down[i], mla_w_uq[i], mla_w_ukv[i]) for i in range(mla_w_down.shape[0])],
        'mla_q_norm': mla_q_norm, 'mla_kv_norm': mla_kv_norm, 'mla_w_out': mla_w_out.astype(BF16),
        'gdn': [_prep_gdn(gdn_w_in[i], gdn_conv[i], gdn_a_log[i], gdn_dt_bias[i], gdn_w_out[i])
                for i in range(gdn_w_in.shape[0])],
        'gdn_o_norm': gdn_o_norm,
        'fnet_w_out': fnet_w_out.astype(BF16), 'fnet_b_out': fnet_b_out,
    }
    nbp = c_prompt.shape[0]
    c_all = jnp.concatenate([c_prompt, c_sample], axis=0)
    mods = _ada_mod(c_all, w_ada, b_ada).reshape(w_ada.shape[0], c_all.shape[0], 3 * N_SUB, d)
    y_prompt = _trunk(x_prompt, mods[:, :nbp], prm)
    y_sample = _trunk(x_sample, mods[:, nbp:], prm)
    return (y_prompt, y_sample)
```

```python
import functools
import math

import jax
import jax.numpy as jnp
from jax import lax
from jax.experimental import pallas as pl
from jax.experimental.pallas import tpu as pltpu

F32 = jnp.float32
BF16 = jnp.bfloat16

N_SUB = 3
D_FF = 2816
NORM_EPS = 1e-6

MLA_HEADS = 8
MLA_Q_LORA = 384
MLA_KV_LORA = 256
MLA_NOPE = 128
MLA_ROPE = 64
MLA_V = 128
ROPE_THETA = 10000.0
MLA_SCALE = (MLA_NOPE + MLA_ROPE) ** -0.5
MLA_QK_PAD = 256

GDN_HEADS = 8
GDN_DK = 128
GDN_DV = 128
GDN_QK = GDN_HEADS * GDN_DK
GDN_VW = GDN_HEADS * GDN_DV
GDN_CONV = 5
GDN_CHUNK = 64
GDN_HALO = 8
GDN_SLABS_PER_DOT = 4

FNET_GROUPS = 8

LANES = 128
VMEM_LIMIT = 56 * 1024 * 1024


def _cparams(sem):
    return pltpu.CompilerParams(dimension_semantics=sem, vmem_limit_bytes=VMEM_LIMIT)


def _const_spec(shape):
    nd = len(shape)
    return pl.BlockSpec(shape, lambda *_: (0,) * nd, pipeline_mode=pl.Buffered(1))


def _rms(x, g):
    ms = jnp.mean(x * x, axis=-1, keepdims=True)
    return x * lax.rsqrt(ms + NORM_EPS) * g


def _modulated(x, mod_ref, gpre_ref, sub):
    shift = mod_ref[0, 3 * sub:3 * sub + 1, :]
    scale = mod_ref[0, 3 * sub + 1:3 * sub + 2, :]
    return _rms(x, gpre_ref[...]) * (1.0 + scale) + shift


def _residual(x, out, mod_ref, gpost_ref, sub, weight):
    gate = mod_ref[0, 3 * sub + 2:3 * sub + 3, :]
    return x + (weight * gate) * _rms(out, gpost_ref[...])


def _dot(a, b):
    return jnp.dot(a, b, preferred_element_type=F32)


def _dot_nt(a, b):
    return lax.dot_general(a, b, (((1,), (1,)), ((), ())), preferred_element_type=F32)


def _dot_tn(a, b):
    return lax.dot_general(a, b, (((0,), (0,)), ((), ())), preferred_element_type=F32)


def _ada_kernel(c_ref, w_ref, b_ref, o_ref):
    c = c_ref[...]
    sc = (c * jax.nn.sigmoid(c)).astype(BF16)
    o_ref[0] = _dot(sc, w_ref[0].astype(BF16)) + b_ref[0]


def _ada_mod(c_all, w_ada, b_ada):
    nb, d = c_all.shape
    depth, _, n_out = w_ada.shape
    tn = n_out // 8
    return pl.pallas_call(
        _ada_kernel,
        grid=(depth, n_out // tn),
        in_specs=[
            pl.BlockSpec((nb, d), lambda l, j: (0, 0)),
            pl.BlockSpec((1, d, tn), lambda l, j: (l, 0, j)),
            pl.BlockSpec((1, 1, tn), lambda l, j: (l, 0, j)),
        ],
        out_specs=pl.BlockSpec((1, nb, tn), lambda l, j: (l, 0, j)),
        out_shape=jax.ShapeDtypeStruct((depth, nb, n_out), F32),
        compiler_params=_cparams(("arbitrary", "arbitrary")),
        name="ada_mod",
    )(c_all, w_ada, b_ada.reshape(depth, 1, n_out))


FFN_CHUNK = 256


FFN_SPLIT = 2


def _ffn_step(xs, mod_ref, gpre_ref, gpost_ref, win_ref, wout_ref, act_ref, sub):
    n = xs[0].shape[0]
    hbs = [_modulated(x, mod_ref, gpre_ref, sub).astype(BF16) for x in xs]
    for c0 in range(0, D_FF, FFN_CHUNK):
        for r, hb in enumerate(hbs):
            g = _dot(hb, win_ref[:, c0:c0 + FFN_CHUNK])
            u = _dot(hb, win_ref[:, D_FF + c0:D_FF + c0 + FFN_CHUNK])
            act_ref[r * n:(r + 1) * n, c0:c0 + FFN_CHUNK] = (g * jax.nn.sigmoid(g) * u).astype(BF16)
    outs = [_dot(act_ref[r * n:(r + 1) * n, :], wout_ref[...]) for r in range(len(xs))]
    return [_residual(x, out, mod_ref, gpost_ref, sub, 0.5) for x, out in zip(xs, outs)]


def _row_groups(ref):
    n = ref.shape[0] // FFN_SPLIT
    return [ref[r * n:(r + 1) * n, :] for r in range(FFN_SPLIT)]


def _store_row_groups(ref, vals):
    n = ref.shape[0] // FFN_SPLIT
    for r, v in enumerate(vals):
        ref[r * n:(r + 1) * n, :] = v


def _ffn_kernel(x_ref, mod_ref, gpre_ref, gpost_ref, win_ref, wout_ref, o_ref, act_ref, *, sub):
    _store_row_groups(o_ref, _ffn_step(_row_groups(x_ref), mod_ref, gpre_ref, gpost_ref,
                                        win_ref, wout_ref, act_ref, sub))


def _mix_ffn_kernel(a_ref, x_ref, mod_ref, gpost_mix_ref, wmix_ref, bmix_ref,
                    gpre_ref, gpost_ref, win_ref, wout_ref, o_ref, act_ref):
    xs = [_residual(x, _dot(a.astype(BF16), wmix_ref[...]) + bmix_ref[...], mod_ref, gpost_mix_ref, 1, 1.0)
          for a, x in zip(_row_groups(a_ref), _row_groups(x_ref))]
    _store_row_groups(o_ref, _ffn_step(xs, mod_ref, gpre_ref, gpost_ref, win_ref, wout_ref, act_ref, 2))


def _mix_ffn(a, x, mod, gpost_mix, w_mix, b_mix, gpre, gpost, w_in, w_out, *, seq, tm):
    t, d = x.shape
    return pl.pallas_call(
        _mix_ffn_kernel,
        grid=(t // tm,),
        in_specs=[
            pl.BlockSpec((tm, a.shape[1]), lambda i: (i, 0)),
            pl.BlockSpec((tm, d), lambda i: (i, 0)),
            pl.BlockSpec((1, 3 * N_SUB, d), lambda i: ((i * tm) // seq, 0, 0)),
            _const_spec((1, d)),
            _const_spec(w_mix.shape),
            _const_spec((1, d)),
            _const_spec((1, d)),
            _const_spec((1, d)),
            _const_spec(w_in.shape),
            _const_spec(w_out.shape),
        ],
        out_specs=pl.BlockSpec((tm, d), lambda i: (i, 0)),
        out_shape=jax.ShapeDtypeStruct((t, d), F32),
        scratch_shapes=[pltpu.VMEM((tm, D_FF), BF16)],
        compiler_params=_cparams(("parallel",)),
        name="mix_ffn",
    )(a, x, mod, gpost_mix, w_mix, b_mix, gpre, gpost, w_in, w_out)


def _ffn(x, mod, gpre, gpost, w_in, w_out, *, sub, seq, tm):
    t, d = x.shape
    return pl.pallas_call(
        functools.partial(_ffn_kernel, sub=sub),
        grid=(t // tm,),
        in_specs=[
            pl.BlockSpec((tm, d), lambda i: (i, 0)),
            pl.BlockSpec((1, 3 * N_SUB, d), lambda i: ((i * tm) // seq, 0, 0)),
            _const_spec((1, d)),
            _const_spec((1, d)),
            _const_spec(w_in.shape),
            _const_spec(w_out.shape),
        ],
        out_specs=pl.BlockSpec((tm, d), lambda i: (i, 0)),
        out_shape=jax.ShapeDtypeStruct((t, d), F32),
        scratch_shapes=[pltpu.VMEM((tm, D_FF), BF16)],
        compiler_params=_cparams(("parallel",)),
        name="ffn",
    )(x, mod, gpre, gpost, w_in, w_out)


def _rope_pair(t):
    return t + pltpu.roll(t, 64, axis=1)


def _mla_proj_kernel(x_ref, mod_ref, gpre_ref, trig_ref, wd_ref, qn_ref, kvn_ref, wq_ref, wkv_ref,
                     qt_ref, k_ref, vt_ref):
    x = x_ref[...]
    hb = _modulated(x, mod_ref, gpre_ref, 1).astype(BF16)
    down = _dot(hb, wd_ref[...])
    cq = _rms(down[:, :MLA_Q_LORA], qn_ref[...]).astype(BF16)
    ckv = _rms(down[:, MLA_Q_LORA:MLA_Q_LORA + MLA_KV_LORA], kvn_ref[...]).astype(BF16)
    trig = trig_ref[...]
    lane = lax.broadcasted_iota(jnp.int32, trig.shape, 1)
    k_rope = _rope_pair(down[:, MLA_Q_LORA + MLA_KV_LORA:] * trig)
    k_rope = jnp.where(lane < MLA_ROPE, k_rope, 0.0).astype(BF16)
    q = _dot(cq, wq_ref[...])
    kv = _dot(ckv, wkv_ref[...])
    qs = MLA_SCALE * math.log2(math.e)
    for h in range(MLA_HEADS):
        c0 = h * MLA_QK_PAD
        q_rope = _rope_pair(q[:, c0 + LANES:c0 + 2 * LANES] * trig)
        qt_ref[0, h, 0, :LANES, :] = (q[:, c0:c0 + LANES] * qs).T.astype(BF16)
        qt_ref[0, h, 0, LANES:, :] = (q_rope * qs).T.astype(BF16)
        k_ref[:, c0:c0 + LANES] = kv[:, h * MLA_NOPE:(h + 1) * MLA_NOPE].astype(BF16)
        k_ref[:, c0 + LANES:c0 + 2 * LANES] = k_rope
        v0 = MLA_HEADS * MLA_NOPE + h * MLA_V
        vt_ref[0, h, 0] = kv[:, v0:v0 + MLA_V].T.astype(BF16)


def _mla_proj(x, mod, gpre, trig, wd, qn, kvn, wq, wkv, *, seq, tm):
    t, d = x.shape
    nq = MLA_HEADS * MLA_QK_PAD
    nt = seq // tm
    return pl.pallas_call(
        _mla_proj_kernel,
        grid=(t // tm,),
        in_specs=[
            pl.BlockSpec((tm, d), lambda i: (i, 0)),
            pl.BlockSpec((1, 3 * N_SUB, d), lambda i: ((i * tm) // seq, 0, 0)),
            _const_spec((1, d)),
            pl.BlockSpec((tm, LANES), lambda i: (i % (seq // tm), 0)),
            _const_spec(wd.shape),
            _const_spec(qn.shape),
            _const_spec(kvn.shape),
            _const_spec(wq.shape),
            _const_spec(wkv.shape),
        ],
        out_specs=[
            pl.BlockSpec((1, MLA_HEADS, 1, MLA_QK_PAD, tm), lambda i: (i // nt, 0, i % nt, 0, 0)),
            pl.BlockSpec((tm, nq), lambda i: (i, 0)),
            pl.BlockSpec((1, MLA_HEADS, 1, MLA_V, tm), lambda i: (i // nt, 0, i % nt, 0, 0)),
        ],
        out_shape=[
            jax.ShapeDtypeStruct((t // seq, MLA_HEADS, nt, MLA_QK_PAD, tm), BF16),
            jax.ShapeDtypeStruct((t, nq), BF16),
            jax.ShapeDtypeStruct((t // seq, MLA_HEADS, nt, MLA_V, tm), BF16),
        ],
        compiler_params=_cparams(("parallel",)),
        name="mla_proj",
    )(x, mod, gpre, trig, wd, qn, kvn, wq, wkv)


def _flash_kernel(qt_ref, k_ref, vt_ref, o_ref, s_ref):
    nq, _, tq = qt_ref.shape[2:]
    nk, _, tk = vt_ref.shape[2:]
    unroll = next((u for u in (8, 4, 2) if nk % u == 0 and nk >= 2 * u), 1)
    cmax0 = jnp.full((8, tq), -1e30, F32)
    acc0 = (jnp.zeros((MLA_V, tq), F32), jnp.zeros((8, tq), F32))

    def score_part(j, slot, qt, cmax):
        r0 = pl.multiple_of(j * tk, tk)
        s = _dot(k_ref[0, pl.ds(r0, tk), :], qt)
        s_ref[slot, j] = s
        return jnp.maximum(cmax, jnp.max(s.reshape(tk // 8, 8, tq), axis=0))

    def value_part(j, slot, m, acc):
        p = jnp.exp2((s_ref[slot, j] - m).astype(BF16))
        o_acc, l_acc = acc
        l_acc = l_acc + jnp.sum(p.astype(F32).reshape(tk // 8, 8, tq), axis=0)
        return o_acc + _dot(vt_ref[0, 0, j], p), l_acc

    def scores_only(i, slot):
        qt = qt_ref[0, 0, i]
        cmax = lax.fori_loop(0, nk, lambda j, c: score_part(j, slot, qt, c), cmax0, unroll=unroll)
        return jnp.max(cmax, axis=0, keepdims=True)

    def values_only(slot, m):
        return lax.fori_loop(0, nk, lambda j, a: value_part(j, slot, m, a), acc0, unroll=unroll)

    def both(i_next, cur, m):
        qt = qt_ref[0, 0, i_next]

        def body(j, carry):
            cmax, acc = carry
            return score_part(j, 1 - cur, qt, cmax), value_part(j, cur, m, acc)

        cmax, acc = lax.fori_loop(0, nk, body, (cmax0, acc0), unroll=unroll)
        return jnp.max(cmax, axis=0, keepdims=True), acc

    def emit(i, acc):
        o_t = acc[0] / jnp.sum(acc[1], axis=0, keepdims=True)
        o_ref[0, pl.ds(pl.multiple_of(i * tq, tq), tq), :] = o_t.T.astype(o_ref.dtype)

    m = scores_only(0, 0)
    if nq == 1:
        emit(0, values_only(0, m))
        return
    assert nq % 2 == 0

    def pair(ip, m):
        i = 2 * ip
        m_odd, acc = both(i + 1, 0, m)
        emit(i, acc)
        m_even, acc = both(i + 2, 1, m_odd)
        emit(i + 1, acc)
        return m_even

    m = lax.fori_loop(0, nq // 2 - 1, pair, m)
    m_last, acc = both(nq - 1, 0, m)
    emit(nq - 2, acc)
    emit(nq - 1, values_only(1, m_last))


def _flash(qt, k, vt):
    b, s, _ = k.shape
    _, _, nq, _, tq = qt.shape
    _, _, nk, dv, tk = vt.shape
    whole = lambda shape: pl.BlockSpec(shape, lambda bi, h: (bi, h) + (0,) * (len(shape) - 2),
                                       pipeline_mode=pl.Buffered(1))
    return pl.pallas_call(
        _flash_kernel,
        grid=(b, MLA_HEADS),
        in_specs=[
            whole((1, 1, nq, MLA_QK_PAD, tq)),
            pl.BlockSpec((1, s, MLA_QK_PAD), lambda bi, h: (bi, 0, h), pipeline_mode=pl.Buffered(1)),
            whole((1, 1, nk, dv, tk)),
        ],
        out_specs=pl.BlockSpec((1, s, MLA_V), lambda bi, h: (bi, 0, h)),
        out_shape=jax.ShapeDtypeStruct((b, s, MLA_HEADS * MLA_V), BF16),
        scratch_shapes=[pltpu.VMEM((2, nk, tk, tq), F32)],
        compiler_params=_cparams(("parallel", "parallel")),
        name="mla_flash",
    )(qt, k, vt)


def _gdn_proj_kernel(xp_ref, x_ref, xn_ref, mod_ref, gpre_ref, wqkv_ref, wg_ref, wab_ref, conv_ref,
                     aneg_ref, dtb_ref, q_ref, k_ref, v_ref, gate_ref, gb_ref, p_ref,
                     *, tiles_per_seq):
    i = pl.program_id(0)
    tm = x_ref.shape[0]
    first = (i % tiles_per_seq) == 0
    last = (i % tiles_per_seq) == tiles_per_seq - 1

    def mod_rows(ref):
        return _modulated(ref[...], mod_ref, gpre_ref, 1)

    h = mod_rows(x_ref)
    hb = h.astype(BF16)
    hb_all = jnp.concatenate([mod_rows(xp_ref), h, mod_rows(xn_ref)], axis=0).astype(BF16)

    def l2n(z):
        return z * lax.rsqrt(jnp.sum(z * z, axis=-1, keepdims=True) + NORM_EPS)

    pad = GDN_CONV // 2
    n_slab = p_ref.shape[0]
    for g0 in range(0, n_slab, GDN_SLABS_PER_DOT):
        p = _dot(hb_all, wqkv_ref[:, g0 * LANES:(g0 + GDN_SLABS_PER_DOT) * LANES])
        for c in range(g0, g0 + GDN_SLABS_PER_DOT):
            cs = slice(c * LANES, (c + 1) * LANES)
            pc = p[:, (c - g0) * LANES:(c - g0 + 1) * LANES]
            p_ref[c, 0:GDN_HALO, :] = jnp.where(first, 0.0, pc[0:GDN_HALO])
            p_ref[c, GDN_HALO:GDN_HALO + tm, :] = pc[GDN_HALO:GDN_HALO + tm]
            p_ref[c, GDN_HALO + tm:, :] = jnp.where(last, 0.0, pc[GDN_HALO + tm:])
            acc = None
            for tap in range(GDN_CONV):
                r0 = GDN_HALO - pad + tap
                term = p_ref[c, r0:r0 + tm, :] * conv_ref[tap:tap + 1, cs]
                acc = term if acc is None else acc + term
            z = acc * jax.nn.sigmoid(acc)
            if c < GDN_HEADS:
                q_ref[:, cs] = l2n(z) * (GDN_DK ** -0.5)
            elif c < 2 * GDN_HEADS:
                k_ref[:, c * LANES - GDN_QK:(c + 1) * LANES - GDN_QK] = l2n(z)
            else:
                v_ref[:, c * LANES - 2 * GDN_QK:(c + 1) * LANES - 2 * GDN_QK] = z
    gate_ref[...] = _dot(hb, wg_ref[...])
    ab = _dot(hb, wab_ref[...])
    z = ab + dtb_ref[...]
    softplus = jnp.maximum(z, 0.0) + jnp.log(1.0 + jnp.exp(-jnp.abs(z)))
    lane = lax.broadcasted_iota(jnp.int32, ab.shape, 1)
    gb_ref[...] = jnp.where(lane < 2 * GDN_HEADS, aneg_ref[...] * softplus, jax.nn.sigmoid(ab))


def _gdn_proj(x, mod, gpre, wqkv, wg, wab, conv, aneg, dtb, *, seq, tm):
    t, d = x.shape
    hb = tm // GDN_HALO
    nblk8 = t // GDN_HALO
    n_conv = wqkv.shape[1]
    return pl.pallas_call(
        functools.partial(_gdn_proj_kernel, tiles_per_seq=seq // tm),
        grid=(t // tm,),
        in_specs=[
            pl.BlockSpec((GDN_HALO, d), lambda i: (jnp.maximum(i * hb - 1, 0), 0)),
            pl.BlockSpec((tm, d), lambda i: (i, 0)),
            pl.BlockSpec((GDN_HALO, d), lambda i: (jnp.minimum((i + 1) * hb, nblk8 - 1), 0)),
            pl.BlockSpec((1, 3 * N_SUB, d), lambda i: ((i * tm) // seq, 0, 0)),
            _const_spec((1, d)),
            _const_spec(wqkv.shape),
            _const_spec(wg.shape),
            _const_spec(wab.shape),
            _const_spec(conv.shape),
            _const_spec(aneg.shape),
            _const_spec(dtb.shape),
        ],
        out_specs=[
            pl.BlockSpec((tm, GDN_QK), lambda i: (i, 0)),
            pl.BlockSpec((tm, GDN_QK), lambda i: (i, 0)),
            pl.BlockSpec((tm, GDN_VW), lambda i: (i, 0)),
            pl.BlockSpec((tm, GDN_VW), lambda i: (i, 0)),
            pl.BlockSpec((tm, LANES), lambda i: (i, 0)),
        ],
        out_shape=[
            jax.ShapeDtypeStruct((t, GDN_QK), F32),
            jax.ShapeDtypeStruct((t, GDN_QK), F32),
            jax.ShapeDtypeStruct((t, GDN_VW), F32),
            jax.ShapeDtypeStruct((t, GDN_VW), F32),
            jax.ShapeDtypeStruct((t, LANES), F32),
        ],
        scratch_shapes=[pltpu.VMEM((n_conv // LANES, tm + 2 * GDN_HALO, LANES), F32)],
        compiler_params=_cparams(("parallel",)),
        name="gdn_proj",
    )(x, x, x, mod, gpre, wqkv, wg, wab, conv, aneg, dtb)


def _gdn_cumsum_kernel(gb_ref, o_ref):
    g = gb_ref[...]
    tm = g.shape[0]
    row = lax.broadcasted_iota(jnp.int32, (tm, tm), 0)
    col = lax.broadcasted_iota(jnp.int32, (tm, tm), 1)
    sh = int(math.log2(GDN_CHUNK))
    same = lax.shift_right_logical(row, sh) == lax.shift_right_logical(col, sh)
    lower = jnp.where(same & (col <= row), 1.0, 0.0).astype(BF16)
    upper = jnp.where(same & (col >= row), 1.0, 0.0).astype(BF16)
    g1 = g.astype(BF16)
    r1 = g - g1.astype(F32)
    g2 = r1.astype(BF16)
    g3 = (r1 - g2.astype(F32)).astype(BF16)
    pre = _dot(lower, g1) + _dot(lower, g2) + _dot(lower, g3)
    suf = _dot(upper, g1) + _dot(upper, g2) + _dot(upper, g3)
    lane = lax.broadcasted_iota(jnp.int32, g.shape, 1)
    o_ref[...] = jnp.where(lane < GDN_HEADS, pre, jnp.where(lane < 2 * GDN_HEADS, suf, g))


def _gdn_cumsum(gb, *, tm):
    t = gb.shape[0]
    return pl.pallas_call(
        _gdn_cumsum_kernel,
        grid=(t // tm,),
        in_specs=[pl.BlockSpec((tm, LANES), lambda i: (i, 0))],
        out_specs=pl.BlockSpec((tm, LANES), lambda i: (i, 0)),
        out_shape=jax.ShapeDtypeStruct((t, LANES), F32),
        compiler_params=_cparams(("parallel",)),
        name="gdn_cumsum",
    )(gb)


def _gdn_delta_kernel(q_ref, k_ref, v_ref, gcol_ref, bcol_ref, grow_ref, o_ref, state_ref,
                      *, reverse):
    c = GDN_CHUNK
    nchunk = q_ref.shape[0] // c

    @pl.when(pl.program_id(1) == 0)
    def _():
        state_ref[...] = jnp.zeros_like(state_ref)

    row = lax.broadcasted_iota(jnp.int32, (c, c), 0)
    col = lax.broadcasted_iota(jnp.int32, (c, c), 1)
    if reverse:
        row, col = col, row
    tri = row >= col
    strict = row > col
    eye = jnp.where(row == col, 1.0, 0.0)
    level_masks = []
    for lv in range(int(math.log2(c))):
        rb = lax.shift_right_logical(row, lv)
        cb = lax.shift_right_logical(col, lv)
        level_masks.append((lax.shift_right_logical(rb, 1) == lax.shift_right_logical(cb, 1))
                           & ((rb & 1) == 1) & ((cb & 1) == 0))
    last_row = 0 if reverse else c - 1

    heads = range(GDN_HEADS)
    order = list(reversed(range(nchunk))) if reverse else list(range(nchunk))
    probs = [(j, h) for j in order for h in heads]
    rows = lambda j: slice(j * c, (j + 1) * c)
    cols = lambda h: slice(h * GDN_DK, (h + 1) * GDN_DK)

    qs = [q_ref[rows(j), cols(h)] for j, h in probs]
    ks = [k_ref[rows(j), cols(h)] for j, h in probs]
    vs = [v_ref[rows(j), cols(h)] for j, h in probs]
    gcbs = [jnp.broadcast_to(gcol_ref[0, rows(j), h:h + 1], (c, GDN_DK)) for j, h in probs]
    betas = [jnp.broadcast_to(bcol_ref[0, rows(j), h:h + 1], (c, GDN_DK)) for j, h in probs]
    n = len(probs)
    decays = []
    for p, (j, h) in enumerate(probs):
        grb = jnp.broadcast_to(grow_ref[0, j, h:h + 1, :], (c, c))
        decays.append(jnp.where(tri, jnp.exp(jnp.where(tri, gcbs[p][:, :c] - grb, 0.0)), 0.0))
    kbs = [ks[p] * betas[p] for p in range(n)]
    a2s = [_dot_nt(jnp.concatenate([kbs[p], qs[p]], axis=0).astype(BF16), ks[p].astype(BF16))
           for p in range(n)]
    ms = [jnp.where(strict, a2s[p][:c] * decays[p], 0.0) for p in range(n)]
    qks = [(a2s[p][c:] * decays[p]).astype(BF16) for p in range(n)]
    tinvs = [eye - jnp.where(level_masks[0], ms[p], 0.0) for p in range(n)]
    for lm in level_masks[1:]:
        tbs = [t.astype(BF16) for t in tinvs]
        tcs = [_dot(tbs[p], jnp.where(lm, ms[p], 0.0).astype(BF16)).astype(BF16) for p in range(n)]
        tinvs = [tinvs[p] - _dot(tcs[p], tbs[p]) for p in range(n)]
    egs = [jnp.exp(g) for g in gcbs]
    sols = [_dot(tinvs[p].astype(BF16),
                 jnp.concatenate([vs[p] * betas[p], kbs[p] * egs[p]], axis=1).astype(BF16))
            for p in range(n)]
    g_lasts = [g[last_row:last_row + 1, :] for g in gcbs]
    k_decs = [(ks[p] * jnp.exp(g_lasts[p] - gcbs[p])).astype(BF16) for p in range(n)]
    wq_lhs = [jnp.concatenate([sols[p][:, GDN_DV:], qs[p] * egs[p]], axis=0).astype(BF16)
              for p in range(n)]

    sts = [state_ref[h] for h in heads]
    for step, j in enumerate(order):
        ps = [step * GDN_HEADS + h for h in heads]
        wqs = [_dot(wq_lhs[ps[h]], sts[h].astype(BF16)) for h in heads]
        vnbs = [(sols[ps[h]][:, :GDN_DV] - wqs[h][:c]).astype(BF16) for h in heads]
        for h in heads:
            o_ref[rows(j), cols(h)] = wqs[h][c:] + _dot(qks[ps[h]], vnbs[h])
        sts = [sts[h] * jnp.exp(g_lasts[ps[h]]) + _dot_tn(k_decs[ps[h]], vnbs[h]) for h in heads]
    for h in heads:
        state_ref[h] = sts[h]


def _gdn_delta(q, k, v, gcol, bcol, grow, *, batch, seq, tb, reverse):
    t = q.shape[0]
    nb = seq // tb
    cpb = tb // GDN_CHUNK

    def blk(bi, i):
        return bi * nb + ((nb - 1 - i) if reverse else i)

    def seq_blk(bi, i):
        return (nb - 1 - i) if reverse else i

    return pl.pallas_call(
        functools.partial(_gdn_delta_kernel, reverse=reverse),
        grid=(batch, nb),
        in_specs=[
            pl.BlockSpec((tb, GDN_QK), lambda bi, i: (blk(bi, i), 0)),
            pl.BlockSpec((tb, GDN_QK), lambda bi, i: (blk(bi, i), 0)),
            pl.BlockSpec((tb, GDN_VW), lambda bi, i: (blk(bi, i), 0)),
            pl.BlockSpec((1, tb, GDN_HEADS), lambda bi, i: (bi, seq_blk(bi, i), 0)),
            pl.BlockSpec((1, tb, GDN_HEADS), lambda bi, i: (bi, seq_blk(bi, i), 0)),
            pl.BlockSpec((1, cpb, GDN_HEADS, GDN_CHUNK), lambda bi, i: (bi, seq_blk(bi, i), 0, 0)),
        ],
        out_specs=pl.BlockSpec((tb, GDN_VW), lambda bi, i: (blk(bi, i), 0)),
        out_shape=jax.ShapeDtypeStruct((t, GDN_VW), F32),
        scratch_shapes=[pltpu.VMEM((GDN_HEADS, GDN_DK, GDN_DV), F32)],
        compiler_params=_cparams(("parallel", "arbitrary")),
        name="gdn_delta_bw" if reverse else "gdn_delta_fw",
    )(q, k, v, gcol, bcol, grow)


def _gdn_out_kernel(of_ref, ob_ref, gate_ref, x_ref, mod_ref, gpost_ref, onorm_ref, w_ref, o_ref,
                    a_ref):
    o = of_ref[...] + ob_ref[...]
    gate = gate_ref[...]
    og = gate * jax.nn.sigmoid(gate)
    for h in range(GDN_HEADS):
        cs = slice(h * GDN_DV, (h + 1) * GDN_DV)
        a_ref[:, cs] = (_rms(o[:, cs], onorm_ref[...]) * og[:, cs]).astype(BF16)
    out = _dot(a_ref[...], w_ref[...])
    o_ref[...] = _residual(x_ref[...], out, mod_ref, gpost_ref, 1, 1.0)


def _gdn_out(o_fw, o_bw, gate, x, mod, gpost, onorm, w, *, seq, tm):
    t, d = x.shape
    return pl.pallas_call(
        _gdn_out_kernel,
        grid=(t // tm,),
        in_specs=[
            pl.BlockSpec((tm, GDN_VW), lambda i: (i, 0)),
            pl.BlockSpec((tm, GDN_VW), lambda i: (i, 0)),
            pl.BlockSpec((tm, GDN_VW), lambda i: (i, 0)),
            pl.BlockSpec((tm, d), lambda i: (i, 0)),
            pl.BlockSpec((1, 3 * N_SUB, d), lambda i: ((i * tm) // seq, 0, 0)),
            _const_spec((1, d)),
            _const_spec(onorm.shape),
            _const_spec(w.shape),
        ],
        out_specs=pl.BlockSpec((tm, d), lambda i: (i, 0)),
        out_shape=jax.ShapeDtypeStruct((t, d), F32),
        scratch_shapes=[pltpu.VMEM((tm, GDN_VW), BF16)],
        compiler_params=_cparams(("parallel",)),
        name="gdn_out",
    )(o_fw, o_bw, gate, x, mod, gpost, onorm, w)


def _fnet_chan_kernel(x_ref, mod_ref, gpre_ref, cs_ref, o_ref):
    hb = _modulated(x_ref[...], mod_ref, gpre_ref, 1).astype(BF16)
    cg = hb.shape[1] // FNET_GROUPS
    for g in range(FNET_GROUPS):
        ab = _dot(hb[:, g * cg:(g + 1) * cg], cs_ref[...])
        o_ref[0, 0, :, g * cg:(g + 1) * cg] = ab[:, :cg]
        o_ref[0, 1, :, g * cg:(g + 1) * cg] = ab[:, cg:]


def _fnet_chan(x, mod, gpre, cs, *, batch, seq, tm):
    t, d = x.shape
    nt = seq // tm
    return pl.pallas_call(
        _fnet_chan_kernel,
        grid=(t // tm,),
        in_specs=[
            pl.BlockSpec((tm, d), lambda i: (i, 0)),
            pl.BlockSpec((1, 3 * N_SUB, d), lambda i: ((i * tm) // seq, 0, 0)),
            _const_spec((1, d)),
            _const_spec(cs.shape),
        ],
        out_specs=pl.BlockSpec((1, 2, tm, d), lambda i: (i // nt, 0, i % nt, 0)),
        out_shape=jax.ShapeDtypeStruct((batch, 2, seq, d), F32),
        compiler_params=_cparams(("parallel",)),
        name="fnet_chan",
    )(x, mod, gpre, cs)


def _fnet_s1_kernel(m1_ref, ab_ref, y_ref):
    _, two, s1, _, rows, d = ab_ref.shape
    ab = ab_ref[0].reshape(two * s1 * rows, d).astype(BF16)
    y = _dot(m1_ref[...], ab)
    y_ref[0] = y.reshape(two, s1, 1, rows, d)


def _fnet_s1(m1, ab):
    batch, _, s1, groups, rows, d = ab.shape
    blk = (1, 2, s1, 1, rows, d)
    return pl.pallas_call(
        _fnet_s1_kernel,
        grid=(batch, groups),
        in_specs=[
            _const_spec(m1.shape),
            pl.BlockSpec(blk, lambda bi, g: (bi, 0, 0, g, 0, 0)),
        ],
        out_specs=pl.BlockSpec(blk, lambda bi, g: (bi, 0, 0, g, 0, 0)),
        out_shape=jax.ShapeDtypeStruct(ab.shape, F32),
        compiler_params=_cparams(("parallel", "parallel")),
        name="fnet_s1",
    )(m1, ab)


def _fnet_s2_kernel(m2_ref, tw_ref, y_ref, f_ref, z_ref):
    n, d = y_ref.shape[2:]
    cos = tw_ref[0]
    sin = tw_ref[1]
    for c0 in range(0, d, LANES):
        yr = y_ref[0, 0, :, c0:c0 + LANES]
        yi = y_ref[0, 1, :, c0:c0 + LANES]
        z_ref[:n, c0:c0 + LANES] = (yr * cos + yi * sin).astype(BF16)
        z_ref[n:, c0:c0 + LANES] = (yi * cos - yr * sin).astype(BF16)
    f = _dot(m2_ref[...], z_ref[...])
    f_ref[0] = f.reshape(f_ref.shape[1:])


def _fnet_s2(m2, tw, y, *, s1, s2):
    batch, _, _, d = y.shape
    n = 8 * s2
    return pl.pallas_call(
        _fnet_s2_kernel,
        grid=(batch, s1 // 8),
        in_specs=[
            _const_spec(m2.shape),
            pl.BlockSpec((2, n, LANES), lambda bi, a: (0, a, 0)),
            pl.BlockSpec((1, 2, n, d), lambda bi, a: (bi, 0, a, 0)),
        ],
        out_specs=pl.BlockSpec((1, s2, 1, 8, d), lambda bi, a: (bi, 0, a, 0, 0)),
        out_shape=jax.ShapeDtypeStruct((batch, s2, s1 // 8, 8, d), F32),
        scratch_shapes=[pltpu.VMEM((2 * n, d), BF16)],
        compiler_params=_cparams(("parallel", "parallel")),
        name="fnet_s2",
    )(m2, tw, y)


def _rope_table(seq):
    half = MLA_ROPE // 2
    pos = jnp.arange(seq, dtype=F32)
    inv_freq = ROPE_THETA ** (-jnp.arange(half, dtype=F32) / half)
    ang = pos[:, None] * inv_freq[None, :]
    cos, sin = jnp.cos(ang), jnp.sin(ang)
    return jnp.concatenate([cos, cos, -sin, sin], axis=1)


def _swap_halves(w):
    half = w.shape[-1] // 2
    return jnp.concatenate([w[..., half:], w[..., :half]], axis=-1)


def _prep_mla(w_down, w_uq, w_ukv):
    d = w_down.shape[0]
    rope = w_down[:, MLA_Q_LORA + MLA_KV_LORA:]
    wd = jnp.concatenate([w_down, _swap_halves(rope)], axis=1).astype(BF16)
    uq = w_uq.reshape(MLA_Q_LORA, MLA_HEADS, MLA_NOPE + MLA_ROPE)
    uq_rope = uq[..., MLA_NOPE:]
    wq = jnp.concatenate([uq, _swap_halves(uq_rope)], axis=-1)
    wq = wq.reshape(MLA_Q_LORA, MLA_HEADS * MLA_QK_PAD).astype(BF16)
    ukv = w_ukv.reshape(MLA_KV_LORA, MLA_HEADS, MLA_NOPE + MLA_V)
    wkv = jnp.concatenate([ukv[..., :MLA_NOPE].reshape(MLA_KV_LORA, -1),
                           ukv[..., MLA_NOPE:].reshape(MLA_KV_LORA, -1)], axis=1).astype(BF16)
    del d
    return wd, wq, wkv


def _fnet_tables(seq, cg):
    def trig(n):
        idx = jnp.arange(n, dtype=jnp.int32)
        ang = ((idx[:, None] * idx[None, :]) % n).astype(F32) * (2.0 * math.pi / n)
        return jnp.cos(ang), jnp.sin(ang)

    cc, sc = trig(cg)
    cs = (jnp.concatenate([cc, sc], axis=1) * (cg ** -0.5)).astype(BF16)
    s2 = min(seq, LANES)
    s1 = seq // s2
    eye8 = jnp.eye(8, dtype=F32)
    c1, sn1 = trig(s1)
    m1 = jnp.block([[c1, -sn1], [-sn1, -c1]]) * (s1 ** -0.5)
    m1 = jnp.kron(m1, eye8).astype(BF16)
    c2, sn2 = trig(s2)
    m2 = jnp.stack([c2, sn2]) * (s2 ** -0.5)
    m2 = jnp.einsum('cts,rq->trcqs', m2, eye8).reshape(8 * s2, 16 * s2).astype(BF16)
    t1 = jnp.arange(s1, dtype=jnp.int32)[:, None]
    k2 = jnp.arange(s2, dtype=jnp.int32)[None, :]
    ang = ((t1 * k2).astype(F32) * (2.0 * math.pi / seq)).reshape(seq)
    tw = jnp.stack([jnp.cos(ang), jnp.sin(ang)])
    tw = jnp.broadcast_to(tw[..., None], (2, seq, LANES))
    return cs, m1, m2, tw


def _pick(n, pref):
    return pref if n % pref == 0 else n


def _trunk(x3, mods, prm):
    batch, seq, d = x3.shape
    t = batch * seq
    x = x3.reshape(t, d)
    tm = _pick(seq, 512)
    row = lambda v: v.reshape(1, -1)
    trig = _rope_table(seq)
    for l in range(mods.shape[0]):
        i = l // 3
        kind = l % 3
        mod = mods[l]
        gpre = prm['norm_pre'][l]
        gpost = prm['norm_post'][l]
        x = _ffn(x, mod, row(gpre[0]), row(gpost[0]), prm['ffn_w_in'][l][0], prm['ffn_w_out'][l][0],
                 sub=0, seq=seq, tm=tm)
        if kind == 0:
            wd, wq, wkv = prm['mla'][i]
            qt, k, vt = _mla_proj(x, mod, row(gpre[1]), trig, wd, row(prm['mla_q_norm'][i]),
                                  row(prm['mla_kv_norm'][i]), wq, wkv, seq=seq, tm=tm)
            o = _flash(qt, k.reshape(batch, seq, -1), vt)
            mixed = (o.reshape(t, -1), prm['mla_w_out'][i], jnp.zeros((1, d), F32))
        elif kind == 1:
            g = prm['gdn'][i]
            q, k, v, gate, gb = _gdn_proj(x, mod, row(gpre[1]), g['wqkv'], g['wg'], g['wab'], g['conv'],
                                          g['aneg'], g['dtb'], seq=seq, tm=_pick(seq, 256))
            gc = _gdn_cumsum(gb, tm=tm)
            nh = GDN_HEADS
            gc3 = gc.reshape(batch, seq, LANES)
            outs = []
            for dr in range(2):
                gcol = gc3[:, :, dr * nh:(dr + 1) * nh]
                bcol = gc3[:, :, 2 * nh + dr * nh:2 * nh + (dr + 1) * nh]
                grow = jnp.swapaxes(gcol.reshape(batch, seq // GDN_CHUNK, GDN_CHUNK, nh), 2, 3)
                outs.append(_gdn_delta(q, k, v, gcol, bcol, grow, batch=batch, seq=seq,
                                       tb=_pick(seq, 256), reverse=bool(dr)))
            x = _gdn_out(outs[0], outs[1], gate, x, mod, row(gpost[1]), row(prm['gdn_o_norm'][i]),
                         g['wout'], seq=seq, tm=tm)
            mixed = None
        else:
            cs, m1, m2, tw = _fnet_tables(seq, d // FNET_GROUPS)
            s2 = min(seq, LANES)
            s1 = seq // s2
            ab = _fnet_chan(x, mod, row(gpre[1]), cs, batch=batch, seq=seq, tm=tm)
            y = _fnet_s1(m1, ab.reshape(batch, 2, s1, s2 // 8, 8, d))
            f = _fnet_s2(m2, tw, y.reshape(batch, 2, seq, d), s1=s1, s2=s2)
            mixed = (f.reshape(t, d), prm['fnet_w_out'][i], row(prm['fnet_b_out'][i]))
        w_in2, w_out2 = prm['ffn_w_in'][l][1], prm['ffn_w_out'][l][1]
        if mixed is None:
            x = _ffn(x, mod, row(gpre[2]), row(gpost[2]), w_in2, w_out2, sub=2, seq=seq, tm=tm)
        else:
            a, w_mix, b_mix = mixed
            x = _mix_ffn(a, x, mod, row(gpost[1]), w_mix, b_mix, row(gpre[2]), row(gpost[2]),
                         w_in2, w_out2, seq=seq, tm=tm)
    return x.reshape(batch, seq, d)


def _prep_gdn(w_in, conv, a_log, dt_bias, w_out):
    n_conv = 2 * GDN_QK + GDN_VW
    wab = jnp.pad(w_in[:, n_conv + GDN_VW:], ((0, 0), (0, LANES - 4 * GDN_HEADS)))
    pad16 = lambda v: jnp.pad(v.reshape(1, -1).astype(F32), ((0, 0), (0, LANES - 2 * GDN_HEADS)))
    return {
        'wqkv': w_in[:, :n_conv].astype(BF16),
        'wg': w_in[:, n_conv:n_conv + GDN_VW].astype(BF16),
        'wab': wab.astype(BF16),
        'conv': conv.astype(F32),
        'aneg': pad16(-jnp.exp(a_log.astype(F32))),
        'dtb': pad16(dt_bias),
        'wout': w_out.astype(BF16),
    }


def kernel(x_prompt, x_sample, c_prompt, c_sample, w_ada, b_ada, norm_pre, norm_post, ffn_w_in, ffn_w_out, mla_w_down, mla_q_norm, mla_kv_norm, mla_w_uq, mla_w_ukv, mla_w_out, gdn_w_in, gdn_conv, gdn_a_log, gdn_dt_bias, gdn_o_norm, gdn_w_out, fnet_w_out, fnet_b_out):
    d = x_prompt.shape[-1]
    prm = {
        'norm_pre': norm_pre, 'norm_post': norm_post,
        'ffn_w_in': ffn_w_in.astype(BF16), 'ffn_w_out': ffn_w_out.astype(BF16),
        'mla': [_prep_mla(mla_w_down[i], mla_w_uq[i], mla_w_ukv[i]) for i in range(mla_w_down.shape[0])],
        'mla_q_norm': mla_q_norm, 'mla_kv_norm': mla_kv_norm, 'mla_w_out': mla_w_out.astype(BF16),
        'gdn': [_prep_gdn(gdn_w_in[i], gdn_conv[i], gdn_a_log[i], gdn_dt_bias[i], gdn_w_out[i])
                for i in range(gdn_w_in.shape[0])],
        'gdn_o_norm': gdn_o_norm,
        'fnet_w_out': fnet_w_out.astype(BF16), 'fnet_b_out': fnet_b_out,
    }
    nbp = c_prompt.shape[0]
    c_all = jnp.concatenate([c_prompt, c_sample], axis=0)
    mods = _ada_mod(c_all, w_ada, b_ada).reshape(w_ada.shape[0], c_all.shape[0], 3 * N_SUB, d)
    y_prompt = _trunk(x_prompt, mods[:, :nbp], prm)
    y_sample = _trunk(x_sample, mods[:, nbp:], prm)
    return (y_prompt, y_sample)
```

```python
import functools
import math

import jax
import jax.numpy as jnp
from jax import lax
from jax.experimental import pallas as pl
from jax.experimental.pallas import tpu as pltpu

F32 = jnp.float32
BF16 = jnp.bfloat16

N_SUB = 3
D_FF = 2816
NORM_EPS = 1e-6

MLA_HEADS = 8
MLA_Q_LORA = 384
MLA_KV_LORA = 256
MLA_NOPE = 128
MLA_ROPE = 64
MLA_V = 128
ROPE_THETA = 10000.0
MLA_SCALE = (MLA_NOPE + MLA_ROPE) ** -0.5
MLA_QK_PAD = 256

GDN_HEADS = 8
GDN_DK = 128
GDN_DV = 128
GDN_QK = GDN_HEADS * GDN_DK
GDN_VW = GDN_HEADS * GDN_DV
GDN_CONV = 5
GDN_CHUNK = 64
GDN_HALO = 8
GDN_SLABS_PER_DOT = 4

FNET_GROUPS = 8

LANES = 128
VMEM_LIMIT = 56 * 1024 * 1024


def _cparams(sem):
    return pltpu.CompilerParams(dimension_semantics=sem, vmem_limit_bytes=VMEM_LIMIT)


def _const_spec(shape):
    nd = len(shape)
    return pl.BlockSpec(shape, lambda *_: (0,) * nd, pipeline_mode=pl.Buffered(1))


def _rms(x, g):
    ms = jnp.mean(x * x, axis=-1, keepdims=True)
    return x * lax.rsqrt(ms + NORM_EPS) * g


def _modulated(x, mod_ref, gpre_ref, sub):
    shift = mod_ref[0, 3 * sub:3 * sub + 1, :]
    scale = mod_ref[0, 3 * sub + 1:3 * sub + 2, :]
    return _rms(x, gpre_ref[...]) * (1.0 + scale) + shift


def _residual(x, out, mod_ref, gpost_ref, sub, weight):
    gate = mod_ref[0, 3 * sub + 2:3 * sub + 3, :]
    return x + (weight * gate) * _rms(out, gpost_ref[...])


def _dot(a, b):
    return jnp.dot(a, b, preferred_element_type=F32)


def _dot_nt(a, b):
    return lax.dot_general(a, b, (((1,), (1,)), ((), ())), preferred_element_type=F32)


def _dot_tn(a, b):
    return lax.dot_general(a, b, (((0,), (0,)), ((), ())), preferred_element_type=F32)


def _ada_kernel(c_ref, w_ref, b_ref, o_ref):
    c = c_ref[...]
    sc = (c * jax.nn.sigmoid(c)).astype(BF16)
    o_ref[0] = _dot(sc, w_ref[0].astype(BF16)) + b_ref[0]


def _ada_mod(c_all, w_ada, b_ada):
    nb, d = c_all.shape
    depth, _, n_out = w_ada.shape
    tn = n_out // 8
    return pl.pallas_call(
        _ada_kernel,
        grid=(depth, n_out // tn),
        in_specs=[
            pl.BlockSpec((nb, d), lambda l, j: (0, 0)),
            pl.BlockSpec((1, d, tn), lambda l, j: (l, 0, j)),
            pl.BlockSpec((1, 1, tn), lambda l, j: (l, 0, j)),
        ],
        out_specs=pl.BlockSpec((1, nb, tn), lambda l, j: (l, 0, j)),
        out_shape=jax.ShapeDtypeStruct((depth, nb, n_out), F32),
        compiler_params=_cparams(("arbitrary", "arbitrary")),
        name="ada_mod",
    )(c_all, w_ada, b_ada.reshape(depth, 1, n_out))


FFN_CHUNK = 256


FFN_SPLIT = 2


def _ffn_step(xs, mod_ref, gpre_ref, gpost_ref, win_ref, wout_ref, act_ref, sub):
    n = xs[0].shape[0]
    hbs = [_modulated(x, mod_ref, gpre_ref, sub).astype(BF16) for x in xs]
    for c0 in range(0, D_FF, FFN_CHUNK):
        for r, hb in enumerate(hbs):
            g = _dot(hb, win_ref[:, c0:c0 + FFN_CHUNK])
            u = _dot(hb, win_ref[:, D_FF + c0:D_FF + c0 + FFN_CHUNK])
            act_ref[r * n:(r + 1) * n, c0:c0 + FFN_CHUNK] = (g * jax.nn.sigmoid(g) * u).astype(BF16)
    outs = [_dot(act_ref[r * n:(r + 1) * n, :], wout_ref[...]) for r in range(len(xs))]
    return [_residual(x, out, mod_ref, gpost_ref, sub, 0.5) for x, out in zip(xs, outs)]


def _row_groups(ref):
    n = ref.shape[0] // FFN_SPLIT
    return [ref[r * n:(r + 1) * n, :] for r in range(FFN_SPLIT)]


def _store_row_groups(ref, vals):
    n = ref.shape[0] // FFN_SPLIT
    for r, v in enumerate(vals):
        ref[r * n:(r + 1) * n, :] = v


def _ffn_kernel(x_ref, mod_ref, gpre_ref, gpost_ref, win_ref, wout_ref, o_ref, act_ref, *, sub):
    _store_row_groups(o_ref, _ffn_step(_row_groups(x_ref), mod_ref, gpre_ref, gpost_ref,
                                        win_ref, wout_ref, act_ref, sub))


def _mix_ffn_kernel(a_ref, x_ref, mod_ref, gpost_mix_ref, wmix_ref, bmix_ref,
                    gpre_ref, gpost_ref, win_ref, wout_ref, o_ref, act_ref):
    xs = [_residual(x, _dot(a.astype(BF16), wmix_ref[...]) + bmix_ref[...], mod_ref, gpost_mix_ref, 1, 1.0)
          for a, x in zip(_row_groups(a_ref), _row_groups(x_ref))]
    _store_row_groups(o_ref, _ffn_step(xs, mod_ref, gpre_ref, gpost_ref, win_ref, wout_ref, act_ref, 2))


def _mix_ffn(a, x, mod, gpost_mix, w_mix, b_mix, gpre, gpost, w_in, w_out, *, seq, tm):
    t, d = x.shape
    return pl.pallas_call(
        _mix_ffn_kernel,
        grid=(t // tm,),
        in_specs=[
            pl.BlockSpec((tm, a.shape[1]), lambda i: (i, 0)),
            pl.BlockSpec((tm, d), lambda i: (i, 0)),
            pl.BlockSpec((1, 3 * N_SUB, d), lambda i: ((i * tm) // seq, 0, 0)),
            _const_spec((1, d)),
            _const_spec(w_mix.shape),
            _const_spec((1, d)),
            _const_spec((1, d)),
            _const_spec((1, d)),
            _const_spec(w_in.shape),
            _const_spec(w_out.shape),
        ],
        out_specs=pl.BlockSpec((tm, d), lambda i: (i, 0)),
        out_shape=jax.ShapeDtypeStruct((t, d), F32),
        scratch_shapes=[pltpu.VMEM((tm, D_FF), BF16)],
        compiler_params=_cparams(("parallel",)),
        name="mix_ffn",
    )(a, x, mod, gpost_mix, w_mix, b_mix, gpre, gpost, w_in, w_out)


def _ffn(x, mod, gpre, gpost, w_in, w_out, *, sub, seq, tm):
    t, d = x.shape
    return pl.pallas_call(
        functools.partial(_ffn_kernel, sub=sub),
        grid=(t // tm,),
        in_specs=[
            pl.BlockSpec((tm, d), lambda i: (i, 0)),
            pl.BlockSpec((1, 3 * N_SUB, d), lambda i: ((i * tm) // seq, 0, 0)),
            _const_spec((1, d)),
            _const_spec((1, d)),
            _const_spec(w_in.shape),
            _const_spec(w_out.shape),
        ],
        out_specs=pl.BlockSpec((tm, d), lambda i: (i, 0)),
        out_shape=jax.ShapeDtypeStruct((t, d), F32),
        scratch_shapes=[pltpu.VMEM((tm, D_FF), BF16)],
        compiler_params=_cparams(("parallel",)),
        name="ffn",
    )(x, mod, gpre, gpost, w_in, w_out)


def _rope_pair(t):
    return t + pltpu.roll(t, 64, axis=1)


def _mla_proj_kernel(x_ref, mod_ref, gpre_ref, trig_ref, wd_ref, qn_ref, kvn_ref, wq_ref, wkv_ref,
                     qt_ref, k_ref, vt_ref):
    x = x_ref[...]
    hb = _modulated(x, mod_ref, gpre_ref, 1).astype(BF16)
    down = _dot(hb, wd_ref[...])
    cq = _rms(down[:, :MLA_Q_LORA], qn_ref[...]).astype(BF16)
    ckv = _rms(down[:, MLA_Q_LORA:MLA_Q_LORA + MLA_KV_LORA], kvn_ref[...]).astype(BF16)
    trig = trig_ref[...]
    lane = lax.broadcasted_iota(jnp.int32, trig.shape, 1)
    k_rope = _rope_pair(down[:, MLA_Q_LORA + MLA_KV_LORA:] * trig)
    k_rope = jnp.where(lane < MLA_ROPE, k_rope, 0.0).astype(BF16)
    q = _dot(cq, wq_ref[...])
    kv = _dot(ckv, wkv_ref[...])
    qs = MLA_SCALE * math.log2(math.e)
    for h in range(MLA_HEADS):
        c0 = h * MLA_QK_PAD
        q_rope = _rope_pair(q[:, c0 + LANES:c0 + 2 * LANES] * trig)
        qt_ref[0, h, 0, :LANES, :] = (q[:, c0:c0 + LANES] * qs).T.astype(BF16)
        qt_ref[0, h, 0, LANES:, :] = (q_rope * qs).T.astype(BF16)
        k_ref[:, c0:c0 + LANES] = kv[:, h * MLA_NOPE:(h + 1) * MLA_NOPE].astype(BF16)
        k_ref[:, c0 + LANES:c0 + 2 * LANES] = k_rope
        v0 = MLA_HEADS * MLA_NOPE + h * MLA_V
        vt_ref[0, h, 0] = kv[:, v0:v0 + MLA_V].T.astype(BF16)


def _mla_proj(x, mod, gpre, trig, wd, qn, kvn, wq, wkv, *, seq, tm):
    t, d = x.shape
    nq = MLA_HEADS * MLA_QK_PAD
    nt = seq // tm
    return pl.pallas_call(
        _mla_proj_kernel,
        grid=(t // tm,),
        in_specs=[
            pl.BlockSpec((tm, d), lambda i: (i, 0)),
            pl.BlockSpec((1, 3 * N_SUB, d), lambda i: ((i * tm) // seq, 0, 0)),
            _const_spec((1, d)),
            pl.BlockSpec((tm, LANES), lambda i: (i % (seq // tm), 0)),
            _const_spec(wd.shape),
            _const_spec(qn.shape),
            _const_spec(kvn.shape),
            _const_spec(wq.shape),
            _const_spec(wkv.shape),
        ],
        out_specs=[
            pl.BlockSpec((1, MLA_HEADS, 1, MLA_QK_PAD, tm), lambda i: (i // nt, 0, i % nt, 0, 0)),
            pl.BlockSpec((tm, nq), lambda i: (i, 0)),
            pl.BlockSpec((1, MLA_HEADS, 1, MLA_V, tm), lambda i: (i // nt, 0, i % nt, 0, 0)),
        ],
        out_shape=[
            jax.ShapeDtypeStruct((t // seq, MLA_HEADS, nt, MLA_QK_PAD, tm), BF16),
            jax.ShapeDtypeStruct((t, nq), BF16),
            jax.ShapeDtypeStruct((t // seq, MLA_HEADS, nt, MLA_V, tm), BF16),
        ],
        compiler_params=_cparams(("parallel",)),
        name="mla_proj",
    )(x, mod, gpre, trig, wd, qn, kvn, wq, wkv)


def _flash_kernel(qt_ref, k_ref, vt_ref, o_ref, s_ref):
    nq, _, tq = qt_ref.shape[2:]
    nk, _, tk = vt_ref.shape[2:]
    unroll = next((u for u in (8, 4, 2) if nk % u == 0 and nk >= 2 * u), 1)
    cmax0 = jnp.full((8, tq), -1e30, F32)
    acc0 = (jnp.zeros((MLA_V, tq), F32), jnp.zeros((8, tq), F32))

    def score_part(j, slot, qt, cmax):
        r0 = pl.multiple_of(j * tk, tk)
        s = _dot(k_ref[0, pl.ds(r0, tk), :], qt)
        s_ref[slot, j] = s
        return jnp.maximum(cmax, jnp.max(s.reshape(tk // 8, 8, tq), axis=0))

    def value_part(j, slot, m, acc):
        p = jnp.exp2((s_ref[slot, j] - m).astype(BF16))
        o_acc, l_acc = acc
        l_acc = l_acc + jnp.sum(p.astype(F32).reshape(tk // 8, 8, tq), axis=0)
        return o_acc + _dot(vt_ref[0, 0, j], p), l_acc

    def scores_only(i, slot):
        qt = qt_ref[0, 0, i]
        cmax = lax.fori_loop(0, nk, lambda j, c: score_part(j, slot, qt, c), cmax0, unroll=unroll)
        return jnp.max(cmax, axis=0, keepdims=True)

    def values_only(slot, m):
        return lax.fori_loop(0, nk, lambda j, a: value_part(j, slot, m, a), acc0, unroll=unroll)

    def both(i_next, cur, m):
        qt = qt_ref[0, 0, i_next]

        def body(j, carry):
            cmax, acc = carry
            return score_part(j, 1 - cur, qt, cmax), value_part(j, cur, m, acc)

        cmax, acc = lax.fori_loop(0, nk, body, (cmax0, acc0), unroll=unroll)
        return jnp.max(cmax, axis=0, keepdims=True), acc

    def emit(i, acc):
        o_t = acc[0] / jnp.sum(acc[1], axis=0, keepdims=True)
        o_ref[0, pl.ds(pl.multiple_of(i * tq, tq), tq), :] = o_t.T.astype(o_ref.dtype)

    m = scores_only(0, 0)
    if nq == 1:
        emit(0, values_only(0, m))
        return
    assert nq % 2 == 0

    def pair(ip, m):
        i = 2 * ip
        m_odd, acc = both(i + 1, 0, m)
        emit(i, acc)
        m_even, acc = both(i + 2, 1, m_odd)
        emit(i + 1, acc)
        return m_even

    m = lax.fori_loop(0, nq // 2 - 1, pair, m)
    m_last, acc = both(nq - 1, 0, m)
    emit(nq - 2, acc)
    emit(nq - 1, values_only(1, m_last))


def _flash(qt, k, vt):
    b, s, _ = k.shape
    _, _, nq, _, tq = qt.shape
    _, _, nk, dv, tk = vt.shape
    whole = lambda shape: pl.BlockSpec(shape, lambda bi, h: (bi, h) + (0,) * (len(shape) - 2),
                                       pipeline_mode=pl.Buffered(1))
    return pl.pallas_call(
        _flash_kernel,
        grid=(b, MLA_HEADS),
        in_specs=[
            whole((1, 1, nq, MLA_QK_PAD, tq)),
            pl.BlockSpec((1, s, MLA_QK_PAD), lambda bi, h: (bi, 0, h), pipeline_mode=pl.Buffered(1)),
            whole((1, 1, nk, dv, tk)),
        ],
        out_specs=pl.BlockSpec((1, s, MLA_V), lambda bi, h: (bi, 0, h)),
        out_shape=jax.ShapeDtypeStruct((b, s, MLA_HEADS * MLA_V), BF16),
        scratch_shapes=[pltpu.VMEM((2, nk, tk, tq), F32)],
        compiler_params=_cparams(("parallel", "parallel")),
        name="mla_flash",
    )(qt, k, vt)


def _gdn_proj_kernel(xp_ref, x_ref, xn_ref, mod_ref, gpre_ref, wqkv_ref, wg_ref, wab_ref, conv_ref,
                     aneg_ref, dtb_ref, q_ref, k_ref, v_ref, gate_ref, gb_ref, p_ref,
                     *, tiles_per_seq):
    i = pl.program_id(0)
    tm = x_ref.shape[0]
    first = (i % tiles_per_seq) == 0
    last = (i % tiles_per_seq) == tiles_per_seq - 1

    def mod_rows(ref):
        return _modulated(ref[...], mod_ref, gpre_ref, 1)

    h = mod_rows(x_ref)
    hb = h.astype(BF16)
    hb_all = jnp.concatenate([mod_rows(xp_ref), h, mod_rows(xn_ref)], axis=0).astype(BF16)

    def l2n(z):
        return z * lax.rsqrt(jnp.sum(z * z, axis=-1, keepdims=True) + NORM_EPS)

    pad = GDN_CONV // 2
    n_slab = p_ref.shape[0]
    for g0 in range(0, n_slab, GDN_SLABS_PER_DOT):
        p = _dot(hb_all, wqkv_ref[:, g0 * LANES:(g0 + GDN_SLABS_PER_DOT) * LANES])
        for c in range(g0, g0 + GDN_SLABS_PER_DOT):
            cs = slice(c * LANES, (c + 1) * LANES)
            pc = p[:, (c - g0) * LANES:(c - g0 + 1) * LANES]
            p_ref[c, 0:GDN_HALO, :] = jnp.where(first, 0.0, pc[0:GDN_HALO])
            p_ref[c, GDN_HALO:GDN_HALO + tm, :] = pc[GDN_HALO:GDN_HALO + tm]
            p_ref[c, GDN_HALO + tm:, :] = jnp.where(last, 0.0, pc[GDN_HALO + tm:])
            acc = None
            for tap in range(GDN_CONV):
                r0 = GDN_HALO - pad + tap
                term = p_ref[c, r0:r0 + tm, :] * conv_ref[tap:tap + 1, cs]
                acc = term if acc is None else acc + term
            z = acc * jax.nn.sigmoid(acc)
            if c < GDN_HEADS:
                q_ref[:, cs] = l2n(z) * (GDN_DK ** -0.5)
            elif c < 2 * GDN_HEADS:
                k_ref[:, c * LANES - GDN_QK:(c + 1) * LANES - GDN_QK] = l2n(z)
            else:
                v_ref[:, c * LANES - 2 * GDN_QK:(c + 1) * LANES - 2 * GDN_QK] = z
    gate_ref[...] = _dot(hb, wg_ref[...])
    ab = _dot(hb, wab_ref[...])
    z = ab + dtb_ref[...]
    softplus = jnp.maximum(z, 0.0) + jnp.log(1.0 + jnp.exp(-jnp.abs(z)))
    lane = lax.broadcasted_iota(jnp.int32, ab.shape, 1)
    gb = jnp.where(lane < 2 * GDN_HEADS, aneg_ref[...] * softplus, jax.nn.sigmoid(ab))
    gb_ref[...] = _chunk_cumsums(gb)


def _gdn_proj(x, mod, gpre, wqkv, wg, wab, conv, aneg, dtb, *, seq, tm):
    t, d = x.shape
    hb = tm // GDN_HALO
    nblk8 = t // GDN_HALO
    n_conv = wqkv.shape[1]
    return pl.pallas_call(
        functools.partial(_gdn_proj_kernel, tiles_per_seq=seq // tm),
        grid=(t // tm,),
        in_specs=[
            pl.BlockSpec((GDN_HALO, d), lambda i: (jnp.maximum(i * hb - 1, 0), 0)),
            pl.BlockSpec((tm, d), lambda i: (i, 0)),
            pl.BlockSpec((GDN_HALO, d), lambda i: (jnp.minimum((i + 1) * hb, nblk8 - 1), 0)),
            pl.BlockSpec((1, 3 * N_SUB, d), lambda i: ((i * tm) // seq, 0, 0)),
            _const_spec((1, d)),
            _const_spec(wqkv.shape),
            _const_spec(wg.shape),
            _const_spec(wab.shape),
            _const_spec(conv.shape),
            _const_spec(aneg.shape),
            _const_spec(dtb.shape),
        ],
        out_specs=[
            pl.BlockSpec((tm, GDN_QK), lambda i: (i, 0)),
            pl.BlockSpec((tm, GDN_QK), lambda i: (i, 0)),
            pl.BlockSpec((tm, GDN_VW), lambda i: (i, 0)),
            pl.BlockSpec((tm, GDN_VW), lambda i: (i, 0)),
            pl.BlockSpec((tm, LANES), lambda i: (i, 0)),
        ],
        out_shape=[
            jax.ShapeDtypeStruct((t, GDN_QK), F32),
            jax.ShapeDtypeStruct((t, GDN_QK), F32),
            jax.ShapeDtypeStruct((t, GDN_VW), F32),
            jax.ShapeDtypeStruct((t, GDN_VW), F32),
            jax.ShapeDtypeStruct((t, LANES), F32),
        ],
        scratch_shapes=[pltpu.VMEM((n_conv // LANES, tm + 2 * GDN_HALO, LANES), F32)],
        compiler_params=_cparams(("parallel",)),
        name="gdn_proj",
    )(x, x, x, mod, gpre, wqkv, wg, wab, conv, aneg, dtb)


def _chunk_cumsums(g):
    tm = g.shape[0]
    row = lax.broadcasted_iota(jnp.int32, (tm, tm), 0)
    col = lax.broadcasted_iota(jnp.int32, (tm, tm), 1)
    sh = int(math.log2(GDN_CHUNK))
    same = lax.shift_right_logical(row, sh) == lax.shift_right_logical(col, sh)
    lower = jnp.where(same & (col <= row), 1.0, 0.0).astype(BF16)
    upper = jnp.where(same & (col >= row), 1.0, 0.0).astype(BF16)
    g1 = g.astype(BF16)
    r1 = g - g1.astype(F32)
    g2 = r1.astype(BF16)
    g3 = (r1 - g2.astype(F32)).astype(BF16)
    pre = _dot(lower, g1) + _dot(lower, g2) + _dot(lower, g3)
    suf = _dot(upper, g1) + _dot(upper, g2) + _dot(upper, g3)
    lane = lax.broadcasted_iota(jnp.int32, g.shape, 1)
    return jnp.where(lane < GDN_HEADS, pre, jnp.where(lane < 2 * GDN_HEADS, suf, g))


def _gdn_delta_kernel(q_ref, k_ref, v_ref, gcol_ref, bcol_ref, grow_ref, o_ref, state_ref,
                      *, reverse):
    c = GDN_CHUNK
    nchunk = q_ref.shape[0] // c

    @pl.when(pl.program_id(1) == 0)
    def _():
        state_ref[...] = jnp.zeros_like(state_ref)

    row = lax.broadcasted_iota(jnp.int32, (c, c), 0)
    col = lax.broadcasted_iota(jnp.int32, (c, c), 1)
    if reverse:
        row, col = col, row
    tri = row >= col
    strict = row > col
    eye = jnp.where(row == col, 1.0, 0.0)
    level_masks = []
    for lv in range(int(math.log2(c))):
        rb = lax.shift_right_logical(row, lv)
        cb = lax.shift_right_logical(col, lv)
        level_masks.append((lax.shift_right_logical(rb, 1) == lax.shift_right_logical(cb, 1))
                           & ((rb & 1) == 1) & ((cb & 1) == 0))
    last_row = 0 if reverse else c - 1

    heads = range(GDN_HEADS)
    order = list(reversed(range(nchunk))) if reverse else list(range(nchunk))
    probs = [(j, h) for j in order for h in heads]
    rows = lambda j: slice(j * c, (j + 1) * c)
    cols = lambda h: slice(h * GDN_DK, (h + 1) * GDN_DK)

    qs = [q_ref[rows(j), cols(h)] for j, h in probs]
    ks = [k_ref[rows(j), cols(h)] for j, h in probs]
    vs = [v_ref[rows(j), cols(h)] for j, h in probs]
    gcbs = [jnp.broadcast_to(gcol_ref[0, rows(j), h:h + 1], (c, GDN_DK)) for j, h in probs]
    betas = [jnp.broadcast_to(bcol_ref[0, rows(j), h:h + 1], (c, GDN_DK)) for j, h in probs]
    n = len(probs)
    decays = []
    for p, (j, h) in enumerate(probs):
        grb = jnp.broadcast_to(grow_ref[0, j, h:h + 1, :], (c, c))
        decays.append(jnp.where(tri, jnp.exp(jnp.where(tri, gcbs[p][:, :c] - grb, 0.0)), 0.0))
    kbs = [ks[p] * betas[p] for p in range(n)]
    a2s = [_dot_nt(jnp.concatenate([kbs[p], qs[p]], axis=0).astype(BF16), ks[p].astype(BF16))
           for p in range(n)]
    ms = [jnp.where(strict, a2s[p][:c] * decays[p], 0.0) for p in range(n)]
    qks = [(a2s[p][c:] * decays[p]).astype(BF16) for p in range(n)]
    tinvs = [eye - jnp.where(level_masks[0], ms[p], 0.0) for p in range(n)]
    for lm in level_masks[1:]:
        tbs = [t.astype(BF16) for t in tinvs]
        tcs = [_dot(tbs[p], jnp.where(lm, ms[p], 0.0).astype(BF16)).astype(BF16) for p in range(n)]
        tinvs = [tinvs[p] - _dot(tcs[p], tbs[p]) for p in range(n)]
    egs = [jnp.exp(g) for g in gcbs]
    sols = [_dot(tinvs[p].astype(BF16),
                 jnp.concatenate([vs[p] * betas[p], kbs[p] * egs[p]], axis=1).astype(BF16))
            for p in range(n)]
    g_lasts = [g[last_row:last_row + 1, :] for g in gcbs]
    k_decs = [(ks[p] * jnp.exp(g_lasts[p] - gcbs[p])).astype(BF16) for p in range(n)]
    wq_lhs = [jnp.concatenate([sols[p][:, GDN_DV:], qs[p] * egs[p]], axis=0).astype(BF16)
              for p in range(n)]

    sts = [state_ref[h] for h in heads]
    for step, j in enumerate(order):
        ps = [step * GDN_HEADS + h for h in heads]
        wqs = [_dot(wq_lhs[ps[h]], sts[h].astype(BF16)) for h in heads]
        vnbs = [(sols[ps[h]][:, :GDN_DV] - wqs[h][:c]).astype(BF16) for h in heads]
        for h in heads:
            o_ref[rows(j), cols(h)] = wqs[h][c:] + _dot(qks[ps[h]], vnbs[h])
        sts = [sts[h] * jnp.exp(g_lasts[ps[h]]) + _dot_tn(k_decs[ps[h]], vnbs[h]) for h in heads]
    for h in heads:
        state_ref[h] = sts[h]


def _gdn_delta(q, k, v, gcol, bcol, grow, *, batch, seq, tb, reverse):
    t = q.shape[0]
    nb = seq // tb
    cpb = tb // GDN_CHUNK

    def blk(bi, i):
        return bi * nb + ((nb - 1 - i) if reverse else i)

    def seq_blk(bi, i):
        return (nb - 1 - i) if reverse else i

    return pl.pallas_call(
        functools.partial(_gdn_delta_kernel, reverse=reverse),
        grid=(batch, nb),
        in_specs=[
            pl.BlockSpec((tb, GDN_QK), lambda bi, i: (blk(bi, i), 0)),
            pl.BlockSpec((tb, GDN_QK), lambda bi, i: (blk(bi, i), 0)),
            pl.BlockSpec((tb, GDN_VW), lambda bi, i: (blk(bi, i), 0)),
            pl.BlockSpec((1, tb, GDN_HEADS), lambda bi, i: (bi, seq_blk(bi, i), 0)),
            pl.BlockSpec((1, tb, GDN_HEADS), lambda bi, i: (bi, seq_blk(bi, i), 0)),
            pl.BlockSpec((1, cpb, GDN_HEADS, GDN_CHUNK), lambda bi, i: (bi, seq_blk(bi, i), 0, 0)),
        ],
        out_specs=pl.BlockSpec((tb, GDN_VW), lambda bi, i: (blk(bi, i), 0)),
        out_shape=jax.ShapeDtypeStruct((t, GDN_VW), F32),
        scratch_shapes=[pltpu.VMEM((GDN_HEADS, GDN_DK, GDN_DV), F32)],
        compiler_params=_cparams(("parallel", "arbitrary")),
        name="gdn_delta_bw" if reverse else "gdn_delta_fw",
    )(q, k, v, gcol, bcol, grow)


def _gdn_out_kernel(of_ref, ob_ref, gate_ref, x_ref, mod_ref, gpost_ref, onorm_ref, w_ref, o_ref,
                    a_ref):
    o = of_ref[...] + ob_ref[...]
    gate = gate_ref[...]
    og = gate * jax.nn.sigmoid(gate)
    for h in range(GDN_HEADS):
        cs = slice(h * GDN_DV, (h + 1) * GDN_DV)
        a_ref[:, cs] = (_rms(o[:, cs], onorm_ref[...]) * og[:, cs]).astype(BF16)
    out = _dot(a_ref[...], w_ref[...])
    o_ref[...] = _residual(x_ref[...], out, mod_ref, gpost_ref, 1, 1.0)


def _gdn_out(o_fw, o_bw, gate, x, mod, gpost, onorm, w, *, seq, tm):
    t, d = x.shape
    return pl.pallas_call(
        _gdn_out_kernel,
        grid=(t // tm,),
        in_specs=[
            pl.BlockSpec((tm, GDN_VW), lambda i: (i, 0)),
            pl.BlockSpec((tm, GDN_VW), lambda i: (i, 0)),
            pl.BlockSpec((tm, GDN_VW), lambda i: (i, 0)),
            pl.BlockSpec((tm, d), lambda i: (i, 0)),
            pl.BlockSpec((1, 3 * N_SUB, d), lambda i: ((i * tm) // seq, 0, 0)),
            _const_spec((1, d)),
            _const_spec(onorm.shape),
            _const_spec(w.shape),
        ],
        out_specs=pl.BlockSpec((tm, d), lambda i: (i, 0)),
        out_shape=jax.ShapeDtypeStruct((t, d), F32),
        scratch_shapes=[pltpu.VMEM((tm, GDN_VW), BF16)],
        compiler_params=_cparams(("parallel",)),
        name="gdn_out",
    )(o_fw, o_bw, gate, x, mod, gpost, onorm, w)


def _fnet_chan_kernel(x_ref, mod_ref, gpre_ref, cs_ref, o_ref):
    hb = _modulated(x_ref[...], mod_ref, gpre_ref, 1).astype(BF16)
    cg = hb.shape[1] // FNET_GROUPS
    for g in range(FNET_GROUPS):
        ab = _dot(hb[:, g * cg:(g + 1) * cg], cs_ref[...])
        o_ref[0, 0, :, g * cg:(g + 1) * cg] = ab[:, :cg]
        o_ref[0, 1, :, g * cg:(g + 1) * cg] = ab[:, cg:]


def _fnet_chan(x, mod, gpre, cs, *, batch, seq, tm):
    t, d = x.shape
    nt = seq // tm
    return pl.pallas_call(
        _fnet_chan_kernel,
        grid=(t // tm,),
        in_specs=[
            pl.BlockSpec((tm, d), lambda i: (i, 0)),
            pl.BlockSpec((1, 3 * N_SUB, d), lambda i: ((i * tm) // seq, 0, 0)),
            _const_spec((1, d)),
            _const_spec(cs.shape),
        ],
        out_specs=pl.BlockSpec((1, 2, tm, d), lambda i: (i // nt, 0, i % nt, 0)),
        out_shape=jax.ShapeDtypeStruct((batch, 2, seq, d), F32),
        compiler_params=_cparams(("parallel",)),
        name="fnet_chan",
    )(x, mod, gpre, cs)


def _fnet_s1_kernel(m1_ref, ab_ref, y_ref):
    _, two, s1, _, rows, d = ab_ref.shape
    ab = ab_ref[0].reshape(two * s1 * rows, d).astype(BF16)
    y = _dot(m1_ref[...], ab)
    y_ref[0] = y.reshape(two, s1, 1, rows, d)


def _fnet_s1(m1, ab):
    batch, _, s1, groups, rows, d = ab.shape
    blk = (1, 2, s1, 1, rows, d)
    return pl.pallas_call(
        _fnet_s1_kernel,
        grid=(batch, groups),
        in_specs=[
            _const_spec(m1.shape),
            pl.BlockSpec(blk, lambda bi, g: (bi, 0, 0, g, 0, 0)),
        ],
        out_specs=pl.BlockSpec(blk, lambda bi, g: (bi, 0, 0, g, 0, 0)),
        out_shape=jax.ShapeDtypeStruct(ab.shape, F32),
        compiler_params=_cparams(("parallel", "parallel")),
        name="fnet_s1",
    )(m1, ab)


def _fnet_s2_kernel(m2_ref, tw_ref, y_ref, f_ref, z_ref):
    n, d = y_ref.shape[2:]
    cos = tw_ref[0]
    sin = tw_ref[1]
    for c0 in range(0, d, LANES):
        yr = y_ref[0, 0, :, c0:c0 + LANES]
        yi = y_ref[0, 1, :, c0:c0 + LANES]
        z_ref[:n, c0:c0 + LANES] = (yr * cos + yi * sin).astype(BF16)
        z_ref[n:, c0:c0 + LANES] = (yi * cos - yr * sin).astype(BF16)
    f = _dot(m2_ref[...], z_ref[...])
    f_ref[0] = f.reshape(f_ref.shape[1:])


def _fnet_s2(m2, tw, y, *, s1, s2):
    batch, _, _, d = y.shape
    n = 8 * s2
    return pl.pallas_call(
        _fnet_s2_kernel,
        grid=(batch, s1 // 8),
        in_specs=[
            _const_spec(m2.shape),
            pl.BlockSpec((2, n, LANES), lambda bi, a: (0, a, 0)),
            pl.BlockSpec((1, 2, n, d), lambda bi, a: (bi, 0, a, 0)),
        ],
        out_specs=pl.BlockSpec((1, s2, 1, 8, d), lambda bi, a: (bi, 0, a, 0, 0)),
        out_shape=jax.ShapeDtypeStruct((batch, s2, s1 // 8, 8, d), F32),
        scratch_shapes=[pltpu.VMEM((2 * n, d), BF16)],
        compiler_params=_cparams(("parallel", "parallel")),
        name="fnet_s2",
    )(m2, tw, y)


def _rope_table(seq):
    half = MLA_ROPE // 2
    pos = jnp.arange(seq, dtype=F32)
    inv_freq = ROPE_THETA ** (-jnp.arange(half, dtype=F32) / half)
    ang = pos[:, None] * inv_freq[None, :]
    cos, sin = jnp.cos(ang), jnp.sin(ang)
    return jnp.concatenate([cos, cos, -sin, sin], axis=1)


def _swap_halves(w):
    half = w.shape[-1] // 2
    return jnp.concatenate([w[..., half:], w[..., :half]], axis=-1)


def _prep_mla(w_down, w_uq, w_ukv):
    d = w_down.shape[0]
    rope = w_down[:, MLA_Q_LORA + MLA_KV_LORA:]
    wd = jnp.concatenate([w_down, _swap_halves(rope)], axis=1).astype(BF16)
    uq = w_uq.reshape(MLA_Q_LORA, MLA_HEADS, MLA_NOPE + MLA_ROPE)
    uq_rope = uq[..., MLA_NOPE:]
    wq = jnp.concatenate([uq, _swap_halves(uq_rope)], axis=-1)
    wq = wq.reshape(MLA_Q_LORA, MLA_HEADS * MLA_QK_PAD).astype(BF16)
    ukv = w_ukv.reshape(MLA_KV_LORA, MLA_HEADS, MLA_NOPE + MLA_V)
    wkv = jnp.concatenate([ukv[..., :MLA_NOPE].reshape(MLA_KV_LORA, -1),
                           ukv[..., MLA_NOPE:].reshape(MLA_KV_LORA, -1)], axis=1).astype(BF16)
    del d
    return wd, wq, wkv


def _fnet_tables(seq, cg):
    def trig(n):
        idx = jnp.arange(n, dtype=jnp.int32)
        ang = ((idx[:, None] * idx[None, :]) % n).astype(F32) * (2.0 * math.pi / n)
        return jnp.cos(ang), jnp.sin(ang)

    cc, sc = trig(cg)
    cs = (jnp.concatenate([cc, sc], axis=1) * (cg ** -0.5)).astype(BF16)
    s2 = min(seq, LANES)
    s1 = seq // s2
    eye8 = jnp.eye(8, dtype=F32)
    c1, sn1 = trig(s1)
    m1 = jnp.block([[c1, -sn1], [-sn1, -c1]]) * (s1 ** -0.5)
    m1 = jnp.kron(m1, eye8).astype(BF16)
    c2, sn2 = trig(s2)
    m2 = jnp.stack([c2, sn2]) * (s2 ** -0.5)
    m2 = jnp.einsum('cts,rq->trcqs', m2, eye8).reshape(8 * s2, 16 * s2).astype(BF16)
    t1 = jnp.arange(s1, dtype=jnp.int32)[:, None]
    k2 = jnp.arange(s2, dtype=jnp.int32)[None, :]
    ang = ((t1 * k2).astype(F32) * (2.0 * math.pi / seq)).reshape(seq)
    tw = jnp.stack([jnp.cos(ang), jnp.sin(ang)])
    tw = jnp.broadcast_to(tw[..., None], (2, seq, LANES))
    return cs, m1, m2, tw


def _pick(n, pref):
    return pref if n % pref == 0 else n


def _trunk(x3, mods, prm):
    batch, seq, d = x3.shape
    t = batch * seq
    x = x3.reshape(t, d)
    tm = _pick(seq, 512)
    row = lambda v: v.reshape(1, -1)
    trig = _rope_table(seq)
    for l in range(mods.shape[0]):
        i = l // 3
        kind = l % 3
        mod = mods[l]
        gpre = prm['norm_pre'][l]
        gpost = prm['norm_post'][l]
        x = _ffn(x, mod, row(gpre[0]), row(gpost[0]), prm['ffn_w_in'][l][0], prm['ffn_w_out'][l][0],
                 sub=0, seq=seq, tm=tm)
        if kind == 0:
            wd, wq, wkv = prm['mla'][i]
            qt, k, vt = _mla_proj(x, mod, row(gpre[1]), trig, wd, row(prm['mla_q_norm'][i]),
                                  row(prm['mla_kv_norm'][i]), wq, wkv, seq=seq, tm=tm)
            o = _flash(qt, k.reshape(batch, seq, -1), vt)
            mixed = (o.reshape(t, -1), prm['mla_w_out'][i], jnp.zeros((1, d), F32))
        elif kind == 1:
            g = prm['gdn'][i]
            q, k, v, gate, gc = _gdn_proj(x, mod, row(gpre[1]), g['wqkv'], g['wg'], g['wab'], g['conv'],
                                          g['aneg'], g['dtb'], seq=seq, tm=_pick(seq, 256))
            nh = GDN_HEADS
            gc3 = gc.reshape(batch, seq, LANES)
            outs = []
            for dr in range(2):
                gcol = gc3[:, :, dr * nh:(dr + 1) * nh]
                bcol = gc3[:, :, 2 * nh + dr * nh:2 * nh + (dr + 1) * nh]
                grow = jnp.swapaxes(gcol.reshape(batch, seq // GDN_CHUNK, GDN_CHUNK, nh), 2, 3)
                outs.append(_gdn_delta(q, k, v, gcol, bcol, grow, batch=batch, seq=seq,
                                       tb=_pick(seq, 256), reverse=bool(dr)))
            x = _gdn_out(outs[0], outs[1], gate, x, mod, row(gpost[1]), row(prm['gdn_o_norm'][i]),
                         g['wout'], seq=seq, tm=tm)
            mixed = None
        else:
            cs, m1, m2, tw = _fnet_tables(seq, d // FNET_GROUPS)
            s2 = min(seq, LANES)
            s1 = seq // s2
            ab = _fnet_chan(x, mod, row(gpre[1]), cs, batch=batch, seq=seq, tm=tm)
            y = _fnet_s1(m1, ab.reshape(batch, 2, s1, s2 // 8, 8, d))
            f = _fnet_s2(m2, tw, y.reshape(batch, 2, seq, d), s1=s1, s2=s2)
            mixed = (f.reshape(t, d), prm['fnet_w_out'][i], row(prm['fnet_b_out'][i]))
        w_in2, w_out2 = prm['ffn_w_in'][l][1], prm['ffn_w_out'][l][1]
        if mixed is None:
            x = _ffn(x, mod, row(gpre[2]), row(gpost[2]), w_in2, w_out2, sub=2, seq=seq, tm=tm)
        else:
            a, w_mix, b_mix = mixed
            x = _mix_ffn(a, x, mod, row(gpost[1]), w_mix, b_mix, row(gpre[2]), row(gpost[2]),
                         w_in2, w_out2, seq=seq, tm=tm)
    return x.reshape(batch, seq, d)


def _prep_gdn(w_in, conv, a_log, dt_bias, w_out):
    n_conv = 2 * GDN_QK + GDN_VW
    wab = jnp.pad(w_in[:, n_conv + GDN_VW:], ((0, 0), (0, LANES - 4 * GDN_HEADS)))
    pad16 = lambda v: jnp.pad(v.reshape(1, -1).astype(F32), ((0, 0), (0, LANES - 2 * GDN_HEADS)))
    return {
        'wqkv': w_in[:, :n_conv].astype(BF16),
        'wg': w_in[:, n_conv:n_conv + GDN_VW].astype(BF16),
        'wab': wab.astype(BF16),
        'conv': conv.astype(F32),
        'aneg': pad16(-jnp.exp(a_log.astype(F32))),
        'dtb': pad16(dt_bias),
        'wout': w_out.astype(BF16),
    }


def kernel(x_prompt, x_sample, c_prompt, c_sample, w_ada, b_ada, norm_pre, norm_post, ffn_w_in, ffn_w_out, mla_w_down, mla_q_norm, mla_kv_norm, mla_w_uq, mla_w_ukv, mla_w_out, gdn_w_in, gdn_conv, gdn_a_log, gdn_dt_bias, gdn_o_norm, gdn_w_out, fnet_w_out, fnet_b_out):
    d = x_prompt.shape[-1]
    prm = {
        'norm_pre': norm_pre, 'norm_post': norm_post,
        'ffn_w_in': ffn_w_in.astype(BF16), 'ffn_w_out': ffn_w_out.astype(BF16),
        'mla': [_prep_mla(mla_w_down[i], mla_w_uq[i], mla_w_ukv[i]) for i in range(mla_w_down.shape[0])],
        'mla_q_norm': mla_q_norm, 'mla_kv_norm': mla_kv_norm, 'mla_w_out': mla_w_out.astype(BF16),
        'gdn': [_prep_gdn(gdn_w_in[i], gdn_conv[i], gdn_a_log[i], gdn_dt_bias[i], gdn_w_out[i])
                for i in range(gdn_w_in.shape[0])],
        'gdn_o_norm': gdn_o_norm,
        'fnet_w_out': fnet_w_out.astype(BF16), 'fnet_b_out': fnet_b_out,
    }
    nbp = c_prompt.shape[0]
    c_all = jnp.concatenate([c_prompt, c_sample], axis=0)
    mods = _ada_mod(c_all, w_ada, b_ada).reshape(w_ada.shape[0], c_all.shape[0], 3 * N_SUB, d)
    y_prompt = _trunk(x_prompt, mods[:, :nbp], prm)
    y_sample = _trunk(x_sample, mods[:, nbp:], prm)
    return (y_prompt, y_sample)
```
